```python
import math
import jax, jax.numpy as jnp
from jax import lax
import numpy as np

D_MODEL = 1024
BATCH = 8
SEQ = 2048
DEPTH = 4
DEC_BATCH = 128
DEC_SEQ = 1
PAST_LEN = 16384
PAGE_SIZE = 128

BRANCH_WIDTH = 512
N_BRANCH = 3
R_HEAD = 64
R_HEADS = BRANCH_WIDTH // R_HEAD
R_WIDTH = R_HEADS * R_HEAD
R_LORA_W = 64
R_LORA_A = 64
R_LORA_G = 128
R_COLS = 3 * R_WIDTH + R_LORA_W + R_LORA_A + R_LORA_G
R_LN_EPS = 64e-5
S5_GROUP = 16
S5_WIDTH = BRANCH_WIDTH
S5_GROUPS = S5_WIDTH // S5_GROUP
S5_STATE = 64
M_HEAD = 64
M_WIDTH = BRANCH_WIDTH
M_HEADS = M_WIDTH // M_HEAD
M_GROUPS = 2
M_STATE = 128
M_CONV = 4
M_CONV_CH = M_WIDTH + 2 * M_GROUPS * M_STATE
M_COLS = M_WIDTH + M_CONV_CH + M_HEADS
M_CHUNK = 128
M_EPS = 1e-5
GATE_COLS = N_BRANCH * D_MODEL
IN_COLS = R_COLS + S5_WIDTH + M_COLS + GATE_COLS
D_FF = 2816
FFN_CONV = 3
EPS = 1e-6

kernel_name = "hybrid_rwkv7_s5_ssd_convglu_step"


def rmsnorm(x, g, eps=EPS):
    xf = x.astype(jnp.float32)
    y = xf * lax.rsqrt(jnp.mean(xf * xf, axis=-1, keepdims=True) + eps)
    return (y * g.astype(jnp.float32)).astype(x.dtype)


def causal_dwconv(u, buf, w, b):
    K = w.shape[-1]
    T = u.shape[1]
    up = jnp.concatenate([buf.astype(u.dtype), u], axis=1)
    y = up[:, 0:T] * w[:, 0]
    for k in range(1, K):
        y = y + up[:, k:k + T] * w[:, k]
    return y + b, up[:, T:]


def rwkv7_mix(z, shift0, wkv0, mu, w0, w2, a0, a2, g2, k_k, k_a, r_k, ln_w, ln_b):
    f32 = jnp.float32
    Bsz, T, _ = z.shape
    prev = jnp.concatenate([shift0[:, None].astype(z.dtype), z[:, :-1]], axis=1)
    zm = z + (prev - z) * mu
    r, k, v, dw, da, dg = jnp.split(zm, [R_WIDTH, 2 * R_WIDTH, 3 * R_WIDTH,
                                         3 * R_WIDTH + R_LORA_W, 3 * R_WIDTH + R_LORA_W + R_LORA_A], axis=-1)
    logw = -jax.nn.softplus(-(w0 + jnp.tanh(dw) @ w2).astype(f32)) - 0.5
    decay = jnp.exp(-jnp.exp(logw))
    a = jax.nn.sigmoid((a0 + da @ a2).astype(f32))
    g = jax.nn.sigmoid(dg) @ g2
    heads = lambda t: t.astype(f32).reshape(Bsz, T, R_HEADS, R_HEAD)
    r, k, v, a, decay = heads(r), heads(k), heads(v), heads(a), heads(decay)
    kk = k * k_k.astype(f32).reshape(R_HEADS, R_HEAD)
    kk = kk / jnp.maximum(jnp.linalg.norm(kk, axis=-1, keepdims=True), 1e-12)
    k = k * (1.0 + (a - 1.0) * k_a.astype(f32).reshape(R_HEADS, R_HEAD))
    tm = lambda t: jnp.moveaxis(t, 1, 0)

    def step(S, inp):
        r_t, w_t, k_t, v_t, kk_t, b_t = inp
        sa = jnp.einsum('bhij,bhj->bhi', S, kk_t)
        S = (S * w_t[:, :, None, :] - sa[..., None] * b_t[:, :, None, :]
             + v_t[..., None] * k_t[:, :, None, :])
        return S, jnp.einsum('bhij,bhj->bhi', S, r_t)

    S_T, y = lax.scan(step, wkv0.astype(f32), (tm(r), tm(decay), tm(k), tm(v), tm(kk), tm(kk * a)))
    y = jnp.moveaxis(y, 0, 1)
    mean = jnp.mean(y, axis=-1, keepdims=True)
    var = jnp.mean(jnp.square(y - mean), axis=-1, keepdims=True)
    y = ((y - mean) * lax.rsqrt(var + R_LN_EPS)).reshape(Bsz, T, R_WIDTH) * ln_w + ln_b
    bonus = jnp.sum(r * k * r_k.astype(f32), axis=-1, keepdims=True) * v
    y = (y + bonus.reshape(Bsz, T, R_WIDTH)) * g.astype(f32)
    return y, z[:, -1], S_T


def s5_mix(u, st_re, st_im, lam_re, lam_im, log_step, b_re, b_im, c_re, c_im, d_skip, w_glu, b_glu):
    f32 = jnp.float32
    Bsz, T, _ = u.shape
    uf = u.astype(f32).reshape(Bsz, T, S5_GROUPS, S5_GROUP)
    lr, li = lam_re.astype(f32), lam_im.astype(f32)
    delta = jnp.exp(log_step.astype(f32))[:, None]
    mag = jnp.exp(lr * delta)
    ab_re, ab_im = mag * jnp.cos(li * delta), mag * jnp.sin(li * delta)
    den = lr * lr + li * li
    nr, ni = ab_re - 1.0, ab_im
    cf_re = (nr * lr + ni * li) / den
    cf_im = (ni * lr - nr * li) / den
    br, bi = b_re.astype(f32), b_im.astype(f32)
    bb_re = cf_re[..., None] * br - cf_im[..., None] * bi
    bb_im = cf_re[..., None] * bi + cf_im[..., None] * br
    bu_re = jnp.einsum('gph,btgh->btgp', bb_re, uf)
    bu_im = jnp.einsum('gph,btgh->btgp', bb_im, uf)
    s_re, s_im = st_re.astype(f32), st_im.astype(f32)
    bu_re = bu_re.at[:, 0].add(ab_re * s_re - ab_im * s_im)
    bu_im = bu_im.at[:, 0].add(ab_re * s_im + ab_im * s_re)
    a_re = jnp.broadcast_to(ab_re, bu_re.shape)
    a_im = jnp.broadcast_to(ab_im, bu_im.shape)

    def combine(e1, e2):
        a1r, a1i, b1r, b1i = e1
        a2r, a2i, b2r, b2i = e2
        return (a2r * a1r - a2i * a1i, a2r * a1i + a2i * a1r,
                a2r * b1r - a2i * b1i + b2r, a2r * b1i + a2i * b1r + b2i)

    _, _, xr, xi = lax.associative_scan(combine, (a_re, a_im, bu_re, bu_im), axis=1)
    y = (jnp.einsum('ghp,btgp->btgh', c_re.astype(f32), xr)
         - jnp.einsum('ghp,btgp->btgh', c_im.astype(f32), xi)
         + d_skip.astype(f32).reshape(S5_GROUPS, S5_GROUP) * uf)
    y = jax.nn.gelu(y.reshape(Bsz, T, S5_WIDTH))
    y = y * jax.nn.sigmoid(y @ w_glu.astype(f32) + b_glu.astype(f32))
    return y, xr[:, -1], xi[:, -1]


def segsum(x):
    L = x.shape[-1]
    cs = jnp.cumsum(x, axis=-1)
    diff = cs[..., :, None] - cs[..., None, :]
    return jnp.where(jnp.tril(jnp.ones((L, L), bool)), diff, -jnp.inf)


def ssd(x, dt, A, Bm, Cm, h0):
    Bsz, T, H, P = x.shape
    G, N = Bm.shape[2], Bm.shape[3]
    E = H // G
    Q = min(M_CHUNK, T)
    nc = -(-T // Q)
    pad = nc * Q - T
    if pad:
        pw = lambda t: jnp.pad(t, [(0, 0), (0, pad)] + [(0, 0)] * (t.ndim - 2))
        x, dt, Bm, Cm = pw(x), pw(dt), pw(Bm), pw(Cm)
    xd = (x * dt[..., None]).reshape(Bsz, nc, Q, G, E, P)
    dA = jnp.transpose((dt * A).reshape(Bsz, nc, Q, G, E), (0, 1, 3, 4, 2))
    Bc = Bm.reshape(Bsz, nc, Q, G, N)
    Cc = Cm.reshape(Bsz, nc, Q, G, N)
    cs = jnp.cumsum(dA, axis=-1)
    Lm = jnp.exp(segsum(dA))
    y_diag = jnp.einsum('bclgn,bcsgn,bcgels,bcsgep->bclgep', Cc, Bc, Lm, xd)
    decay_states = jnp.exp(cs[..., -1:] - cs)
    states = jnp.einsum('bclgn,bcgel,bclgep->bcgepn', Bc, decay_states, xd)
    states = jnp.concatenate([h0.reshape(Bsz, 1, G, E, P, N), states], axis=1)
    last = jnp.pad(jnp.moveaxis(cs[..., -1], 1, -1), [(0, 0), (0, 0), (0, 0), (1, 0)])
    chunk_decay = jnp.exp(segsum(last))
    new_states = jnp.einsum('bgezc,bcgepn->bzgepn', chunk_decay, states)
    y_off = jnp.einsum('bclgn,bcgepn,bcgel->bclgep', Cc, new_states[:, :-1], jnp.exp(cs))
    y = (y_diag + y_off).reshape(Bsz, nc * Q, H, P)[:, :T]
    return y, new_states[:, -1].reshape(Bsz, H, P, N)


def mamba2_mix(zm, conv_buf, h0, conv_w, conv_b, dt_bias, a_log, d_skip, norm_w):
    f32 = jnp.float32
    Bsz, T, _ = zm.shape
    z, xbc, dt = jnp.split(zm, [M_WIDTH, M_WIDTH + M_CONV_CH], axis=-1)
    xbc_c, new_buf = causal_dwconv(xbc, conv_buf, conv_w, conv_b)
    xbc_c = jax.nn.silu(xbc_c.astype(f32))
    xs, Bm, Cm = jnp.split(xbc_c, [M_WIDTH, M_WIDTH + M_GROUPS * M_STATE], axis=-1)
    dt = jax.nn.softplus(dt.astype(f32) + dt_bias.astype(f32))
    A = -jnp.exp(a_log.astype(f32))
    xs = xs.reshape(Bsz, T, M_HEADS, M_HEAD)
    y, hT = ssd(xs, dt, A, Bm.reshape(Bsz, T, M_GROUPS, M_STATE),
                Cm.reshape(Bsz, T, M_GROUPS, M_STATE), h0.astype(f32))
    y = y + d_skip.astype(f32)[:, None] * xs
    y = y.reshape(Bsz, T, M_WIDTH) * jax.nn.silu(z.astype(f32))
    yg = y.reshape(Bsz, T, M_GROUPS, M_WIDTH // M_GROUPS)
    yg = yg * lax.rsqrt(jnp.mean(yg * yg, axis=-1, keepdims=True) + M_EPS)
    return yg.reshape(Bsz, T, M_WIDTH) * norm_w.astype(f32), new_buf, hT


def _layer(x, states, lw):
    shift0, wkv0, s5r0, s5i0, ssd0, ssdbuf0, ffnbuf0 = states
    (g_pre_mix, g_post_mix, g_pre_ffn, g_post_ffn, w_in,
     r_mu, r_w0, r_w2, r_a0, r_a2, r_g2, r_kk, r_ka, r_rk, r_ln_w, r_ln_b,
     s5_lam_re, s5_lam_im, s5_log_step, s5_b_re, s5_b_im, s5_c_re, s5_c_im, s5_d, s5_w_glu, s5_b_glu,
     m_conv_w, m_conv_b, m_dt_bias, m_a_log, m_d, m_norm_w,
     w_branch, w_out, w_up, f_conv_w, f_conv_b, w_down) = lw
    Bsz, T, _ = x.shape
    h = rmsnorm(x, g_pre_mix)
    z = h @ w_in
    z_r, z_s, z_m, z_g = jnp.split(z, [R_COLS, R_COLS + S5_WIDTH, R_COLS + S5_WIDTH + M_COLS], axis=-1)
    o_r, shift1, wkv1 = rwkv7_mix(z_r, shift0, wkv0, r_mu, r_w0, r_w2, r_a0, r_a2, r_g2,
                                  r_kk, r_ka, r_rk, r_ln_w, r_ln_b)
    o_s, s5r1, s5i1 = s5_mix(z_s, s5r0, s5i0, s5_lam_re, s5_lam_im, s5_log_step, s5_b_re, s5_b_im,
                             s5_c_re, s5_c_im, s5_d, s5_w_glu, s5_b_glu)
    o_m, ssdbuf1, ssd1 = mamba2_mix(z_m, ssdbuf0, ssd0, m_conv_w, m_conv_b, m_dt_bias, m_a_log,
                                    m_d, m_norm_w)
    branches = jnp.stack([o_r, o_s, o_m], axis=2).astype(x.dtype)
    proj = jnp.einsum('btkc,kcd->btkd', branches, w_branch)
    gates = jax.nn.sigmoid(z_g.reshape(Bsz, T, N_BRANCH, D_MODEL))
    mixed = jnp.einsum('btkd,btkd->btd', gates, proj) @ w_out
    x = x + rmsnorm(mixed, g_post_mix)
    h2 = rmsnorm(x, g_pre_ffn)
    gate, val = jnp.split(h2 @ w_up, [D_FF], axis=-1)
    gc, ffnbuf1 = causal_dwconv(gate, ffnbuf0, f_conv_w, f_conv_b)
    f = (jax.nn.gelu(gc) * val) @ w_down
    x = x + rmsnorm(f, g_post_ffn)
    return x, (shift1, wkv1, s5r1, s5i1, ssd1, ssdbuf1, ffnbuf1)


def _zero_states(n, dtype):
    return (jnp.zeros((n, R_COLS), dtype),
            jnp.zeros((n, R_HEADS, R_HEAD, R_HEAD), jnp.float32),
            jnp.zeros((n, S5_GROUPS, S5_STATE), jnp.float32),
            jnp.zeros((n, S5_GROUPS, S5_STATE), jnp.float32),
            jnp.zeros((n, M_HEADS, M_HEAD, M_STATE), jnp.float32),
            jnp.zeros((n, M_CONV - 1, M_CONV_CH), dtype),
            jnp.zeros((n, FFN_CONV - 1, D_FF), dtype))


def setup_inputs(seed: int = 0) -> dict:
    key = jax.random.key(seed)
    ks = iter(jax.random.split(key, 64))
    f32 = jnp.float32
    nrm = lambda shape, s=1.0: s * jax.random.normal(next(ks), shape, f32)
    unif = lambda shape, lo, hi: jax.random.uniform(next(ks), shape, f32, lo, hi)
    L = DEPTH
    G, H, P = S5_GROUPS, S5_GROUP, S5_STATE
    dt0 = jnp.exp(unif((L, M_HEADS), math.log(1e-3), math.log(1e-1)))
    return {
        "x_prompt": nrm((BATCH, SEQ, D_MODEL)),
        "x_sample": nrm((DEC_BATCH, DEC_SEQ, D_MODEL)),
        "state_rwkv_shift": nrm((L, DEC_BATCH, R_COLS)),
        "state_rwkv_wkv": nrm((L, DEC_BATCH, R_HEADS, R_HEAD, R_HEAD)),
        "state_s5_re": nrm((L, DEC_BATCH, G, P), 0.5),
        "state_s5_im": nrm((L, DEC_BATCH, G, P), 0.5),
        "state_ssd": nrm((L, DEC_BATCH, M_HEADS, M_HEAD, M_STATE), 0.5),
        "state_ssd_conv": nrm((L, DEC_BATCH, M_CONV - 1, M_CONV_CH)),
        "state_ffn_conv": nrm((L, DEC_BATCH, FFN_CONV - 1, D_FF)),
        "g_pre_mix": 1.0 + nrm((L, D_MODEL), 0.02),
        "g_post_mix": 1.0 + nrm((L, D_MODEL), 0.02),
        "g_pre_ffn": 1.0 + nrm((L, D_MODEL), 0.02),
        "g_post_ffn": 1.0 + nrm((L, D_MODEL), 0.02),
        "w_in": nrm((L, D_MODEL, IN_COLS), D_MODEL ** -0.5),
        "r_mu": unif((L, R_COLS), 0.0, 1.0),
        "r_w0": unif((L, R_WIDTH), -5.0, 1.0),
        "r_w2": nrm((L, R_LORA_W, R_WIDTH), 0.1),
        "r_a0": nrm((L, R_WIDTH), 0.1),
        "r_a2": nrm((L, R_LORA_A, R_WIDTH), 0.1),
        "r_g2": nrm((L, R_LORA_G, R_WIDTH), R_LORA_G ** -0.5),
        "r_kk": 0.85 + nrm((L, R_WIDTH), 0.05),
        "r_ka": 1.0 + nrm((L, R_WIDTH), 0.05),
        "r_rk": nrm((L, R_HEADS, R_HEAD), 0.1),
        "r_ln_w": 1.0 + nrm((L, R_WIDTH), 0.02),
        "r_ln_b": nrm((L, R_WIDTH), 0.02),
        "s5_lam_re": -0.5 + nrm((L, G, P), 0.01),
        "s5_lam_im": jnp.pi * jnp.arange(P, dtype=f32) + nrm((L, G, P), 0.01),
        "s5_log_step": unif((L, G), math.log(1e-3), math.log(1e-1)),
        "s5_b_re": nrm((L, G, P, H), (2 * H) ** -0.5),
        "s5_b_im": nrm((L, G, P, H), (2 * H) ** -0.5),
        "s5_c_re": nrm((L, G, H, P), (2 * P) ** -0.5),
        "s5_c_im": nrm((L, G, H, P), (2 * P) ** -0.5),
        "s5_d": nrm((L, S5_WIDTH)),
        "s5_w_glu": nrm((L, S5_WIDTH, S5_WIDTH), S5_WIDTH ** -0.5),
        "s5_b_glu": nrm((L, S5_WIDTH), 0.02),
        "m_conv_w": nrm((L, M_CONV_CH, M_CONV), M_CONV ** -0.5),
        "m_conv_b": nrm((L, M_CONV_CH), 0.02),
        "m_dt_bias": dt0 + jnp.log(-jnp.expm1(-dt0)),
        "m_a_log": jnp.log(unif((L, M_HEADS), 1.0, 16.0)),
        "m_d": 1.0 + nrm((L, M_HEADS), 0.1),
        "m_norm_w": 1.0 + nrm((L, M_WIDTH), 0.02),
        "w_branch": nrm((L, N_BRANCH, BRANCH_WIDTH, D_MODEL), BRANCH_WIDTH ** -0.5),
        "w_out": nrm((L, D_MODEL, D_MODEL), D_MODEL ** -0.5),
        "w_up": nrm((L, D_MODEL, 2 * D_FF), D_MODEL ** -0.5),
        "f_conv_w": nrm((L, D_FF, FFN_CONV), FFN_CONV ** -0.5),
        "f_conv_b": nrm((L, D_FF), 0.02),
        "w_down": nrm((L, D_FF, D_MODEL), D_FF ** -0.5),
    }


def reference(x_prompt, x_sample, state_rwkv_shift, state_rwkv_wkv, state_s5_re, state_s5_im,
              state_ssd, state_ssd_conv, state_ffn_conv,
              g_pre_mix, g_post_mix, g_pre_ffn, g_post_ffn, w_in,
              r_mu, r_w0, r_w2, r_a0, r_a2, r_g2, r_kk, r_ka, r_rk, r_ln_w, r_ln_b,
              s5_lam_re, s5_lam_im, s5_log_step, s5_b_re, s5_b_im, s5_c_re, s5_c_im, s5_d,
              s5_w_glu, s5_b_glu,
              m_conv_w, m_conv_b, m_dt_bias, m_a_log, m_d, m_norm_w,
              w_branch, w_out, w_up, f_conv_w, f_conv_b, w_down):
    stacked = (g_pre_mix, g_post_mix, g_pre_ffn, g_post_ffn, w_in,
               r_mu, r_w0, r_w2, r_a0, r_a2, r_g2, r_kk, r_ka, r_rk, r_ln_w, r_ln_b,
               s5_lam_re, s5_lam_im, s5_log_step, s5_b_re, s5_b_im, s5_c_re, s5_c_im, s5_d,
               s5_w_glu, s5_b_glu,
               m_conv_w, m_conv_b, m_dt_bias, m_a_log, m_d, m_norm_w,
               w_branch, w_out, w_up, f_conv_w, f_conv_b, w_down)
    cache_in = (state_rwkv_shift, state_rwkv_wkv, state_s5_re, state_s5_im,
                state_ssd, state_ssd_conv, state_ffn_conv)
    y_prompt, y_sample = x_prompt, x_sample
    new_p = [[] for _ in cache_in]
    new_s = [[] for _ in cache_in]
    for l in range(DEPTH):
        lw = tuple(a[l] for a in stacked)
        y_prompt, sp = _layer(y_prompt, _zero_states(x_prompt.shape[0], x_prompt.dtype), lw)
        y_sample, ss = _layer(y_sample, tuple(c[l] for c in cache_in), lw)
        for i in range(len(cache_in)):
            new_p[i].append(sp[i])
            new_s[i].append(ss[i])
    p_shift, p_wkv, p_s5_re, p_s5_im, p_ssd, p_ssd_conv, p_ffn_conv = [jnp.stack(v, 0) for v in new_p]
    s_shift, s_wkv, s_s5_re, s_s5_im, s_ssd, s_ssd_conv, s_ffn_conv = [jnp.stack(v, 0) for v in new_s]
    return (y_prompt, y_sample, p_shift, s_shift, p_wkv, s_wkv, p_s5_re, s_s5_re, p_s5_im, s_s5_im,
            p_ssd, s_ssd, p_ssd_conv, s_ssd_conv, p_ffn_conv, s_ffn_conv)
```

```python
import functools

import jax
import jax.numpy as jnp
from jax import lax
from jax.experimental import pallas as pl
from jax.experimental.pallas import tpu as pltpu

F32 = jnp.float32
BF16 = jnp.bfloat16

D_MODEL = 1024
WIDTH = 512
R_HEADS, R_HEAD = 8, 64
R_COLS = 1792
R_LN_EPS = 64e-5
S5_GROUPS, S5_GROUP, S5_STATE = 32, 16, 64
S5_LANES = S5_GROUPS * S5_STATE
M_HEADS, M_HEAD, M_GROUPS, M_STATE = 8, 64, 2, 128
M_CONV, M_CONV_CH = 4, 1024
M_DT_PAD = 256
M_COLS_PAD = WIDTH + M_CONV_CH + M_DT_PAD
M_EPS = 1e-5
D_FF = 2816
FFN_CONV = 3
EPS = 1e-6
SUBLANES = 8
LANES = 128

RWKV_CHUNK = 64
SSD_CHUNK = 128
S5_CHUNK = 64
TOKEN_TILE = 256


def _cp(sem, vmem_mb=48):
    return pltpu.CompilerParams(dimension_semantics=sem, vmem_limit_bytes=vmem_mb * 1024 * 1024)


def _bf(x):
    return x.astype(BF16)


def _dot(a, b):
    return jnp.dot(_bf(a), _bf(b), preferred_element_type=F32)


def _dot_nt(a, b):
    return lax.dot_general(_bf(a), _bf(b), (((1,), (1,)), ((), ())), preferred_element_type=F32)


def _dot_tn(a, b):
    return lax.dot_general(_bf(a), _bf(b), (((0,), (0,)), ((), ())), preferred_element_type=F32)


def _split(x, terms):
    out = []
    for _ in range(terms - 1):
        h = _bf(x)
        out.append(h)
        x = x - h.astype(F32)
    out.append(_bf(x))
    return out


def _dot_exact_lhs(m_bf16, x, terms=3):
    acc = None
    for h in _split(x, terms):
        p = jnp.dot(m_bf16, h, preferred_element_type=F32)
        acc = p if acc is None else acc + p
    return acc


def _dot_exact_rhs(x, m_bf16, terms=2):
    acc = None
    for h in _split(x, terms):
        p = jnp.dot(h, m_bf16, preferred_element_type=F32)
        acc = p if acc is None else acc + p
    return acc


def _softplus(x):
    return jnp.maximum(x, 0.0) + jnp.log1p(jnp.exp(-jnp.abs(x)))


def _rms(x, g, eps=EPS):
    return x * lax.rsqrt(jnp.mean(x * x, axis=-1, keepdims=True) + eps) * g


def _delayed(x, carry8, k):
    rx = pltpu.roll(x, k, 0)
    rc = pltpu.roll(carry8, k, 0)
    row = lax.broadcasted_iota(jnp.int32, (SUBLANES, x.shape[1]), 0)
    head = jnp.where(row < k, rc, rx[:SUBLANES])
    if x.shape[0] == SUBLANES:
        return head
    return jnp.concatenate([head, rx[SUBLANES:]], axis=0)


def _norm_matmul_kernel(x_ref, g_ref, w_ref, o_ref, *, cn):
    hb = _bf(_rms(x_ref[...], g_ref[...]))
    for c in range(0, o_ref.shape[1], cn):
        o_ref[:, c:c + cn] = jnp.dot(hb, w_ref[:, c:c + cn], preferred_element_type=F32)


def _norm_matmul(x, g, w, name):
    n, d = x.shape
    c = w.shape[1]
    tm = min(TOKEN_TILE, n)
    cn = 256
    return pl.pallas_call(
        functools.partial(_norm_matmul_kernel, cn=cn),
        grid=(n // tm,),
        in_specs=[pl.BlockSpec((tm, d), lambda i: (i, 0)),
                  pl.BlockSpec((1, d), lambda i: (0, 0)),
                  pl.BlockSpec((d, c), lambda i: (0, 0))],
        out_specs=pl.BlockSpec((tm, c), lambda i: (i, 0)),
        out_shape=jax.ShapeDtypeStruct((n, c), F32),
        compiler_params=_cp(("arbitrary",)),
        name=name,
    )(x, g, w)


def _rwkv_prep_math(z, prev, mu, w0, w2, a0, a2, g2, kkw, kaw, ones_bd):
    zm = z + (prev - z) * mu
    r = zm[:, 0:WIDTH]
    k = zm[:, WIDTH:2 * WIDTH]
    v = zm[:, 2 * WIDTH:3 * WIDTH]
    dw = zm[:, 1536:1600]
    da = zm[:, 1600:1664]
    dg = zm[:, 1664:1792]
    logw = -_softplus(-(w0 + _dot(jnp.tanh(dw), w2))) - 0.5
    ld = -jnp.exp(logw)
    a = jax.nn.sigmoid(a0 + _dot(da, a2))
    g = _dot(jax.nn.sigmoid(dg), g2)
    kk = k * kkw
    ss = _dot_exact_rhs(kk * kk, ones_bd)
    kk = kk / jnp.maximum(jnp.sqrt(ss), 1e-12)
    k2 = k * (1.0 + (a - 1.0) * kaw)
    return r, ld, k2, v, kk, kk * a, g


def _rwkv_prep_seq_kernel(z_ref, init_ref, mu_ref, w0_ref, w2_ref, a0_ref, a2_ref, g2_ref, kkw_ref, kaw_ref,
                          ones_ref, r_ref, ld_ref, k_ref, v_ref, kk_ref, b_ref, g_ref, last_ref, carry):
    @pl.when(pl.program_id(1) == 0)
    def _():
        carry[...] = init_ref[...]

    z = z_ref[...]
    prev = _delayed(z, carry[...], 1)
    tail = z[z.shape[0] - SUBLANES:, :]
    carry[...] = tail
    last_ref[...] = tail
    outs = _rwkv_prep_math(z, prev, mu_ref[...], w0_ref[...], w2_ref[...], a0_ref[...], a2_ref[...],
                           g2_ref[...], kkw_ref[...], kaw_ref[...], ones_ref[...])
    for ref, val in zip((r_ref, ld_ref, k_ref, v_ref, kk_ref, b_ref, g_ref), outs):
        ref[...] = val


def _rwkv_prep_step_kernel(z_ref, prev_ref, mu_ref, w0_ref, w2_ref, a0_ref, a2_ref, g2_ref, kkw_ref, kaw_ref,
                           ones_ref, r_ref, ld_ref, k_ref, v_ref, kk_ref, b_ref, g_ref):
    outs = _rwkv_prep_math(z_ref[...], prev_ref[...], mu_ref[...], w0_ref[...], w2_ref[...], a0_ref[...],
                           a2_ref[...], g2_ref[...], kkw_ref[...], kaw_ref[...], ones_ref[...])
    for ref, val in zip((r_ref, ld_ref, k_ref, v_ref, kk_ref, b_ref, g_ref), outs):
        ref[...] = val


def _full(shape):
    nd = len(shape)
    return pl.BlockSpec(shape, lambda *_: (0,) * nd)


def _rwkv_prep(z, shift, weights, seq):
    wspecs = [_full(w.shape) for w in weights]
    if seq:
        bsz, t, _ = z.shape
        tm = min(TOKEN_TILE, t)
        tile = lambda c: pl.BlockSpec((None, tm, c), lambda b, j: (b, j, 0))
        edge = pl.BlockSpec((None, SUBLANES, R_COLS), lambda b, j: (b, 0, 0))
        return pl.pallas_call(
            _rwkv_prep_seq_kernel,
            grid=(bsz, t // tm),
            in_specs=[tile(R_COLS), edge] + wspecs,
            out_specs=[tile(WIDTH)] * 7 + [edge],
            out_shape=[jax.ShapeDtypeStruct((bsz, t, WIDTH), F32)] * 7
            + [jax.ShapeDtypeStruct((bsz, SUBLANES, R_COLS), F32)],
            scratch_shapes=[pltpu.VMEM((SUBLANES, R_COLS), F32)],
            compiler_params=_cp(("arbitrary", "arbitrary")),
            name="rwkv_prep_seq",
        )(z, shift, *weights)
    n = z.shape[0]
    return pl.pallas_call(
        _rwkv_prep_step_kernel,
        grid=(1,),
        in_specs=[_full((n, R_COLS)), _full((n, R_COLS))] + wspecs,
        out_specs=[_full((n, WIDTH))] * 7,
        out_shape=[jax.ShapeDtypeStruct((n, WIDTH), F32)] * 7,
        compiler_params=_cp(("arbitrary",)),
        name="rwkv_prep_step",
    )(z, shift, *weights)


def _rwkv_post(y, r, k, v, g, rk, lnw, lnb, ones_bd):
    inv = 1.0 / R_HEAD
    mean = _dot_exact_rhs(y, ones_bd) * inv
    d = y - mean
    var = _dot_exact_rhs(d * d, ones_bd) * inv
    yn = d * lax.rsqrt(var + R_LN_EPS) * lnw + lnb
    bonus = _dot_exact_rhs(r * k * rk, ones_bd) * v
    return (yn + bonus) * g


def _rwkv_chunk_kernel(r_ref, ld_ref, k_ref, v_ref, kk_ref, b_ref, g_ref, s0_ref, rk_ref, lnw_ref, lnb_ref,
                       ones_ref, o_ref, st_ref, s_scr):
    j = pl.program_id(1)

    @pl.when(j == 0)
    def _():
        s_scr[...] = s0_ref[...]

    r, ld, k, v, kk, b = (ref[...] for ref in (r_ref, ld_ref, k_ref, v_ref, kk_ref, b_ref))
    L = r.shape[0]
    row = lax.broadcasted_iota(jnp.int32, (L, L), 0)
    col = lax.broadcasted_iota(jnp.int32, (L, L), 1)
    lower = row >= col
    strict = row > col
    cum = _dot_exact_lhs(_bf(lower.astype(F32)), ld)
    wc = jnp.exp(cum)
    winv = jnp.exp(-cum)
    wl = wc[L - 1:L, :]
    r_t = r * wc
    kk_t = kk * jnp.exp(cum - ld)
    k_h = k * winv
    b_h = b * winv
    k_w = k_h * wl
    b_w = b_h * wl
    ys = []
    for h in range(R_HEADS):
        cs = slice(h * R_HEAD, (h + 1) * R_HEAD)
        lhs = jnp.concatenate([kk_t[:, cs], r_t[:, cs]], axis=0)
        rhs = jnp.concatenate([k_h[:, cs], b_h[:, cs]], axis=0)
        a = _dot_nt(lhs, rhs)
        a_kk_k = jnp.where(strict, a[:L, :L], 0.0)
        n = jnp.where(strict, a[:L, L:], 0.0)
        a_r_k = jnp.where(lower, a[L:, :L], 0.0)
        a_r_b = jnp.where(lower, a[L:, L:], 0.0)
        q = -n
        m = n
        p = 2
        while p < L:
            m = _dot(m, m)
            q = q + m + _dot(q, m)
            p *= 2
        s_h = s_scr[h]
        proj = _dot_nt(lhs, s_h)
        vh = v[:, cs]
        rhs_u = proj[:L] + _dot(a_kk_k, vh)
        u = rhs_u + _dot(q, rhs_u)
        ys.append(proj[L:] + _dot(a_r_k, vh) - _dot(a_r_b, u))
        upd = _dot_tn(jnp.concatenate([vh, -u], axis=0),
                      jnp.concatenate([k_w[:, cs], b_w[:, cs]], axis=0))
        s_scr[h] = s_h * wl[:, cs] + upd
    y = jnp.concatenate(ys, axis=1)
    o_ref[...] = _rwkv_post(y, r, k, v, g_ref[...], rk_ref[...], lnw_ref[...], lnb_ref[...], ones_ref[...])

    @pl.when(j == pl.num_programs(1) - 1)
    def _():
        st_ref[...] = s_scr[...]


def _rwkv_chunk(streams, s0, weights):
    bsz, t, _ = streams[0].shape
    L = min(RWKV_CHUNK, t)
    tile = pl.BlockSpec((None, L, WIDTH), lambda b, j: (b, j, 0))
    st = pl.BlockSpec((None, R_HEADS, R_HEAD, R_HEAD), lambda b, j: (b, 0, 0, 0))
    return pl.pallas_call(
        _rwkv_chunk_kernel,
        grid=(bsz, t // L),
        in_specs=[tile] * 7 + [st] + [_full(w.shape) for w in weights],
        out_specs=[tile, st],
        out_shape=[jax.ShapeDtypeStruct((bsz, t, WIDTH), F32),
                   jax.ShapeDtypeStruct((bsz, R_HEADS, R_HEAD, R_HEAD), F32)],
        scratch_shapes=[pltpu.VMEM((R_HEADS, R_HEAD, R_HEAD), F32)],
        compiler_params=_cp(("arbitrary", "arbitrary")),
        name="rwkv_chunk",
    )(*streams, s0, *weights)


def _rows(first, second):
    row = lax.broadcasted_iota(jnp.int32, (SUBLANES, first.shape[1]), 0)
    return jnp.where(row == 0, first, jnp.where(row == 1, second, 0.0))


def _rwkv_step_kernel(r_ref, ld_ref, k_ref, v_ref, kk_ref, b_ref, g_ref, s0_ref, rk_ref, lnw_ref, lnb_ref,
                      ones_ref, o_ref, st_ref, y_scr):
    bt = r_ref.shape[0]
    ones_bd = ones_ref[...]
    r, k, v, b = r_ref[...], k_ref[...], v_ref[...], b_ref[...]
    w = jnp.exp(ld_ref[...])
    wr = w * r
    b_dot_r = _dot_exact_rhs(b * r, ones_bd)
    k_dot_r = _dot_exact_rhs(k * r, ones_bd)

    for i in range(bt):
        one = slice(i, i + 1)
        lhs = _rows(kk_ref[one, :], wr[one])
        sa_rows = []
        for h in range(R_HEADS):
            cs = slice(h * R_HEAD, (h + 1) * R_HEAD)
            sa_rows.append(_dot_nt(lhs[:, cs], s0_ref[i, h]))
        proj = jnp.concatenate(sa_rows, axis=1)
        sa = proj[0:1]
        y_scr[one, :] = proj[1:2] - sa * b_dot_r[one] + v[one] * k_dot_r[one]
        left = _rows(v[one], -sa)
        right = _rows(k[one], b[one])
        l_hi, l_lo = _split(left, 2)
        r_hi, r_lo = _split(right, 2)
        for h in range(R_HEADS):
            cs = slice(h * R_HEAD, (h + 1) * R_HEAD)
            tn = lambda x, y: lax.dot_general(x[:, cs], y[:, cs], (((0,), (0,)), ((), ())),
                                              preferred_element_type=F32)
            upd = tn(l_hi, r_hi) + tn(l_hi, r_lo) + tn(l_lo, r_hi)
            st_ref[i, h] = s0_ref[i, h] * w[one, cs] + upd
    o_ref[...] = _rwkv_post(y_scr[...], r, k, v, g_ref[...], rk_ref[...], lnw_ref[...], lnb_ref[...], ones_bd)


def _rwkv_step(streams, s0, weights):
    n = streams[0].shape[0]
    bt = SUBLANES
    tile = pl.BlockSpec((bt, WIDTH), lambda i: (i, 0))
    st = pl.BlockSpec((bt, R_HEADS, R_HEAD, R_HEAD), lambda i: (i, 0, 0, 0))
    return pl.pallas_call(
        _rwkv_step_kernel,
        grid=(n // bt,),
        in_specs=[tile] * 7 + [st] + [_full(w.shape) for w in weights],
        out_specs=[tile, st],
        out_shape=[jax.ShapeDtypeStruct((n, WIDTH), F32),
                   jax.ShapeDtypeStruct((n, R_HEADS, R_HEAD, R_HEAD), F32)],
        scratch_shapes=[pltpu.VMEM((bt, WIDTH), F32)],
        compiler_params=_cp(("arbitrary",)),
        name="rwkv_step",
    )(*streams, s0, *weights)


def _s5_param_kernel(lr_ref, li_ref, ls_ref, br_ref, bi_ref, abr_ref, abi_ref, bbr_ref, bbi_ref):
    lr, li = lr_ref[...], li_ref[...]
    delta = jnp.exp(ls_ref[...])
    mag = jnp.exp(lr * delta)
    ab_re = mag * jnp.cos(li * delta)
    ab_im = mag * jnp.sin(li * delta)
    den = lr * lr + li * li
    nr, ni = ab_re - 1.0, ab_im
    cf_re = (nr * lr + ni * li) / den
    cf_im = (ni * lr - nr * li) / den
    abr_ref[...] = ab_re
    abi_ref[...] = ab_im
    br, bi = br_ref[...], bi_ref[...]
    bbr_ref[...] = cf_re[:, None, :] * br - cf_im[:, None, :] * bi
    bbi_ref[...] = cf_re[:, None, :] * bi + cf_im[:, None, :] * br


def _s5_params(lam_re, lam_im, log_step, b_re, b_im):
    g, p, h = b_re.shape
    gp = jax.ShapeDtypeStruct((g, p), F32)
    ghp = jax.ShapeDtypeStruct((g, h, p), F32)
    args = (lam_re, lam_im, log_step.reshape(g, 1), jnp.swapaxes(b_re, 1, 2), jnp.swapaxes(b_im, 1, 2))
    return pl.pallas_call(
        _s5_param_kernel,
        grid=(1,),
        in_specs=[_full(a.shape) for a in args],
        out_specs=[_full((g, p)), _full((g, p)), _full((g, h, p)), _full((g, h, p))],
        out_shape=[gp, gp, ghp, ghp],
        compiler_params=_cp(("arbitrary",)),
        name="s5_params",
    )(*args)


def _s5_readout(u, xr, xi, wcr, wci, d, wg, bg):
    y = _dot(xr, wcr) - _dot(xi, wci) + d * u
    y = jax.nn.gelu(y)
    return y * jax.nn.sigmoid(_dot(y, wg) + bg)


def _s5_seq_kernel(u_ref, s0r_ref, s0i_ref, wbr_ref, wbi_ref, wcr_ref, wci_ref, ar_ref, ai_ref, d_ref, wg_ref,
                   bg_ref, o_ref, str_ref, sti_ref, xr_scr, xi_scr, cr_scr, ci_scr, *, bt, tc):
    @pl.when(pl.program_id(0) == 0)
    def _():
        cr_scr[...] = s0r_ref[...]
        ci_scr[...] = s0i_ref[...]

    n_tiles = S5_LANES // LANES
    u = u_ref[...].reshape(bt * tc, WIDTH)
    ub = _bf(u)
    bu_re = jnp.dot(ub, wbr_ref[...], preferred_element_type=F32)
    bu_im = jnp.dot(ub, wbi_ref[...], preferred_element_type=F32)
    for c in range(n_tiles):
        xr_scr[c] = bu_re[:, c * LANES:(c + 1) * LANES]
        xi_scr[c] = bu_im[:, c * LANES:(c + 1) * LANES]

    for c in range(n_tiles):
        lanes = slice(c * LANES, (c + 1) * LANES)
        ar = jnp.broadcast_to(ar_ref[:, lanes], (bt, LANES))
        ai = jnp.broadcast_to(ai_ref[:, lanes], (bt, LANES))

        def body(t, carry, c=c, ar=ar, ai=ai):
            xr, xi = carry
            rows = pl.ds(t, bt, stride=tc)
            nr = ar * xr - ai * xi + xr_scr[c, rows, :]
            ni = ar * xi + ai * xr + xi_scr[c, rows, :]
            xr_scr[c, rows, :] = nr
            xi_scr[c, rows, :] = ni
            return nr, ni

        xr, xi = lax.fori_loop(0, tc, body, (cr_scr[:, lanes], ci_scr[:, lanes]))
        cr_scr[:, lanes] = xr
        ci_scr[:, lanes] = xi
    str_ref[...] = cr_scr[...]
    sti_ref[...] = ci_scr[...]
    xr_all = jnp.concatenate([xr_scr[c] for c in range(n_tiles)], axis=1)
    xi_all = jnp.concatenate([xi_scr[c] for c in range(n_tiles)], axis=1)
    y = _s5_readout(u, xr_all, xi_all, wcr_ref[...], wci_ref[...], d_ref[...], wg_ref[...], bg_ref[...])
    o_ref[...] = y.reshape(bt, tc, WIDTH)


def _s5_step_kernel(u_ref, s0r_ref, s0i_ref, wbr_ref, wbi_ref, wcr_ref, wci_ref, ar_ref, ai_ref, d_ref, wg_ref,
                    bg_ref, o_ref, str_ref, sti_ref):
    u = u_ref[...]
    ub = _bf(u)
    ar, ai = ar_ref[...], ai_ref[...]
    sr, si = s0r_ref[...], s0i_ref[...]
    xr = ar * sr - ai * si + jnp.dot(ub, wbr_ref[...], preferred_element_type=F32)
    xi = ar * si + ai * sr + jnp.dot(ub, wbi_ref[...], preferred_element_type=F32)
    str_ref[...] = xr
    sti_ref[...] = xi
    o_ref[...] = _s5_readout(u, xr, xi, wcr_ref[...], wci_ref[...], d_ref[...], wg_ref[...], bg_ref[...])


def _s5(u, s0r, s0i, weights, seq):
    wspecs = [_full(w.shape) for w in weights]
    if seq:
        bsz, t, _ = u.shape
        tc = min(S5_CHUNK, t)
        st = _full((bsz, S5_LANES))
        tile = pl.BlockSpec((bsz, tc, WIDTH), lambda j: (0, j, 0))
        return pl.pallas_call(
            functools.partial(_s5_seq_kernel, bt=bsz, tc=tc),
            grid=(t // tc,),
            in_specs=[tile, st, st] + wspecs,
            out_specs=[tile, st, st],
            out_shape=[jax.ShapeDtypeStruct((bsz, t, WIDTH), F32)] + [jax.ShapeDtypeStruct((bsz, S5_LANES), F32)] * 2,
            scratch_shapes=[pltpu.VMEM((S5_LANES // LANES, bsz * tc, LANES), F32)] * 2
            + [pltpu.VMEM((bsz, S5_LANES), F32)] * 2,
            compiler_params=_cp(("arbitrary",)),
            name="s5_seq",
        )(u, s0r, s0i, *weights)
    n = u.shape[0]
    st = _full((n, S5_LANES))
    return pl.pallas_call(
        _s5_step_kernel,
        grid=(1,),
        in_specs=[_full((n, WIDTH)), st, st] + wspecs,
        out_specs=[_full((n, WIDTH)), st, st],
        out_shape=[jax.ShapeDtypeStruct((n, WIDTH), F32)] + [jax.ShapeDtypeStruct((n, S5_LANES), F32)] * 2,
        compiler_params=_cp(("arbitrary",)),
        name="s5_step",
    )(u, s0r, s0i, *weights)


def _mamba_prep_math(xbc_taps, dt_raw, cw, cb, dtb):
    acc = xbc_taps[M_CONV - 1] * cw[0:1]
    for kk in range(1, M_CONV):
        acc = acc + xbc_taps[M_CONV - 1 - kk] * cw[kk:kk + 1]
    acc = acc + cb
    return acc * jax.nn.sigmoid(acc), _softplus(dt_raw + dtb)


def _mamba_prep_seq_kernel(xbc_ref, dt_ref, init_ref, cw_ref, cb_ref, dtb_ref, o_ref, dto_ref, last_ref, carry):
    @pl.when(pl.program_id(1) == 0)
    def _():
        carry[...] = init_ref[...]

    x = xbc_ref[...]
    c8 = carry[...]
    taps = [x] + [_delayed(x, c8, kk) for kk in range(1, M_CONV)]
    tail = x[x.shape[0] - SUBLANES:, :]
    carry[...] = tail
    last_ref[...] = tail
    o_ref[...], dto_ref[...] = _mamba_prep_math(taps, dt_ref[...], cw_ref[...], cb_ref[...], dtb_ref[...])


def _mamba_prep_step_kernel(xbc_ref, dt_ref, p1_ref, p2_ref, p3_ref, cw_ref, cb_ref, dtb_ref, o_ref, dto_ref):
    taps = [xbc_ref[...], p1_ref[...], p2_ref[...], p3_ref[...]]
    o_ref[...], dto_ref[...] = _mamba_prep_math(taps, dt_ref[...], cw_ref[...], cb_ref[...], dtb_ref[...])


M_DT_BLOCK = M_CONV_CH // M_DT_PAD + WIDTH // M_DT_PAD
M_Z_BLOCK = M_CONV_CH // WIDTH


def _mamba_prep(zm, conv_state, weights, seq):
    wspecs = [_full(w.shape) for w in weights]
    if seq:
        bsz, t, _ = zm.shape
        tm = min(TOKEN_TILE, t)
        tile = lambda c: pl.BlockSpec((None, tm, c), lambda b, j: (b, j, 0))
        dt_in = pl.BlockSpec((None, tm, M_DT_PAD), lambda b, j: (b, j, M_DT_BLOCK))
        edge = pl.BlockSpec((None, SUBLANES, M_CONV_CH), lambda b, j: (b, 0, 0))
        return pl.pallas_call(
            _mamba_prep_seq_kernel,
            grid=(bsz, t // tm),
            in_specs=[tile(M_CONV_CH), dt_in, edge] + wspecs,
            out_specs=[tile(M_CONV_CH), tile(M_DT_PAD), edge],
            out_shape=[jax.ShapeDtypeStruct((bsz, t, M_CONV_CH), F32),
                       jax.ShapeDtypeStruct((bsz, t, M_DT_PAD), F32),
                       jax.ShapeDtypeStruct((bsz, SUBLANES, M_CONV_CH), F32)],
            scratch_shapes=[pltpu.VMEM((SUBLANES, M_CONV_CH), F32)],
            compiler_params=_cp(("arbitrary", "arbitrary")),
            name="mamba_prep_seq",
        )(zm, zm, conv_state, *weights)
    n = zm.shape[0]
    past = [pl.BlockSpec((n, M_CONV_CH), lambda i, kk=kk: (0, M_CONV - 1 - kk)) for kk in range(1, M_CONV)]
    return pl.pallas_call(
        _mamba_prep_step_kernel,
        grid=(1,),
        in_specs=[pl.BlockSpec((n, M_CONV_CH), lambda i: (0, 0)),
                  pl.BlockSpec((n, M_DT_PAD), lambda i: (0, M_DT_BLOCK))] + past + wspecs,
        out_specs=[_full((n, M_CONV_CH)), _full((n, M_DT_PAD))],
        out_shape=[jax.ShapeDtypeStruct((n, M_CONV_CH), F32), jax.ShapeDtypeStruct((n, M_DT_PAD), F32)],
        compiler_params=_cp(("arbitrary",)),
        name="mamba_prep_step",
    )(zm, zm, conv_state, conv_state, conv_state, *weights)


def _ssd_post(y, z, nw):
    y = y * (z * jax.nn.sigmoid(z))
    half = WIDTH // M_GROUPS
    parts = []
    for gi in range(M_GROUPS):
        yg = y[:, gi * half:(gi + 1) * half]
        parts.append(yg * lax.rsqrt(jnp.mean(yg * yg, axis=-1, keepdims=True) + M_EPS))
    return jnp.concatenate(parts, axis=1) * nw


def _ssd_chunk_kernel(xbc_ref, dt_ref, z_ref, h0_ref, alog_ref, dsk_ref, nw_ref, o_ref, ht_ref, h_scr):
    j = pl.program_id(1)

    @pl.when(j == 0)
    def _():
        h_scr[...] = h0_ref[...]

    xbc, dt = xbc_ref[...], dt_ref[...]
    q = xbc.shape[0]
    a_row = -jnp.exp(alog_ref[...])
    dsk = dsk_ref[...]
    row = lax.broadcasted_iota(jnp.int32, (q, q), 0)
    col = lax.broadcasted_iota(jnp.int32, (q, q), 1)
    lower = row >= col
    tri = _bf(lower.astype(F32))
    heads_per_group = M_HEADS // M_GROUPS
    ys = []
    for gi in range(M_GROUPS):
        bg = xbc[:, WIDTH + gi * M_STATE:WIDTH + (gi + 1) * M_STATE]
        cg = xbc[:, WIDTH + (M_GROUPS + gi) * M_STATE:WIDTH + (M_GROUPS + gi + 1) * M_STATE]
        cb = _dot_nt(cg, bg)
        for e in range(heads_per_group):
            h = gi * heads_per_group + e
            cs_ = slice(h * M_HEAD, (h + 1) * M_HEAD)
            xh = xbc[:, cs_]
            dtc = dt[:, h:h + 1]
            dac = dtc * a_row[:, h * M_HEAD:h * M_HEAD + 1]
            seg = _dot_exact_lhs(tri, jnp.where(row > col, jnp.broadcast_to(dac, (q, q)), 0.0))
            lm = jnp.where(lower, jnp.exp(seg), 0.0)
            cum = _dot_exact_lhs(tri, jnp.broadcast_to(dac, (q, M_HEAD)))
            cum_last = cum[q - 1:q, :]
            xd = xh * dtc
            hs = h_scr[h]
            y = _dot(cb * lm, xd) + _dot_nt(cg, hs) * jnp.exp(cum) + dsk[:, cs_] * xh
            ys.append(y)
            keep = jnp.exp(jnp.concatenate([cum_last, cum_last], axis=1))
            h_scr[h] = hs * keep + _dot_tn(xd * jnp.exp(cum_last - cum), bg)
    o_ref[...] = _ssd_post(jnp.concatenate(ys, axis=1), z_ref[...], nw_ref[...])

    @pl.when(j == pl.num_programs(1) - 1)
    def _():
        ht_ref[...] = h_scr[...]


def _ssd_chunk(xbc, dt, zm, h0, weights):
    bsz, t, _ = xbc.shape
    q = min(SSD_CHUNK, t)
    tile = lambda c: pl.BlockSpec((None, q, c), lambda b, j: (b, j, 0))
    st = pl.BlockSpec((None, M_HEADS, M_HEAD, M_STATE), lambda b, j: (b, 0, 0, 0))
    return pl.pallas_call(
        _ssd_chunk_kernel,
        grid=(bsz, t // q),
        in_specs=[tile(M_CONV_CH), tile(M_DT_PAD), pl.BlockSpec((None, q, WIDTH), lambda b, j: (b, j, M_Z_BLOCK)),
                  st] + [_full(w.shape) for w in weights],
        out_specs=[tile(WIDTH), st],
        out_shape=[jax.ShapeDtypeStruct((bsz, t, WIDTH), F32),
                   jax.ShapeDtypeStruct((bsz, M_HEADS, M_HEAD, M_STATE), F32)],
        scratch_shapes=[pltpu.VMEM((M_HEADS, M_HEAD, M_STATE), F32)],
        compiler_params=_cp(("arbitrary", "arbitrary")),
        name="ssd_chunk",
    )(xbc, dt, zm, h0, *weights)


def _ssd_step_kernel(xbc_ref, dt_ref, z_ref, h0_ref, alog_ref, dsk_ref, nw_ref, expand_ref, o_ref, ht_ref, y_scr):
    bt = xbc_ref.shape[0]
    xbc = xbc_ref[...]
    xs = xbc[:, :WIDTH]
    dt_full = _dot_exact_rhs(dt_ref[...], expand_ref[...], terms=3)
    keep = jnp.exp(dt_full * -jnp.exp(alog_ref[...]))
    xd = xs * dt_full
    heads_per_group = M_HEADS // M_GROUPS
    zero = jnp.zeros((1, 1), F32)
    for i in range(bt):
        one = slice(i, i + 1)
        xrow = _rows(xd[one], jnp.broadcast_to(zero, (1, WIDTH)))
        brow = _rows(xbc[one, WIDTH:WIDTH + M_GROUPS * M_STATE], jnp.broadcast_to(zero, (1, M_GROUPS * M_STATE)))
        crow = _rows(xbc[one, WIDTH + M_GROUPS * M_STATE:], jnp.broadcast_to(zero, (1, M_GROUPS * M_STATE)))
        outs = []
        for h in range(M_HEADS):
            gi = h // heads_per_group
            cs_ = slice(h * M_HEAD, (h + 1) * M_HEAD)
            gs = slice(gi * M_STATE, (gi + 1) * M_STATE)
            kp = keep[one, cs_]
            hn = h0_ref[i, h] * jnp.concatenate([kp, kp], axis=1) + _dot_tn(xrow[:, cs_], brow[:, gs])
            ht_ref[i, h] = hn
            outs.append(_dot_nt(crow[:, gs], hn))
        y_scr[one, :] = jnp.concatenate(outs, axis=1)[0:1]
    y = y_scr[...] + dsk_ref[...] * xs
    o_ref[...] = _ssd_post(y, z_ref[...], nw_ref[...])


def _ssd_step(xbc, dt, z, h0, weights):
    n = xbc.shape[0]
    bt = SUBLANES
    tile = lambda c: pl.BlockSpec((bt, c), lambda i: (i, 0))
    st = pl.BlockSpec((bt, M_HEADS, M_HEAD, M_STATE), lambda i: (i, 0, 0, 0))
    return pl.pallas_call(
        _ssd_step_kernel,
        grid=(n // bt,),
        in_specs=[tile(M_CONV_CH), tile(M_DT_PAD), pl.BlockSpec((bt, WIDTH), lambda i: (i, M_Z_BLOCK)), st]
        + [_full(w.shape) for w in weights],
        out_specs=[tile(WIDTH), st],
        out_shape=[jax.ShapeDtypeStruct((n, WIDTH), F32),
                   jax.ShapeDtypeStruct((n, M_HEADS, M_HEAD, M_STATE), F32)],
        scratch_shapes=[pltpu.VMEM((bt, WIDTH), F32)],
        compiler_params=_cp(("arbitrary",)),
        name="ssd_step",
    )(xbc, dt, z, h0, *weights)


def _merge_kernel(or_ref, os_ref, om_ref, zg_ref, x_ref, wb_ref, wo_ref, g_ref, o_ref):
    zg = zg_ref[...]
    mixed = None
    for kk, ref in enumerate((or_ref, os_ref, om_ref)):
        proj = jnp.dot(_bf(ref[...]), wb_ref[kk], preferred_element_type=F32)
        term = jax.nn.sigmoid(zg[:, kk * D_MODEL:(kk + 1) * D_MODEL]) * proj
        mixed = term if mixed is None else mixed + term
    out = jnp.dot(_bf(mixed), wo_ref[...], preferred_element_type=F32)
    o_ref[...] = x_ref[...] + _rms(out, g_ref[...])


def _merge(o_r, o_s, o_m, zg, x, wb, wo, g):
    n = x.shape[0]
    tm = min(TOKEN_TILE, n)
    tile = lambda c: pl.BlockSpec((tm, c), lambda i: (i, 0))
    return pl.pallas_call(
        _merge_kernel,
        grid=(n // tm,),
        in_specs=[tile(WIDTH)] * 3 + [tile(3 * D_MODEL), tile(D_MODEL), _full(wb.shape), _full(wo.shape),
                                      _full(g.shape)],
        out_specs=tile(D_MODEL),
        out_shape=jax.ShapeDtypeStruct((n, D_MODEL), F32),
        compiler_params=_cp(("arbitrary",)),
        name="merge",
    )(o_r, o_s, o_m, zg, x, wb, wo, g)


def _ffn_math(gate_taps, val, x, cw, cb, wd, g):
    acc = gate_taps[FFN_CONV - 1] * cw[0:1]
    for kk in range(1, FFN_CONV):
        acc = acc + gate_taps[FFN_CONV - 1 - kk] * cw[kk:kk + 1]
    acc = acc + cb
    f = jnp.dot(_bf(jax.nn.gelu(acc) * val), wd, preferred_element_type=F32)
    return x + _rms(f, g)


def _ffn_seq_kernel(up_ref, x_ref, init_ref, cw_ref, cb_ref, wd_ref, g_ref, o_ref, last_ref, carry):
    @pl.when(pl.program_id(1) == 0)
    def _():
        carry[...] = init_ref[...]

    gate = up_ref[:, :D_FF]
    c8 = carry[...]
    taps = [gate] + [_delayed(gate, c8, kk) for kk in range(1, FFN_CONV)]
    tail = gate[gate.shape[0] - SUBLANES:, :]
    carry[...] = tail
    last_ref[...] = tail
    o_ref[...] = _ffn_math(taps, up_ref[:, D_FF:], x_ref[...], cw_ref[...], cb_ref[...], wd_ref[...], g_ref[...])


def _ffn_step_kernel(up_ref, x_ref, p1_ref, p2_ref, cw_ref, cb_ref, wd_ref, g_ref, o_ref):
    taps = [up_ref[:, :D_FF], p1_ref[...], p2_ref[...]]
    o_ref[...] = _ffn_math(taps, up_ref[:, D_FF:], x_ref[...], cw_ref[...], cb_ref[...], wd_ref[...], g_ref[...])


def _ffn(up, x, conv_state, weights, seq):
    wspecs = [_full(w.shape) for w in weights]
    if seq:
        bsz, t, _ = up.shape
        tm = min(TOKEN_TILE, t)
        tile = lambda c: pl.BlockSpec((None, tm, c), lambda b, j: (b, j, 0))
        edge = pl.BlockSpec((None, SUBLANES, D_FF), lambda b, j: (b, 0, 0))
        return pl.pallas_call(
            _ffn_seq_kernel,
            grid=(bsz, t // tm),
            in_specs=[tile(2 * D_FF), tile(D_MODEL), edge] + wspecs,
            out_specs=[tile(D_MODEL), edge],
            out_shape=[jax.ShapeDtypeStruct((bsz, t, D_MODEL), F32),
                       jax.ShapeDtypeStruct((bsz, SUBLANES, D_FF), F32)],
            scratch_shapes=[pltpu.VMEM((SUBLANES, D_FF), F32)],
            compiler_params=_cp(("arbitrary", "arbitrary")),
            name="ffn_seq",
        )(up, x, conv_state, *weights)
    n = up.shape[0]
    past = [pl.BlockSpec((n, D_FF), lambda i, kk=kk: (0, FFN_CONV - 1 - kk)) for kk in range(1, FFN_CONV)]
    return pl.pallas_call(
        _ffn_step_kernel,
        grid=(1,),
        in_specs=[_full((n, 2 * D_FF)), _full((n, D_MODEL))] + past + wspecs,
        out_specs=_full((n, D_MODEL)),
        out_shape=jax.ShapeDtypeStruct((n, D_MODEL), F32),
        compiler_params=_cp(("arbitrary",)),
        name="ffn_step",
    )(up, x, conv_state, conv_state, *weights)


def _row(v):
    return v.reshape(1, -1).astype(F32)


def _block_diag_ones():
    head = jnp.arange(WIDTH) // R_HEAD
    return (head[:, None] == head[None, :]).astype(BF16)


def _layer_weights(lw):
    (g_pre_mix, g_post_mix, g_pre_ffn, g_post_ffn, w_in,
     r_mu, r_w0, r_w2, r_a0, r_a2, r_g2, r_kk, r_ka, r_rk, r_ln_w, r_ln_b,
     s5_lam_re, s5_lam_im, s5_log_step, s5_b_re, s5_b_im, s5_c_re, s5_c_im, s5_d, s5_w_glu, s5_b_glu,
     m_conv_w, m_conv_b, m_dt_bias, m_a_log, m_d, m_norm_w,
     w_branch, w_out, w_up, f_conv_w, f_conv_b, w_down) = lw
    c0 = R_COLS
    c1 = c0 + WIDTH
    c2 = c1 + WIDTH + M_CONV_CH
    c3 = c2 + M_HEADS
    ones_bd = _block_diag_ones()
    w = {}
    w["g_pre_mix"], w["g_post_mix"] = _row(g_pre_mix), _row(g_post_mix)
    w["g_pre_ffn"], w["g_post_ffn"] = _row(g_pre_ffn), _row(g_post_ffn)
    w["w_r"] = _bf(w_in[:, :c0])
    w["w_s"] = _bf(w_in[:, c0:c1])
    w["w_m"] = _bf(jnp.concatenate(
        [w_in[:, c1 + WIDTH:c2], w_in[:, c1:c1 + WIDTH], w_in[:, c2:c3],
         jnp.zeros((D_MODEL, M_DT_PAD - M_HEADS), F32)], axis=1))
    w["w_g"] = _bf(w_in[:, c3:])
    w["rwkv_prep"] = (_row(r_mu), _row(r_w0), _bf(r_w2), _row(r_a0), _bf(r_a2), _bf(r_g2), _row(r_kk), _row(r_ka),
                      ones_bd)
    w["rwkv_post"] = (_row(r_rk), _row(r_ln_w), _row(r_ln_b), ones_bd)
    ab_re, ab_im, bb_re, bb_im = _s5_params(s5_lam_re, s5_lam_im, s5_log_step, s5_b_re, s5_b_im)
    eye = jnp.eye(S5_GROUPS, dtype=F32)
    to_state = lambda bb: _bf(jnp.einsum('ghp,gk->ghkp', bb, eye).reshape(WIDTH, S5_LANES))
    from_state = lambda c: _bf(jnp.einsum('ghp,gk->gpkh', c, eye).reshape(S5_LANES, WIDTH))
    w["s5"] = (to_state(bb_re), to_state(bb_im), from_state(s5_c_re), from_state(s5_c_im),
               ab_re.reshape(1, S5_LANES), ab_im.reshape(1, S5_LANES), _row(s5_d), _bf(s5_w_glu), _row(s5_b_glu))
    w["mamba_prep"] = (m_conv_w.T.astype(F32), _row(m_conv_b),
                       _row(jnp.pad(m_dt_bias, (0, M_DT_PAD - M_HEADS))))
    head_rows = (_row(jnp.repeat(m_a_log, M_HEAD)), _row(jnp.repeat(m_d, M_HEAD)), _row(m_norm_w))
    w["ssd"] = head_rows
    lane_head = jnp.arange(WIDTH) // M_HEAD
    w["ssd_step"] = head_rows + ((jnp.arange(M_DT_PAD)[:, None] == lane_head[None, :]).astype(BF16),)
    w["w_branch"], w["w_out"] = _bf(w_branch), _bf(w_out)
    w["w_up"] = _bf(w_up)
    w["ffn"] = (f_conv_w.T.astype(F32), _row(f_conv_b), _bf(w_down), _row(g_post_ffn))
    return w


def _pad_rows(state):
    return jnp.pad(state, ((0, 0), (SUBLANES - state.shape[1], 0), (0, 0)))


def _layer(x, states, w, bsz, t):
    shift0, wkv0, s5r0, s5i0, ssd0, mconv0, fconv0 = states
    seq = t > 1
    n = bsz * t
    view = (lambda a: a.reshape(bsz, t, a.shape[-1])) if seq else (lambda a: a)
    flat = lambda a: a.reshape(n, a.shape[-1])

    z_r = _norm_matmul(x, w["g_pre_mix"], w["w_r"], "in_proj_rwkv")
    z_s = _norm_matmul(x, w["g_pre_mix"], w["w_s"], "in_proj_s5")
    z_m = _norm_matmul(x, w["g_pre_mix"], w["w_m"], "in_proj_mamba")
    z_g = _norm_matmul(x, w["g_pre_mix"], w["w_g"], "in_proj_gates")

    if seq:
        *streams, last = _rwkv_prep(view(z_r), jnp.broadcast_to(shift0[:, None], (bsz, SUBLANES, R_COLS)),
                                    w["rwkv_prep"], True)
        shift1 = last[:, SUBLANES - 1]
        o_r, wkv1 = _rwkv_chunk(streams, wkv0, w["rwkv_post"])
    else:
        streams = _rwkv_prep(z_r, shift0, w["rwkv_prep"], False)
        shift1 = z_r
        o_r, wkv1 = _rwkv_step(streams, wkv0, w["rwkv_post"])

    o_s, s5r1, s5i1 = _s5(view(z_s), s5r0.reshape(bsz, S5_LANES), s5i0.reshape(bsz, S5_LANES), w["s5"], seq)
    s5r1 = s5r1.reshape(bsz, S5_GROUPS, S5_STATE)
    s5i1 = s5i1.reshape(bsz, S5_GROUPS, S5_STATE)

    if seq:
        xbc_c, dt, last = _mamba_prep(view(z_m), _pad_rows(mconv0), w["mamba_prep"], True)
        mconv1 = last[:, SUBLANES - (M_CONV - 1):]
        o_m, ssd1 = _ssd_chunk(xbc_c, dt, view(z_m), ssd0, w["ssd"])
    else:
        xbc_c, dt = _mamba_prep(z_m, mconv0.reshape(bsz, (M_CONV - 1) * M_CONV_CH), w["mamba_prep"], False)
        mconv1 = jnp.concatenate([mconv0[:, 1:], z_m[:, None, :M_CONV_CH]], axis=1)
        o_m, ssd1 = _ssd_step(xbc_c, dt, z_m, ssd0, w["ssd_step"])

    x = _merge(flat(o_r), flat(o_s), flat(o_m), z_g, x, w["w_branch"], w["w_out"], w["g_post_mix"])

    up = _norm_matmul(x, w["g_pre_ffn"], w["w_up"], "ffn_up")
    if seq:
        x3, last = _ffn(view(up), view(x), _pad_rows(fconv0), w["ffn"], True)
        x = flat(x3)
        fconv1 = last[:, SUBLANES - (FFN_CONV - 1):]
    else:
        x = _ffn(up, x, fconv0.reshape(bsz, (FFN_CONV - 1) * D_FF), w["ffn"], False)
        fconv1 = jnp.concatenate([fconv0[:, 1:], up[:, None, :D_FF]], axis=1)
    return x, (shift1, wkv1, s5r1, s5i1, ssd1, mconv1, fconv1)


def _zero_states(n):
    return (jnp.zeros((n, R_COLS), F32),
            jnp.zeros((n, R_HEADS, R_HEAD, R_HEAD), F32),
            jnp.zeros((n, S5_GROUPS, S5_STATE), F32),
            jnp.zeros((n, S5_GROUPS, S5_STATE), F32),
            jnp.zeros((n, M_HEADS, M_HEAD, M_STATE), F32),
            jnp.zeros((n, M_CONV - 1, M_CONV_CH), F32),
            jnp.zeros((n, FFN_CONV - 1, D_FF), F32))


def kernel(x_prompt, x_sample, state_rwkv_shift, state_rwkv_wkv, state_s5_re, state_s5_im, state_ssd, state_ssd_conv, state_ffn_conv, g_pre_mix, g_post_mix, g_pre_ffn, g_post_ffn, w_in, r_mu, r_w0, r_w2, r_a0, r_a2, r_g2, r_kk, r_ka, r_rk, r_ln_w, r_ln_b, s5_lam_re, s5_lam_im, s5_log_step, s5_b_re, s5_b_im, s5_c_re, s5_c_im, s5_d, s5_w_glu, s5_b_glu, m_conv_w, m_conv_b, m_dt_bias, m_a_log, m_d, m_norm_w, w_branch, w_out, w_up, f_conv_w, f_conv_b, w_down):
    stacked = (g_pre_mix, g_post_mix, g_pre_ffn, g_post_ffn, w_in,
               r_mu, r_w0, r_w2, r_a0, r_a2, r_g2, r_kk, r_ka, r_rk, r_ln_w, r_ln_b,
               s5_lam_re, s5_lam_im, s5_log_step, s5_b_re, s5_b_im, s5_c_re, s5_c_im, s5_d,
               s5_w_glu, s5_b_glu,
               m_conv_w, m_conv_b, m_dt_bias, m_a_log, m_d, m_norm_w,
               w_branch, w_out, w_up, f_conv_w, f_conv_b, w_down)
    cache_in = (state_rwkv_shift, state_rwkv_wkv, state_s5_re, state_s5_im,
                state_ssd, state_ssd_conv, state_ffn_conv)
    depth = w_in.shape[0]
    pb, pt, d = x_prompt.shape
    sb, s_t, _ = x_sample.shape
    xp = x_prompt.reshape(pb * pt, d)
    xs = x_sample.reshape(sb * s_t, d)
    new_p = [[] for _ in cache_in]
    new_s = [[] for _ in cache_in]
    for l in range(depth):
        w = _layer_weights(tuple(a[l] for a in stacked))
        xp, sp = _layer(xp, _zero_states(pb), w, pb, pt)
        xs, ss = _layer(xs, tuple(c[l] for c in cache_in), w, sb, s_t)
        for i in range(len(cache_in)):
            new_p[i].append(sp[i])
            new_s[i].append(ss[i])
    outs = [xp.reshape(pb, pt, d), xs.reshape(sb, s_t, d)]
    for p_list, s_list in zip(new_p, new_s):
        outs.append(jnp.stack(p_list, 0))
        outs.append(jnp.stack(s_list, 0))
    return tuple(outs)
```

```python
import functools

import jax
import jax.numpy as jnp
from jax import lax
from jax.experimental import pallas as pl
from jax.experimental.pallas import tpu as pltpu

F32 = jnp.float32
BF16 = jnp.bfloat16

D_MODEL = 1024
WIDTH = 512
R_HEADS, R_HEAD = 8, 64
R_COLS = 1792
R_LN_EPS = 64e-5
S5_GROUPS, S5_GROUP, S5_STATE = 32, 16, 64
S5_LANES = S5_GROUPS * S5_STATE
M_HEADS, M_HEAD, M_GROUPS, M_STATE = 8, 64, 2, 128
M_CONV, M_CONV_CH = 4, 1024
M_DT_PAD = 256
M_COLS_PAD = WIDTH + M_CONV_CH + M_DT_PAD
M_EPS = 1e-5
D_FF = 2816
FFN_CONV = 3
EPS = 1e-6
SUBLANES = 8
LANES = 128

RWKV_CHUNK = 64
RWKV_TILE = 128
SSD_CHUNK = 128
S5_CHUNK = 64
TOKEN_TILE = 256


def _cp(sem, vmem_mb=48):
    return pltpu.CompilerParams(dimension_semantics=sem, vmem_limit_bytes=vmem_mb * 1024 * 1024)


def _bf(x):
    return x.astype(BF16)


def _dot(a, b):
    return jnp.dot(_bf(a), _bf(b), preferred_element_type=F32)


def _dot_nt(a, b):
    return lax.dot_general(_bf(a), _bf(b), (((1,), (1,)), ((), ())), preferred_element_type=F32)


def _dot_tn(a, b):
    return lax.dot_general(_bf(a), _bf(b), (((0,), (0,)), ((), ())), preferred_element_type=F32)


def _split(x, terms):
    out = []
    for _ in range(terms - 1):
        h = _bf(x)
        out.append(h)
        x = x - h.astype(F32)
    out.append(_bf(x))
    return out


def _dot_exact_lhs(m_bf16, x, terms=3):
    acc = None
    for h in _split(x, terms):
        p = jnp.dot(m_bf16, h, preferred_element_type=F32)
        acc = p if acc is None else acc + p
    return acc


def _dot_exact_rhs(x, m_bf16, terms=2):
    acc = None
    for h in _split(x, terms):
        p = jnp.dot(h, m_bf16, preferred_element_type=F32)
        acc = p if acc is None else acc + p
    return acc


def _softplus(x):
    return jnp.maximum(x, 0.0) + jnp.log1p(jnp.exp(-jnp.abs(x)))


def _rms(x, g, eps=EPS):
    return x * lax.rsqrt(jnp.mean(x * x, axis=-1, keepdims=True) + eps) * g


def _delayed(x, carry8, k):
    rx = pltpu.roll(x, k, 0)
    rc = pltpu.roll(carry8, k, 0)
    row = lax.broadcasted_iota(jnp.int32, (SUBLANES, x.shape[1]), 0)
    head = jnp.where(row < k, rc, rx[:SUBLANES])
    if x.shape[0] == SUBLANES:
        return head
    return jnp.concatenate([head, rx[SUBLANES:]], axis=0)


def _norm_matmul_kernel(x_ref, g_ref, w_ref, o_ref, *, cn):
    hb = _bf(_rms(x_ref[...], g_ref[...]))
    for c in range(0, o_ref.shape[1], cn):
        o_ref[:, c:c + cn] = jnp.dot(hb, w_ref[:, c:c + cn], preferred_element_type=F32)


def _norm_matmul(x, g, w, name):
    n, d = x.shape
    c = w.shape[1]
    tm = min(TOKEN_TILE, n)
    cn = 256
    return pl.pallas_call(
        functools.partial(_norm_matmul_kernel, cn=cn),
        grid=(n // tm,),
        in_specs=[pl.BlockSpec((tm, d), lambda i: (i, 0)),
                  pl.BlockSpec((1, d), lambda i: (0, 0)),
                  pl.BlockSpec((d, c), lambda i: (0, 0))],
        out_specs=pl.BlockSpec((tm, c), lambda i: (i, 0)),
        out_shape=jax.ShapeDtypeStruct((n, c), F32),
        compiler_params=_cp(("arbitrary",)),
        name=name,
    )(x, g, w)


def _rwkv_prep_math(z, prev, mu, w0, w2, a0, a2, g2, kkw, kaw, ones_bd):
    zm = z + (prev - z) * mu
    r = zm[:, 0:WIDTH]
    k = zm[:, WIDTH:2 * WIDTH]
    v = zm[:, 2 * WIDTH:3 * WIDTH]
    dw = zm[:, 1536:1600]
    da = zm[:, 1600:1664]
    dg = zm[:, 1664:1792]
    logw = -_softplus(-(w0 + _dot(jnp.tanh(dw), w2))) - 0.5
    ld = -jnp.exp(logw)
    a = jax.nn.sigmoid(a0 + _dot(da, a2))
    g = _dot(jax.nn.sigmoid(dg), g2)
    kk = k * kkw
    ss = _dot_exact_rhs(kk * kk, ones_bd)
    kk = kk / jnp.maximum(jnp.sqrt(ss), 1e-12)
    k2 = k * (1.0 + (a - 1.0) * kaw)
    return r, ld, k2, v, kk, kk * a, g


def _rwkv_prep_seq_kernel(z_ref, init_ref, mu_ref, w0_ref, w2_ref, a0_ref, a2_ref, g2_ref, kkw_ref, kaw_ref,
                          ones_ref, r_ref, ld_ref, k_ref, v_ref, kk_ref, b_ref, g_ref, last_ref, carry):
    @pl.when(pl.program_id(1) == 0)
    def _():
        carry[...] = init_ref[...]

    z = z_ref[...]
    prev = _delayed(z, carry[...], 1)
    tail = z[z.shape[0] - SUBLANES:, :]
    carry[...] = tail
    last_ref[...] = tail
    outs = _rwkv_prep_math(z, prev, mu_ref[...], w0_ref[...], w2_ref[...], a0_ref[...], a2_ref[...],
                           g2_ref[...], kkw_ref[...], kaw_ref[...], ones_ref[...])
    for ref, val in zip((r_ref, ld_ref, k_ref, v_ref, kk_ref, b_ref, g_ref), outs):
        ref[...] = val


def _rwkv_prep_step_kernel(z_ref, prev_ref, mu_ref, w0_ref, w2_ref, a0_ref, a2_ref, g2_ref, kkw_ref, kaw_ref,
                           ones_ref, r_ref, ld_ref, k_ref, v_ref, kk_ref, b_ref, g_ref):
    outs = _rwkv_prep_math(z_ref[...], prev_ref[...], mu_ref[...], w0_ref[...], w2_ref[...], a0_ref[...],
                           a2_ref[...], g2_ref[...], kkw_ref[...], kaw_ref[...], ones_ref[...])
    for ref, val in zip((r_ref, ld_ref, k_ref, v_ref, kk_ref, b_ref, g_ref), outs):
        ref[...] = val


def _full(shape):
    nd = len(shape)
    return pl.BlockSpec(shape, lambda *_: (0,) * nd)


def _rwkv_prep(z, shift, weights, seq):
    wspecs = [_full(w.shape) for w in weights]
    if seq:
        bsz, t, _ = z.shape
        tm = min(TOKEN_TILE, t)
        tile = lambda c: pl.BlockSpec((None, tm, c), lambda b, j: (b, j, 0))
        edge = pl.BlockSpec((None, SUBLANES, R_COLS), lambda b, j: (b, 0, 0))
        return pl.pallas_call(
            _rwkv_prep_seq_kernel,
            grid=(bsz, t // tm),
            in_specs=[tile(R_COLS), edge] + wspecs,
            out_specs=[tile(WIDTH)] * 7 + [edge],
            out_shape=[jax.ShapeDtypeStruct((bsz, t, WIDTH), F32)] * 7
            + [jax.ShapeDtypeStruct((bsz, SUBLANES, R_COLS), F32)],
            scratch_shapes=[pltpu.VMEM((SUBLANES, R_COLS), F32)],
            compiler_params=_cp(("arbitrary", "arbitrary")),
            name="rwkv_prep_seq",
        )(z, shift, *weights)
    n = z.shape[0]
    return pl.pallas_call(
        _rwkv_prep_step_kernel,
        grid=(1,),
        in_specs=[_full((n, R_COLS)), _full((n, R_COLS))] + wspecs,
        out_specs=[_full((n, WIDTH))] * 7,
        out_shape=[jax.ShapeDtypeStruct((n, WIDTH), F32)] * 7,
        compiler_params=_cp(("arbitrary",)),
        name="rwkv_prep_step",
    )(z, shift, *weights)


def _rwkv_post(y, r, k, v, g, rk, lnw, lnb, ones_bd):
    inv = 1.0 / R_HEAD
    mean = _dot_exact_rhs(y, ones_bd) * inv
    d = y - mean
    var = _dot_exact_rhs(d * d, ones_bd) * inv
    yn = d * lax.rsqrt(var + R_LN_EPS) * lnw + lnb
    bonus = _dot_exact_rhs(r * k * rk, ones_bd) * v
    return (yn + bonus) * g


def _rwkv_chunk_kernel(r_ref, ld_ref, k_ref, v_ref, kk_ref, b_ref, g_ref, s0_ref, rk_ref, lnw_ref, lnb_ref,
                       ones_ref, o_ref, st_ref, s_scr):
    j = pl.program_id(1)
    L = RWKV_CHUNK
    pair = 2 * R_HEAD
    n_pairs = R_HEADS // 2
    tm = r_ref.shape[0]
    n_chunks = tm // L

    @pl.when(j == 0)
    def _():
        zero = jnp.zeros((R_HEAD, R_HEAD), F32)
        for p in range(n_pairs):
            top = jnp.concatenate([s0_ref[2 * p], zero], axis=1)
            bot = jnp.concatenate([zero, s0_ref[2 * p + 1]], axis=1)
            s_scr[p] = jnp.concatenate([top, bot], axis=0)

    r, ld, k, v, kk, b = (ref[...] for ref in (r_ref, ld_ref, k_ref, v_ref, kk_ref, b_ref))
    trow = lax.broadcasted_iota(jnp.int32, (tm, tm), 0)
    tcol = lax.broadcasted_iota(jnp.int32, (tm, tm), 1)
    same_chunk = (trow // L) == (tcol // L)
    cum = _dot_exact_lhs(_bf(((trow >= tcol) & same_chunk).astype(F32)), ld)
    wc = jnp.exp(cum)
    winv = jnp.exp(-cum)
    r_t = r * wc
    kk_t = kk * jnp.exp(cum - ld)
    k_h = k * winv
    b_h = b * winv

    row = lax.broadcasted_iota(jnp.int32, (pair, pair), 0)
    col = lax.broadcasted_iota(jnp.int32, (pair, pair), 1)
    same_head = (row // R_HEAD) == (col // R_HEAD)
    strict = same_head & ((row % R_HEAD) > (col % R_HEAD))
    lower = same_head & ((row % R_HEAD) >= (col % R_HEAD))

    def bd(x):
        return jnp.where(same_head, jnp.concatenate([x, x], axis=0), 0.0)

    units = [(c, p) for c in range(n_chunks) for p in range(n_pairs)]
    pre = {}
    for c, p in units:
        rows = slice(c * L, (c + 1) * L)
        lanes = slice(p * pair, (p + 1) * pair)
        wl = wc[(c + 1) * L - 1:(c + 1) * L, lanes]
        pre[c, p] = dict(kkt=bd(kk_t[rows, lanes]), rt=bd(r_t[rows, lanes]), kh=bd(k_h[rows, lanes]),
                         bh=bd(b_h[rows, lanes]), v=bd(v[rows, lanes]), wl=wl)
    for u in units:
        d = pre[u]
        a = _dot_nt(jnp.concatenate([d["kkt"], d["rt"]], axis=0), jnp.concatenate([d["kh"], d["bh"]], axis=0))
        d["akk_k"] = jnp.where(strict, a[:pair, :pair], 0.0)
        d["n"] = jnp.where(strict, a[:pair, pair:], 0.0)
        d["ar_k"] = jnp.where(lower, a[pair:, :pair], 0.0)
        d["ar_b"] = jnp.where(lower, a[pair:, pair:], 0.0)
        d["q"] = -d["n"]
        d["m"] = d["n"]
    power = 2
    while power < L:
        for u in units:
            d = pre[u]
            d["m"] = _dot(d["m"], d["m"])
            d["q"] = d["q"] + d["m"] + _dot(d["q"], d["m"])
        power *= 2
    for u in units:
        d = pre[u]
        x = _dot(d["akk_k"], d["v"])
        both = jnp.concatenate([d["kkt"], x], axis=1)
        both = both + _dot(d["q"], both)
        d["g"], d["u0"] = both[:, :pair], both[:, pair:]
    for u in units:
        d = pre[u]
        t = _dot(d["ar_b"], jnp.concatenate([d["g"], d["u0"]], axis=1))
        d["ry"] = d["rt"] - t[:, :pair]
        d["y0"] = _dot(d["ar_k"], d["v"]) - t[:, pair:]
        kw = d["kh"] * d["wl"]
        bw = d["bh"] * d["wl"]
        d["pm"] = _dot_tn(d["g"], bw)
        d["c"] = _dot_tn(jnp.concatenate([d["v"], -d["u0"]], axis=0), jnp.concatenate([kw, bw], axis=0))
    ys = []
    for c in range(n_chunks):
        parts = []
        for p in range(n_pairs):
            d = pre[c, p]
            s = s_scr[p]
            y_bd = _dot_nt(d["ry"], s) + d["y0"]
            s_scr[p] = s * d["wl"] - _dot(s, d["pm"]) + d["c"]
            parts.append(y_bd[:L] + y_bd[L:])
        ys.append(jnp.concatenate(parts, axis=1))
    y = jnp.concatenate(ys, axis=0)
    o_ref[...] = _rwkv_post(y, r, k, v, g_ref[...], rk_ref[...], lnw_ref[...], lnb_ref[...], ones_ref[...])

    @pl.when(j == pl.num_programs(1) - 1)
    def _():
        for p in range(n_pairs):
            s = s_scr[p]
            st_ref[2 * p] = s[:R_HEAD, :R_HEAD]
            st_ref[2 * p + 1] = s[R_HEAD:, R_HEAD:]


def _rwkv_chunk(streams, s0, weights):
    bsz, t, _ = streams[0].shape
    tm = min(RWKV_TILE, t)
    tile = pl.BlockSpec((None, tm, WIDTH), lambda b, j: (b, j, 0))
    st = pl.BlockSpec((None, R_HEADS, R_HEAD, R_HEAD), lambda b, j: (b, 0, 0, 0))
    return pl.pallas_call(
        _rwkv_chunk_kernel,
        grid=(bsz, t // tm),
        in_specs=[tile] * 7 + [st] + [_full(w.shape) for w in weights],
        out_specs=[tile, st],
        out_shape=[jax.ShapeDtypeStruct((bsz, t, WIDTH), F32),
                   jax.ShapeDtypeStruct((bsz, R_HEADS, R_HEAD, R_HEAD), F32)],
        scratch_shapes=[pltpu.VMEM((R_HEADS // 2, 2 * R_HEAD, 2 * R_HEAD), F32)],
        compiler_params=_cp(("arbitrary", "arbitrary")),
        name="rwkv_chunk",
    )(*streams, s0, *weights)


def _rows(first, second):
    row = lax.broadcasted_iota(jnp.int32, (SUBLANES, first.shape[1]), 0)
    return jnp.where(row == 0, first, jnp.where(row == 1, second, 0.0))


def _rwkv_step_kernel(r_ref, ld_ref, k_ref, v_ref, kk_ref, b_ref, g_ref, s0_ref, rk_ref, lnw_ref, lnb_ref,
                      ones_ref, o_ref, st_ref, y_scr):
    bt = r_ref.shape[0]
    ones_bd = ones_ref[...]
    r, k, v, b = r_ref[...], k_ref[...], v_ref[...], b_ref[...]
    w = jnp.exp(ld_ref[...])
    wr = w * r
    b_dot_r = _dot_exact_rhs(b * r, ones_bd)
    k_dot_r = _dot_exact_rhs(k * r, ones_bd)

    for i in range(bt):
        one = slice(i, i + 1)
        lhs = _rows(kk_ref[one, :], wr[one])
        sa_rows = []
        for h in range(R_HEADS):
            cs = slice(h * R_HEAD, (h + 1) * R_HEAD)
            sa_rows.append(_dot_nt(lhs[:, cs], s0_ref[i, h]))
        proj = jnp.concatenate(sa_rows, axis=1)
        sa = proj[0:1]
        y_scr[one, :] = proj[1:2] - sa * b_dot_r[one] + v[one] * k_dot_r[one]
        left = _rows(v[one], -sa)
        right = _rows(k[one], b[one])
        l_hi, l_lo = _split(left, 2)
        r_hi, r_lo = _split(right, 2)
        for h in range(R_HEADS):
            cs = slice(h * R_HEAD, (h + 1) * R_HEAD)
            tn = lambda x, y: lax.dot_general(x[:, cs], y[:, cs], (((0,), (0,)), ((), ())),
                                              preferred_element_type=F32)
            upd = tn(l_hi, r_hi) + tn(l_hi, r_lo) + tn(l_lo, r_hi)
            st_ref[i, h] = s0_ref[i, h] * w[one, cs] + upd
    o_ref[...] = _rwkv_post(y_scr[...], r, k, v, g_ref[...], rk_ref[...], lnw_ref[...], lnb_ref[...], ones_bd)


def _rwkv_step(streams, s0, weights):
    n = streams[0].shape[0]
    bt = SUBLANES
    tile = pl.BlockSpec((bt, WIDTH), lambda i: (i, 0))
    st = pl.BlockSpec((bt, R_HEADS, R_HEAD, R_HEAD), lambda i: (i, 0, 0, 0))
    return pl.pallas_call(
        _rwkv_step_kernel,
        grid=(n // bt,),
        in_specs=[tile] * 7 + [st] + [_full(w.shape) for w in weights],
        out_specs=[tile, st],
        out_shape=[jax.ShapeDtypeStruct((n, WIDTH), F32),
                   jax.ShapeDtypeStruct((n, R_HEADS, R_HEAD, R_HEAD), F32)],
        scratch_shapes=[pltpu.VMEM((bt, WIDTH), F32)],
        compiler_params=_cp(("arbitrary",)),
        name="rwkv_step",
    )(*streams, s0, *weights)


def _s5_param_kernel(lr_ref, li_ref, ls_ref, br_ref, bi_ref, abr_ref, abi_ref, bbr_ref, bbi_ref):
    lr, li = lr_ref[...], li_ref[...]
    delta = jnp.exp(ls_ref[...])
    mag = jnp.exp(lr * delta)
    ab_re = mag * jnp.cos(li * delta)
    ab_im = mag * jnp.sin(li * delta)
    den = lr * lr + li * li
    nr, ni = ab_re - 1.0, ab_im
    cf_re = (nr * lr + ni * li) / den
    cf_im = (ni * lr - nr * li) / den
    abr_ref[...] = ab_re
    abi_ref[...] = ab_im
    br, bi = br_ref[...], bi_ref[...]
    bbr_ref[...] = cf_re[:, None, :] * br - cf_im[:, None, :] * bi
    bbi_ref[...] = cf_re[:, None, :] * bi + cf_im[:, None, :] * br


def _s5_params(lam_re, lam_im, log_step, b_re, b_im):
    g, p, h = b_re.shape
    gp = jax.ShapeDtypeStruct((g, p), F32)
    ghp = jax.ShapeDtypeStruct((g, h, p), F32)
    args = (lam_re, lam_im, log_step.reshape(g, 1), jnp.swapaxes(b_re, 1, 2), jnp.swapaxes(b_im, 1, 2))
    return pl.pallas_call(
        _s5_param_kernel,
        grid=(1,),
        in_specs=[_full(a.shape) for a in args],
        out_specs=[_full((g, p)), _full((g, p)), _full((g, h, p)), _full((g, h, p))],
        out_shape=[gp, gp, ghp, ghp],
        compiler_params=_cp(("arbitrary",)),
        name="s5_params",
    )(*args)


S5_BLOCKS = 4
S5_BLOCK_IN = WIDTH // S5_BLOCKS
S5_BLOCK_STATE = S5_LANES // S5_BLOCKS
S5_BLOCK_TILES = S5_BLOCK_STATE // LANES


def _s5_drive(ub, wbr_ref, wbi_ref, blk):
    cols = slice(blk * S5_BLOCK_IN, (blk + 1) * S5_BLOCK_IN)
    return (jnp.dot(ub[:, cols], wbr_ref[blk], preferred_element_type=F32),
            jnp.dot(ub[:, cols], wbi_ref[blk], preferred_element_type=F32))


def _s5_readout(u, xr_blocks, xi_blocks, wcr_ref, wci_ref, d, wg, bg):
    y = jnp.concatenate([_dot(xr_blocks[q], wcr_ref[q]) - _dot(xi_blocks[q], wci_ref[q])
                         for q in range(S5_BLOCKS)], axis=1) + d * u
    y = jax.nn.gelu(y)
    return y * jax.nn.sigmoid(_dot(y, wg) + bg)


def _s5_seq_kernel(u_ref, s0r_ref, s0i_ref, wbr_ref, wbi_ref, wcr_ref, wci_ref, ar_ref, ai_ref, d_ref, wg_ref,
                   bg_ref, o_ref, str_ref, sti_ref, xr_scr, xi_scr, cr_scr, ci_scr, *, bt, tc):
    @pl.when(pl.program_id(0) == 0)
    def _():
        cr_scr[...] = s0r_ref[...]
        ci_scr[...] = s0i_ref[...]

    u = u_ref[...].reshape(bt * tc, WIDTH)
    ub = _bf(u)
    for q in range(S5_BLOCKS):
        bu_re, bu_im = _s5_drive(ub, wbr_ref, wbi_ref, q)
        for c in range(S5_BLOCK_TILES):
            xr_scr[q * S5_BLOCK_TILES + c] = bu_re[:, c * LANES:(c + 1) * LANES]
            xi_scr[q * S5_BLOCK_TILES + c] = bu_im[:, c * LANES:(c + 1) * LANES]

    for q in range(S5_BLOCKS):
        tiles = range(q * S5_BLOCK_TILES, (q + 1) * S5_BLOCK_TILES)
        ar = [jnp.broadcast_to(ar_ref[:, c * LANES:(c + 1) * LANES], (bt, LANES)) for c in tiles]
        ai = [jnp.broadcast_to(ai_ref[:, c * LANES:(c + 1) * LANES], (bt, LANES)) for c in tiles]

        def body(t, carry, tiles=tiles, ar=ar, ai=ai):
            rows = pl.ds(t, bt, stride=tc)
            out = []
            for n, c in enumerate(tiles):
                xr, xi = carry[2 * n], carry[2 * n + 1]
                nr = ar[n] * xr - ai[n] * xi + xr_scr[c, rows, :]
                ni = ar[n] * xi + ai[n] * xr + xi_scr[c, rows, :]
                xr_scr[c, rows, :] = nr
                xi_scr[c, rows, :] = ni
                out += [nr, ni]
            return tuple(out)

        init = []
        for c in tiles:
            init += [cr_scr[:, c * LANES:(c + 1) * LANES], ci_scr[:, c * LANES:(c + 1) * LANES]]
        fin = lax.fori_loop(0, tc, body, tuple(init), unroll=2)
        for n, c in enumerate(tiles):
            cr_scr[:, c * LANES:(c + 1) * LANES] = fin[2 * n]
            ci_scr[:, c * LANES:(c + 1) * LANES] = fin[2 * n + 1]
    str_ref[...] = cr_scr[...]
    sti_ref[...] = ci_scr[...]
    blocks = lambda scr: [jnp.concatenate([scr[q * S5_BLOCK_TILES + c] for c in range(S5_BLOCK_TILES)], axis=1)
                          for q in range(S5_BLOCKS)]
    y = _s5_readout(u, blocks(xr_scr), blocks(xi_scr), wcr_ref, wci_ref, d_ref[...], wg_ref[...], bg_ref[...])
    o_ref[...] = y.reshape(bt, tc, WIDTH)


def _s5_step_kernel(u_ref, s0r_ref, s0i_ref, wbr_ref, wbi_ref, wcr_ref, wci_ref, ar_ref, ai_ref, d_ref, wg_ref,
                    bg_ref, o_ref, str_ref, sti_ref):
    u = u_ref[...]
    ub = _bf(u)
    xr_blocks, xi_blocks = [], []
    for q in range(S5_BLOCKS):
        lanes = slice(q * S5_BLOCK_STATE, (q + 1) * S5_BLOCK_STATE)
        ar, ai = ar_ref[:, lanes], ai_ref[:, lanes]
        sr, si = s0r_ref[:, lanes], s0i_ref[:, lanes]
        bu_re, bu_im = _s5_drive(ub, wbr_ref, wbi_ref, q)
        xr = ar * sr - ai * si + bu_re
        xi = ar * si + ai * sr + bu_im
        str_ref[:, lanes] = xr
        sti_ref[:, lanes] = xi
        xr_blocks.append(xr)
        xi_blocks.append(xi)
    o_ref[...] = _s5_readout(u, xr_blocks, xi_blocks, wcr_ref, wci_ref, d_ref[...], wg_ref[...], bg_ref[...])


def _s5(u, s0r, s0i, weights, seq):
    wspecs = [_full(w.shape) for w in weights]
    if seq:
        bsz, t, _ = u.shape
        tc = min(S5_CHUNK, t)
        st = _full((bsz, S5_LANES))
        tile = pl.BlockSpec((bsz, tc, WIDTH), lambda j: (0, j, 0))
        return pl.pallas_call(
            functools.partial(_s5_seq_kernel, bt=bsz, tc=tc),
            grid=(t // tc,),
            in_specs=[tile, st, st] + wspecs,
            out_specs=[tile, st, st],
            out_shape=[jax.ShapeDtypeStruct((bsz, t, WIDTH), F32)] + [jax.ShapeDtypeStruct((bsz, S5_LANES), F32)] * 2,
            scratch_shapes=[pltpu.VMEM((S5_LANES // LANES, bsz * tc, LANES), F32)] * 2
            + [pltpu.VMEM((bsz, S5_LANES), F32)] * 2,
            compiler_params=_cp(("arbitrary",)),
            name="s5_seq",
        )(u, s0r, s0i, *weights)
    n = u.shape[0]
    st = _full((n, S5_LANES))
    return pl.pallas_call(
        _s5_step_kernel,
        grid=(1,),
        in_specs=[_full((n, WIDTH)), st, st] + wspecs,
        out_specs=[_full((n, WIDTH)), st, st],
        out_shape=[jax.ShapeDtypeStruct((n, WIDTH), F32)] + [jax.ShapeDtypeStruct((n, S5_LANES), F32)] * 2,
        compiler_params=_cp(("arbitrary",)),
        name="s5_step",
    )(u, s0r, s0i, *weights)


def _mamba_prep_math(xbc_taps, dt_raw, cw, cb, dtb):
    acc = xbc_taps[M_CONV - 1] * cw[0:1]
    for kk in range(1, M_CONV):
        acc = acc + xbc_taps[M_CONV - 1 - kk] * cw[kk:kk + 1]
    acc = acc + cb
    return acc * jax.nn.sigmoid(acc), _softplus(dt_raw + dtb)


def _mamba_prep_seq_kernel(xbc_ref, dt_ref, init_ref, cw_ref, cb_ref, dtb_ref, o_ref, dto_ref, last_ref, carry):
    @pl.when(pl.program_id(1) == 0)
    def _():
        carry[...] = init_ref[...]

    x = xbc_ref[...]
    c8 = carry[...]
    taps = [x] + [_delayed(x, c8, kk) for kk in range(1, M_CONV)]
    tail = x[x.shape[0] - SUBLANES:, :]
    carry[...] = tail
    last_ref[...] = tail
    o_ref[...], dto_ref[...] = _mamba_prep_math(taps, dt_ref[...], cw_ref[...], cb_ref[...], dtb_ref[...])


def _mamba_prep_step_kernel(xbc_ref, dt_ref, p1_ref, p2_ref, p3_ref, cw_ref, cb_ref, dtb_ref, o_ref, dto_ref):
    taps = [xbc_ref[...], p1_ref[...], p2_ref[...], p3_ref[...]]
    o_ref[...], dto_ref[...] = _mamba_prep_math(taps, dt_ref[...], cw_ref[...], cb_ref[...], dtb_ref[...])


M_DT_BLOCK = M_CONV_CH // M_DT_PAD + WIDTH // M_DT_PAD
M_Z_BLOCK = M_CONV_CH // WIDTH


def _mamba_prep(zm, conv_state, weights, seq):
    wspecs = [_full(w.shape) for w in weights]
    if seq:
        bsz, t, _ = zm.shape
        tm = min(TOKEN_TILE, t)
        tile = lambda c: pl.BlockSpec((None, tm, c), lambda b, j: (b, j, 0))
        dt_in = pl.BlockSpec((None, tm, M_DT_PAD), lambda b, j: (b, j, M_DT_BLOCK))
        edge = pl.BlockSpec((None, SUBLANES, M_CONV_CH), lambda b, j: (b, 0, 0))
        return pl.pallas_call(
            _mamba_prep_seq_kernel,
            grid=(bsz, t // tm),
            in_specs=[tile(M_CONV_CH), dt_in, edge] + wspecs,
            out_specs=[tile(M_CONV_CH), tile(M_DT_PAD), edge],
            out_shape=[jax.ShapeDtypeStruct((bsz, t, M_CONV_CH), F32),
                       jax.ShapeDtypeStruct((bsz, t, M_DT_PAD), F32),
                       jax.ShapeDtypeStruct((bsz, SUBLANES, M_CONV_CH), F32)],
            scratch_shapes=[pltpu.VMEM((SUBLANES, M_CONV_CH), F32)],
            compiler_params=_cp(("arbitrary", "arbitrary")),
            name="mamba_prep_seq",
        )(zm, zm, conv_state, *weights)
    n = zm.shape[0]
    past = [pl.BlockSpec((n, M_CONV_CH), lambda i, kk=kk: (0, M_CONV - 1 - kk)) for kk in range(1, M_CONV)]
    return pl.pallas_call(
        _mamba_prep_step_kernel,
        grid=(1,),
        in_specs=[pl.BlockSpec((n, M_CONV_CH), lambda i: (0, 0)),
                  pl.BlockSpec((n, M_DT_PAD), lambda i: (0, M_DT_BLOCK))] + past + wspecs,
        out_specs=[_full((n, M_CONV_CH)), _full((n, M_DT_PAD))],
        out_shape=[jax.ShapeDtypeStruct((n, M_CONV_CH), F32), jax.ShapeDtypeStruct((n, M_DT_PAD), F32)],
        compiler_params=_cp(("arbitrary",)),
        name="mamba_prep_step",
    )(zm, zm, conv_state, conv_state, conv_state, *weights)


def _ssd_post(y, z, nw):
    y = y * (z * jax.nn.sigmoid(z))
    half = WIDTH // M_GROUPS
    parts = []
    for gi in range(M_GROUPS):
        yg = y[:, gi * half:(gi + 1) * half]
        parts.append(yg * lax.rsqrt(jnp.mean(yg * yg, axis=-1, keepdims=True) + M_EPS))
    return jnp.concatenate(parts, axis=1) * nw


def _ssd_chunk_kernel(xbc_ref, dt_ref, z_ref, h0_ref, alog_ref, dsk_ref, nw_ref, o_ref, ht_ref, h_scr):
    j = pl.program_id(1)

    @pl.when(j == 0)
    def _():
        h_scr[...] = h0_ref[...]

    xbc, dt = xbc_ref[...], dt_ref[...]
    q = xbc.shape[0]
    a_row = -jnp.exp(alog_ref[...])
    dsk = dsk_ref[...]
    row = lax.broadcasted_iota(jnp.int32, (q, q), 0)
    col = lax.broadcasted_iota(jnp.int32, (q, q), 1)
    lower = row >= col
    tri = _bf(lower.astype(F32))
    heads_per_group = M_HEADS // M_GROUPS
    ys = []
    for gi in range(M_GROUPS):
        bg = xbc[:, WIDTH + gi * M_STATE:WIDTH + (gi + 1) * M_STATE]
        cg = xbc[:, WIDTH + (M_GROUPS + gi) * M_STATE:WIDTH + (M_GROUPS + gi + 1) * M_STATE]
        cb = _dot_nt(cg, bg)
        for e in range(heads_per_group):
            h = gi * heads_per_group + e
            cs_ = slice(h * M_HEAD, (h + 1) * M_HEAD)
            xh = xbc[:, cs_]
            dtc = dt[:, h:h + 1]
            dac = dtc * a_row[:, h * M_HEAD:h * M_HEAD + 1]
            seg = _dot_exact_lhs(tri, jnp.where(row > col, jnp.broadcast_to(dac, (q, q)), 0.0))
            lm = jnp.where(lower, jnp.exp(seg), 0.0)
            cum = _dot_exact_lhs(tri, jnp.broadcast_to(dac, (q, M_HEAD)))
            cum_last = cum[q - 1:q, :]
            xd = xh * dtc
            hs = h_scr[h]
            y = _dot(cb * lm, xd) + _dot_nt(cg, hs) * jnp.exp(cum) + dsk[:, cs_] * xh
            ys.append(y)
            keep = jnp.exp(jnp.concatenate([cum_last, cum_last], axis=1))
            h_scr[h] = hs * keep + _dot_tn(xd * jnp.exp(cum_last - cum), bg)
    o_ref[...] = _ssd_post(jnp.concatenate(ys, axis=1), z_ref[...], nw_ref[...])

    @pl.when(j == pl.num_programs(1) - 1)
    def _():
        ht_ref[...] = h_scr[...]


def _ssd_chunk(xbc, dt, zm, h0, weights):
    bsz, t, _ = xbc.shape
    q = min(SSD_CHUNK, t)
    tile = lambda c: pl.BlockSpec((None, q, c), lambda b, j: (b, j, 0))
    st = pl.BlockSpec((None, M_HEADS, M_HEAD, M_STATE), lambda b, j: (b, 0, 0, 0))
    return pl.pallas_call(
        _ssd_chunk_kernel,
        grid=(bsz, t // q),
        in_specs=[tile(M_CONV_CH), tile(M_DT_PAD), pl.BlockSpec((None, q, WIDTH), lambda b, j: (b, j, M_Z_BLOCK)),
                  st] + [_full(w.shape) for w in weights],
        out_specs=[tile(WIDTH), st],
        out_shape=[jax.ShapeDtypeStruct((bsz, t, WIDTH), F32),
                   jax.ShapeDtypeStruct((bsz, M_HEADS, M_HEAD, M_STATE), F32)],
        scratch_shapes=[pltpu.VMEM((M_HEADS, M_HEAD, M_STATE), F32)],
        compiler_params=_cp(("arbitrary", "arbitrary")),
        name="ssd_chunk",
    )(xbc, dt, zm, h0, *weights)


def _ssd_step_kernel(xbc_ref, dt_ref, z_ref, h0_ref, alog_ref, dsk_ref, nw_ref, expand_ref, o_ref, ht_ref, y_scr):
    bt = xbc_ref.shape[0]
    xbc = xbc_ref[...]
    xs = xbc[:, :WIDTH]
    dt_full = _dot_exact_rhs(dt_ref[...], expand_ref[...], terms=3)
    keep = jnp.exp(dt_full * -jnp.exp(alog_ref[...]))
    xd = xs * dt_full
    heads_per_group = M_HEADS // M_GROUPS
    zero = jnp.zeros((1, 1), F32)
    for i in range(bt):
        one = slice(i, i + 1)
        xrow = _rows(xd[one], jnp.broadcast_to(zero, (1, WIDTH)))
        brow = _rows(xbc[one, WIDTH:WIDTH + M_GROUPS * M_STATE], jnp.broadcast_to(zero, (1, M_GROUPS * M_STATE)))
        crow = _rows(xbc[one, WIDTH + M_GROUPS * M_STATE:], jnp.broadcast_to(zero, (1, M_GROUPS * M_STATE)))
        outs = []
        for h in range(M_HEADS):
            gi = h // heads_per_group
            cs_ = slice(h * M_HEAD, (h + 1) * M_HEAD)
            gs = slice(gi * M_STATE, (gi + 1) * M_STATE)
            kp = keep[one, cs_]
            hn = h0_ref[i, h] * jnp.concatenate([kp, kp], axis=1) + _dot_tn(xrow[:, cs_], brow[:, gs])
            ht_ref[i, h] = hn
            outs.append(_dot_nt(crow[:, gs], hn))
        y_scr[one, :] = jnp.concatenate(outs, axis=1)[0:1]
    y = y_scr[...] + dsk_ref[...] * xs
    o_ref[...] = _ssd_post(y, z_ref[...], nw_ref[...])


def _ssd_step(xbc, dt, z, h0, weights):
    n = xbc.shape[0]
    bt = SUBLANES
    tile = lambda c: pl.BlockSpec((bt, c), lambda i: (i, 0))
    st = pl.BlockSpec((bt, M_HEADS, M_HEAD, M_STATE), lambda i: (i, 0, 0, 0))
    return pl.pallas_call(
        _ssd_step_kernel,
        grid=(n // bt,),
        in_specs=[tile(M_CONV_CH), tile(M_DT_PAD), pl.BlockSpec((bt, WIDTH), lambda i: (i, M_Z_BLOCK)), st]
        + [_full(w.shape) for w in weights],
        out_specs=[tile(WIDTH), st],
        out_shape=[jax.ShapeDtypeStruct((n, WIDTH), F32),
                   jax.ShapeDtypeStruct((n, M_HEADS, M_HEAD, M_STATE), F32)],
        scratch_shapes=[pltpu.VMEM((bt, WIDTH), F32)],
        compiler_params=_cp(("arbitrary",)),
        name="ssd_step",
    )(xbc, dt, z, h0, *weights)


def _merge_kernel(or_ref, os_ref, om_ref, zg_ref, x_ref, wb_ref, wo_ref, g_ref, o_ref):
    zg = zg_ref[...]
    mixed = None
    for kk, ref in enumerate((or_ref, os_ref, om_ref)):
        proj = jnp.dot(_bf(ref[...]), wb_ref[kk], preferred_element_type=F32)
        term = jax.nn.sigmoid(zg[:, kk * D_MODEL:(kk + 1) * D_MODEL]) * proj
        mixed = term if mixed is None else mixed + term
    out = jnp.dot(_bf(mixed), wo_ref[...], preferred_element_type=F32)
    o_ref[...] = x_ref[...] + _rms(out, g_ref[...])


def _merge(o_r, o_s, o_m, zg, x, wb, wo, g):
    n = x.shape[0]
    tm = min(TOKEN_TILE, n)
    tile = lambda c: pl.BlockSpec((tm, c), lambda i: (i, 0))
    return pl.pallas_call(
        _merge_kernel,
        grid=(n // tm,),
        in_specs=[tile(WIDTH)] * 3 + [tile(3 * D_MODEL), tile(D_MODEL), _full(wb.shape), _full(wo.shape),
                                      _full(g.shape)],
        out_specs=tile(D_MODEL),
        out_shape=jax.ShapeDtypeStruct((n, D_MODEL), F32),
        compiler_params=_cp(("arbitrary",)),
        name="merge",
    )(o_r, o_s, o_m, zg, x, wb, wo, g)


def _ffn_math(gate_taps, val, x, cw, cb, wd, g):
    acc = gate_taps[FFN_CONV - 1] * cw[0:1]
    for kk in range(1, FFN_CONV):
        acc = acc + gate_taps[FFN_CONV - 1 - kk] * cw[kk:kk + 1]
    acc = acc + cb
    f = jnp.dot(_bf(jax.nn.gelu(acc) * val), wd, preferred_element_type=F32)
    return x + _rms(f, g)


def _ffn_seq_kernel(up_ref, x_ref, init_ref, cw_ref, cb_ref, wd_ref, g_ref, o_ref, last_ref, carry):
    @pl.when(pl.program_id(1) == 0)
    def _():
        carry[...] = init_ref[...]

    gate = up_ref[:, :D_FF]
    c8 = carry[...]
    taps = [gate] + [_delayed(gate, c8, kk) for kk in range(1, FFN_CONV)]
    tail = gate[gate.shape[0] - SUBLANES:, :]
    carry[...] = tail
    last_ref[...] = tail
    o_ref[...] = _ffn_math(taps, up_ref[:, D_FF:], x_ref[...], cw_ref[...], cb_ref[...], wd_ref[...], g_ref[...])


def _ffn_step_kernel(up_ref, x_ref, p1_ref, p2_ref, cw_ref, cb_ref, wd_ref, g_ref, o_ref):
    taps = [up_ref[:, :D_FF], p1_ref[...], p2_ref[...]]
    o_ref[...] = _ffn_math(taps, up_ref[:, D_FF:], x_ref[...], cw_ref[...], cb_ref[...], wd_ref[...], g_ref[...])


def _ffn(up, x, conv_state, weights, seq):
    wspecs = [_full(w.shape) for w in weights]
    if seq:
        bsz, t, _ = up.shape
        tm = min(TOKEN_TILE, t)
        tile = lambda c: pl.BlockSpec((None, tm, c), lambda b, j: (b, j, 0))
        edge = pl.BlockSpec((None, SUBLANES, D_FF), lambda b, j: (b, 0, 0))
        return pl.pallas_call(
            _ffn_seq_kernel,
            grid=(bsz, t // tm),
            in_specs=[tile(2 * D_FF), tile(D_MODEL), edge] + wspecs,
            out_specs=[tile(D_MODEL), edge],
            out_shape=[jax.ShapeDtypeStruct((bsz, t, D_MODEL), F32),
                       jax.ShapeDtypeStruct((bsz, SUBLANES, D_FF), F32)],
            scratch_shapes=[pltpu.VMEM((SUBLANES, D_FF), F32)],
            compiler_params=_cp(("arbitrary", "arbitrary")),
            name="ffn_seq",
        )(up, x, conv_state, *weights)
    n = up.shape[0]
    past = [pl.BlockSpec((n, D_FF), lambda i, kk=kk: (0, FFN_CONV - 1 - kk)) for kk in range(1, FFN_CONV)]
    return pl.pallas_call(
        _ffn_step_kernel,
        grid=(1,),
        in_specs=[_full((n, 2 * D_FF)), _full((n, D_MODEL))] + past + wspecs,
        out_specs=_full((n, D_MODEL)),
        out_shape=jax.ShapeDtypeStruct((n, D_MODEL), F32),
        compiler_params=_cp(("arbitrary",)),
        name="ffn_step",
    )(up, x, conv_state, conv_state, *weights)


def _row(v):
    return v.reshape(1, -1).astype(F32)


def _block_diag_ones():
    head = jnp.arange(WIDTH) // R_HEAD
    return (head[:, None] == head[None, :]).astype(BF16)


def _layer_weights(lw):
    (g_pre_mix, g_post_mix, g_pre_ffn, g_post_ffn, w_in,
     r_mu, r_w0, r_w2, r_a0, r_a2, r_g2, r_kk, r_ka, r_rk, r_ln_w, r_ln_b,
     s5_lam_re, s5_lam_im, s5_log_step, s5_b_re, s5_b_im, s5_c_re, s5_c_im, s5_d, s5_w_glu, s5_b_glu,
     m_conv_w, m_conv_b, m_dt_bias, m_a_log, m_d, m_norm_w,
     w_branch, w_out, w_up, f_conv_w, f_conv_b, w_down) = lw
    c0 = R_COLS
    c1 = c0 + WIDTH
    c2 = c1 + WIDTH + M_CONV_CH
    c3 = c2 + M_HEADS
    ones_bd = _block_diag_ones()
    w = {}
    w["g_pre_mix"], w["g_post_mix"] = _row(g_pre_mix), _row(g_post_mix)
    w["g_pre_ffn"], w["g_post_ffn"] = _row(g_pre_ffn), _row(g_post_ffn)
    w["w_r"] = _bf(w_in[:, :c0])
    w["w_s"] = _bf(w_in[:, c0:c1])
    w["w_m"] = _bf(jnp.concatenate(
        [w_in[:, c1 + WIDTH:c2], w_in[:, c1:c1 + WIDTH], w_in[:, c2:c3],
         jnp.zeros((D_MODEL, M_DT_PAD - M_HEADS), F32)], axis=1))
    w["w_g"] = _bf(w_in[:, c3:])
    w["rwkv_prep"] = (_row(r_mu), _row(r_w0), _bf(r_w2), _row(r_a0), _bf(r_a2), _bf(r_g2), _row(r_kk), _row(r_ka),
                      ones_bd)
    w["rwkv_post"] = (_row(r_rk), _row(r_ln_w), _row(r_ln_b), ones_bd)
    ab_re, ab_im, bb_re, bb_im = _s5_params(s5_lam_re, s5_lam_im, s5_log_step, s5_b_re, s5_b_im)
    gpb = S5_GROUPS // S5_BLOCKS
    eye = jnp.eye(gpb, dtype=F32)
    blocked = lambda m: m.reshape(S5_BLOCKS, gpb, S5_GROUP, S5_STATE)
    to_state = lambda bb: _bf(jnp.einsum('qghp,gk->qghkp', blocked(bb), eye)
                              .reshape(S5_BLOCKS, S5_BLOCK_IN, S5_BLOCK_STATE))
    from_state = lambda c: _bf(jnp.einsum('qghp,gk->qgpkh', blocked(c), eye)
                               .reshape(S5_BLOCKS, S5_BLOCK_STATE, S5_BLOCK_IN))
    w["s5"] = (to_state(bb_re), to_state(bb_im), from_state(s5_c_re), from_state(s5_c_im),
               ab_re.reshape(1, S5_LANES), ab_im.reshape(1, S5_LANES), _row(s5_d), _bf(s5_w_glu), _row(s5_b_glu))
    w["mamba_prep"] = (m_conv_w.T.astype(F32), _row(m_conv_b),
                       _row(jnp.pad(m_dt_bias, (0, M_DT_PAD - M_HEADS))))
    head_rows = (_row(jnp.repeat(m_a_log, M_HEAD)), _row(jnp.repeat(m_d, M_HEAD)), _row(m_norm_w))
    w["ssd"] = head_rows
    lane_head = jnp.arange(WIDTH) // M_HEAD
    w["ssd_step"] = head_rows + ((jnp.arange(M_DT_PAD)[:, None] == lane_head[None, :]).astype(BF16),)
    w["w_branch"], w["w_out"] = _bf(w_branch), _bf(w_out)
    w["w_up"] = _bf(w_up)
    w["ffn"] = (f_conv_w.T.astype(F32), _row(f_conv_b), _bf(w_down), _row(g_post_ffn))
    return w


def _pad_rows(state):
    return jnp.pad(state, ((0, 0), (SUBLANES - state.shape[1], 0), (0, 0)))


def _layer(x, states, w, bsz, t):
    shift0, wkv0, s5r0, s5i0, ssd0, mconv0, fconv0 = states
    seq = t > 1
    n = bsz * t
    view = (lambda a: a.reshape(bsz, t, a.shape[-1])) if seq else (lambda a: a)
    flat = lambda a: a.reshape(n, a.shape[-1])

    z_r = _norm_matmul(x, w["g_pre_mix"], w["w_r"], "in_proj_rwkv")
    z_s = _norm_matmul(x, w["g_pre_mix"], w["w_s"], "in_proj_s5")
    z_m = _norm_matmul(x, w["g_pre_mix"], w["w_m"], "in_proj_mamba")
    z_g = _norm_matmul(x, w["g_pre_mix"], w["w_g"], "in_proj_gates")

    if seq:
        *streams, last = _rwkv_prep(view(z_r), jnp.broadcast_to(shift0[:, None], (bsz, SUBLANES, R_COLS)),
                                    w["rwkv_prep"], True)
        shift1 = last[:, SUBLANES - 1]
        o_r, wkv1 = _rwkv_chunk(streams, wkv0, w["rwkv_post"])
    else:
        streams = _rwkv_prep(z_r, shift0, w["rwkv_prep"], False)
        shift1 = z_r
        o_r, wkv1 = _rwkv_step(streams, wkv0, w["rwkv_post"])

    o_s, s5r1, s5i1 = _s5(view(z_s), s5r0.reshape(bsz, S5_LANES), s5i0.reshape(bsz, S5_LANES), w["s5"], seq)
    s5r1 = s5r1.reshape(bsz, S5_GROUPS, S5_STATE)
    s5i1 = s5i1.reshape(bsz, S5_GROUPS, S5_STATE)

    if seq:
        xbc_c, dt, last = _mamba_prep(view(z_m), _pad_rows(mconv0), w["mamba_prep"], True)
        mconv1 = last[:, SUBLANES - (M_CONV - 1):]
        o_m, ssd1 = _ssd_chunk(xbc_c, dt, view(z_m), ssd0, w["ssd"])
    else:
        xbc_c, dt = _mamba_prep(z_m, mconv0.reshape(bsz, (M_CONV - 1) * M_CONV_CH), w["mamba_prep"], False)
        mconv1 = jnp.concatenate([mconv0[:, 1:], z_m[:, None, :M_CONV_CH]], axis=1)
        o_m, ssd1 = _ssd_step(xbc_c, dt, z_m, ssd0, w["ssd_step"])

    x = _merge(flat(o_r), flat(o_s), flat(o_m), z_g, x, w["w_branch"], w["w_out"], w["g_post_mix"])

    up = _norm_matmul(x, w["g_pre_ffn"], w["w_up"], "ffn_up")
    if seq:
        x3, last = _ffn(view(up), view(x), _pad_rows(fconv0), w["ffn"], True)
        x = flat(x3)
        fconv1 = last[:, SUBLANES - (FFN_CONV - 1):]
    else:
        x = _ffn(up, x, fconv0.reshape(bsz, (FFN_CONV - 1) * D_FF), w["ffn"], False)
        fconv1 = jnp.concatenate([fconv0[:, 1:], up[:, None, :D_FF]], axis=1)
    return x, (shift1, wkv1, s5r1, s5i1, ssd1, mconv1, fconv1)


def _zero_states(n):
    return (jnp.zeros((n, R_COLS), F32),
            jnp.zeros((n, R_HEADS, R_HEAD, R_HEAD), F32),
            jnp.zeros((n, S5_GROUPS, S5_STATE), F32),
            jnp.zeros((n, S5_GROUPS, S5_STATE), F32),
            jnp.zeros((n, M_HEADS, M_HEAD, M_STATE), F32),
            jnp.zeros((n, M_CONV - 1, M_CONV_CH), F32),
            jnp.zeros((n, FFN_CONV - 1, D_FF), F32))


def kernel(x_prompt, x_sample, state_rwkv_shift, state_rwkv_wkv, state_s5_re, state_s5_im, state_ssd, state_ssd_conv, state_ffn_conv, g_pre_mix, g_post_mix, g_pre_ffn, g_post_ffn, w_in, r_mu, r_w0, r_w2, r_a0, r_a2, r_g2, r_kk, r_ka, r_rk, r_ln_w, r_ln_b, s5_lam_re, s5_lam_im, s5_log_step, s5_b_re, s5_b_im, s5_c_re, s5_c_im, s5_d, s5_w_glu, s5_b_glu, m_conv_w, m_conv_b, m_dt_bias, m_a_log, m_d, m_norm_w, w_branch, w_out, w_up, f_conv_w, f_conv_b, w_down):
    stacked = (g_pre_mix, g_post_mix, g_pre_ffn, g_post_ffn, w_in,
               r_mu, r_w0, r_w2, r_a0, r_a2, r_g2, r_kk, r_ka, r_rk, r_ln_w, r_ln_b,
               s5_lam_re, s5_lam_im, s5_log_step, s5_b_re, s5_b_im, s5_c_re, s5_c_im, s5_d,
               s5_w_glu, s5_b_glu,
               m_conv_w, m_conv_b, m_dt_bias, m_a_log, m_d, m_norm_w,
               w_branch, w_out, w_up, f_conv_w, f_conv_b, w_down)
    cache_in = (state_rwkv_shift, state_rwkv_wkv, state_s5_re, state_s5_im,
                state_ssd, state_ssd_conv, state_ffn_conv)
    depth = w_in.shape[0]
    pb, pt, d = x_prompt.shape
    sb, s_t, _ = x_sample.shape
    xp = x_prompt.reshape(pb * pt, d)
    xs = x_sample.reshape(sb * s_t, d)
    new_p = [[] for _ in cache_in]
    new_s = [[] for _ in cache_in]
    for l in range(depth):
        w = _layer_weights(tuple(a[l] for a in stacked))
        xp, sp = _layer(xp, _zero_states(pb), w, pb, pt)
        xs, ss = _layer(xs, tuple(c[l] for c in cache_in), w, sb, s_t)
        for i in range(len(cache_in)):
            new_p[i].append(sp[i])
            new_s[i].append(ss[i])
    outs = [xp.reshape(pb, pt, d), xs.reshape(sb, s_t, d)]
    for p_list, s_list in zip(new_p, new_s):
        outs.append(jnp.stack(p_list, 0))
        outs.append(jnp.stack(s_list, 0))
    return tuple(outs)
```

```python
import functools

import jax
import jax.numpy as jnp
from jax import lax
from jax.experimental import pallas as pl
from jax.experimental.pallas import tpu as pltpu

F32 = jnp.float32
BF16 = jnp.bfloat16

D_MODEL = 1024
WIDTH = 512
R_HEADS, R_HEAD = 8, 64
R_COLS = 1792
R_LN_EPS = 64e-5
S5_GROUPS, S5_GROUP, S5_STATE = 32, 16, 64
S5_LANES = S5_GROUPS * S5_STATE
M_HEADS, M_HEAD, M_GROUPS, M_STATE = 8, 64, 2, 128
M_CONV, M_CONV_CH = 4, 1024
M_DT_PAD = 256
M_COLS_PAD = M_CONV_CH + WIDTH + M_DT_PAD
M_EPS = 1e-5
D_FF = 2816
FFN_CONV = 3
EPS = 1e-6
SUBLANES = 8
LANES = 128

RWKV_CHUNK = 64
RWKV_TILE = 128
SSD_CHUNK = 128
S5_CHUNK = 128
S5_ROW_PAD = 8
TOKEN_TILE = 256
FFN_COL_CHUNK = D_FF // 2
VMEM_LIMIT = 56 * 1024 * 1024


def _cp(sem):
    return pltpu.CompilerParams(dimension_semantics=sem, vmem_limit_bytes=VMEM_LIMIT)


def _const(shape):
    nd = len(shape)
    return pl.BlockSpec(shape, lambda *_: (0,) * nd, pipeline_mode=pl.Buffered(1))


def _full(shape):
    nd = len(shape)
    return pl.BlockSpec(shape, lambda *_: (0,) * nd)


def _bf(x):
    return x.astype(BF16)


def _dot(a, b):
    return jnp.dot(_bf(a), _bf(b), preferred_element_type=F32)


def _dot_nt(a, b):
    return lax.dot_general(_bf(a), _bf(b), (((1,), (1,)), ((), ())), preferred_element_type=F32)


def _dot_tn(a, b):
    return lax.dot_general(_bf(a), _bf(b), (((0,), (0,)), ((), ())), preferred_element_type=F32)


def _split(x, terms):
    out = []
    for _ in range(terms - 1):
        h = _bf(x)
        out.append(h)
        x = x - h.astype(F32)
    out.append(_bf(x))
    return out


def _dot_exact_lhs(m_bf16, x, terms=3):
    acc = None
    for h in _split(x, terms):
        p = jnp.dot(m_bf16, h, preferred_element_type=F32)
        acc = p if acc is None else acc + p
    return acc


def _dot_exact_rhs(x, m_bf16, terms=2):
    acc = None
    for h in _split(x, terms):
        p = jnp.dot(h, m_bf16, preferred_element_type=F32)
        acc = p if acc is None else acc + p
    return acc


def _softplus(x):
    return jnp.maximum(x, 0.0) + jnp.log1p(jnp.exp(-jnp.abs(x)))


def _rms(x, g, eps=EPS):
    return x * lax.rsqrt(jnp.mean(x * x, axis=-1, keepdims=True) + eps) * g


def _delayed(x, carry8, k):
    rx = pltpu.roll(x, k, 0)
    rc = pltpu.roll(carry8, k, 0)
    row = lax.broadcasted_iota(jnp.int32, (SUBLANES, x.shape[1]), 0)
    head = jnp.where(row < k, rc, rx[:SUBLANES])
    if x.shape[0] == SUBLANES:
        return head
    return jnp.concatenate([head, rx[SUBLANES:]], axis=0)


def _rows(first, second):
    row = lax.broadcasted_iota(jnp.int32, (SUBLANES, first.shape[1]), 0)
    return jnp.where(row == 0, first, jnp.where(row == 1, second, 0.0))


def _norm_matmul_kernel(x_ref, g_ref, w_ref, o_ref):
    o_ref[...] = jnp.dot(_bf(_rms(x_ref[...], g_ref[...])), w_ref[...], preferred_element_type=F32)


def _norm_matmul(x, g, w, name):
    n, d = x.shape
    c = w.shape[1]
    return pl.pallas_call(
        _norm_matmul_kernel,
        grid=(1,),
        in_specs=[_full((n, d)), _full((1, d)), _full((d, c))],
        out_specs=_full((n, c)),
        out_shape=jax.ShapeDtypeStruct((n, c), F32),
        compiler_params=_cp(("arbitrary",)),
        name=name,
    )(x, g, w)


def _rwkv_prep_math(z, prev, mu, w0, w2, a0, a2, g2, kkw, kaw, ones_bd):
    zm = z + (prev - z) * mu
    r = zm[:, 0:WIDTH]
    k = zm[:, WIDTH:2 * WIDTH]
    v = zm[:, 2 * WIDTH:3 * WIDTH]
    dw = zm[:, 1536:1600]
    da = zm[:, 1600:1664]
    dg = zm[:, 1664:1792]
    logw = -_softplus(-(w0 + _dot(jnp.tanh(dw), w2))) - 0.5
    ld = -jnp.exp(logw)
    a = jax.nn.sigmoid(a0 + _dot(da, a2))
    g = _dot(jax.nn.sigmoid(dg), g2)
    kk = k * kkw
    ss = _dot_exact_rhs(kk * kk, ones_bd)
    kk = kk / jnp.maximum(jnp.sqrt(ss), 1e-12)
    k2 = k * (1.0 + (a - 1.0) * kaw)
    return r, ld, k2, v, kk, kk * a, g


def _rwkv_post(y, r, k, v, g, rk, lnw, lnb, ones_bd):
    inv = 1.0 / R_HEAD
    mean = _dot_exact_rhs(y, ones_bd) * inv
    d = y - mean
    var = _dot_exact_rhs(d * d, ones_bd) * inv
    yn = d * lax.rsqrt(var + R_LN_EPS) * lnw + lnb
    bonus = _dot_exact_rhs(r * k * rk, ones_bd) * v
    return (yn + bonus) * g


def _rwkv_chunks(r, ld, k, v, kk, b, s_scr):
    L = RWKV_CHUNK
    pair = 2 * R_HEAD
    n_pairs = R_HEADS // 2
    tm = r.shape[0]
    n_chunks = tm // L
    trow = lax.broadcasted_iota(jnp.int32, (tm, tm), 0)
    tcol = lax.broadcasted_iota(jnp.int32, (tm, tm), 1)
    same_chunk = (trow // L) == (tcol // L)
    cum = _dot_exact_lhs(_bf(((trow >= tcol) & same_chunk).astype(F32)), ld)
    wc = jnp.exp(cum)
    winv = jnp.exp(-cum)
    r_t = r * wc
    kk_t = kk * jnp.exp(cum - ld)
    k_h = k * winv
    b_h = b * winv

    row = lax.broadcasted_iota(jnp.int32, (pair, pair), 0)
    col = lax.broadcasted_iota(jnp.int32, (pair, pair), 1)
    same_head = (row // R_HEAD) == (col // R_HEAD)
    strict = same_head & ((row % R_HEAD) > (col % R_HEAD))
    lower = same_head & ((row % R_HEAD) >= (col % R_HEAD))

    def bd(x):
        return jnp.where(same_head, jnp.concatenate([x, x], axis=0), 0.0)

    units = [(c, p) for c in range(n_chunks) for p in range(n_pairs)]
    pre = {}
    for c, p in units:
        rows = slice(c * L, (c + 1) * L)
        lanes = slice(p * pair, (p + 1) * pair)
        wl = wc[(c + 1) * L - 1:(c + 1) * L, lanes]
        pre[c, p] = dict(kkt=bd(kk_t[rows, lanes]), rt=bd(r_t[rows, lanes]), kh=bd(k_h[rows, lanes]),
                         bh=bd(b_h[rows, lanes]), v=bd(v[rows, lanes]), wl=wl)
    for u in units:
        d = pre[u]
        a = _dot_nt(jnp.concatenate([d["kkt"], d["rt"]], axis=0), jnp.concatenate([d["kh"], d["bh"]], axis=0))
        d["akk_k"] = jnp.where(strict, a[:pair, :pair], 0.0)
        d["n"] = jnp.where(strict, a[:pair, pair:], 0.0)
        d["ar_k"] = jnp.where(lower, a[pair:, :pair], 0.0)
        d["ar_b"] = jnp.where(lower, a[pair:, pair:], 0.0)
        d["q"] = -d["n"]
        d["m"] = d["n"]
    power = 2
    while power < L:
        for u in units:
            d = pre[u]
            d["m"] = _dot(d["m"], d["m"])
            d["q"] = d["q"] + d["m"] + _dot(d["q"], d["m"])
        power *= 2
    for u in units:
        d = pre[u]
        x = _dot(d["akk_k"], d["v"])
        both = jnp.concatenate([d["kkt"], x], axis=1)
        both = both + _dot(d["q"], both)
        d["g"], d["u0"] = both[:, :pair], both[:, pair:]
    for u in units:
        d = pre[u]
        t = _dot(d["ar_b"], jnp.concatenate([d["g"], d["u0"]], axis=1))
        d["ry"] = d["rt"] - t[:, :pair]
        d["y0"] = _dot(d["ar_k"], d["v"]) - t[:, pair:]
        kw = d["kh"] * d["wl"]
        bw = d["bh"] * d["wl"]
        d["pm"] = _dot_tn(d["g"], bw)
        d["c"] = _dot_tn(jnp.concatenate([d["v"], -d["u0"]], axis=0), jnp.concatenate([kw, bw], axis=0))
    ys = []
    for c in range(n_chunks):
        parts = []
        for p in range(n_pairs):
            d = pre[c, p]
            s = s_scr[p]
            y_bd = _dot_nt(d["ry"], s) + d["y0"]
            s_scr[p] = s * d["wl"] - _dot(s, d["pm"]) + d["c"]
            parts.append(y_bd[:L] + y_bd[L:])
        ys.append(jnp.concatenate(parts, axis=1))
    return jnp.concatenate(ys, axis=0)


def _rwkv_seq_kernel(x_ref, init_ref, s0_ref, gn_ref, wr_ref, mu_ref, w0_ref, w2_ref, a0_ref, a2_ref, g2_ref,
                     kkw_ref, kaw_ref, ones_ref, rk_ref, lnw_ref, lnb_ref, o_ref, last_ref, st_ref, carry, s_scr):
    j = pl.program_id(1)
    n_pairs = R_HEADS // 2

    @pl.when(j == 0)
    def _():
        carry[...] = init_ref[...]
        zero = jnp.zeros((R_HEAD, R_HEAD), F32)
        for p in range(n_pairs):
            top = jnp.concatenate([s0_ref[2 * p], zero], axis=1)
            bot = jnp.concatenate([zero, s0_ref[2 * p + 1]], axis=1)
            s_scr[p] = jnp.concatenate([top, bot], axis=0)

    z = jnp.dot(_bf(_rms(x_ref[...], gn_ref[...])), wr_ref[...], preferred_element_type=F32)
    prev = _delayed(z, carry[...], 1)
    tail = z[z.shape[0] - SUBLANES:, :]
    carry[...] = tail
    last_ref[...] = tail
    ones_bd = ones_ref[...]
    r, ld, k, v, kk, b, g = _rwkv_prep_math(z, prev, mu_ref[...], w0_ref[...], w2_ref[...], a0_ref[...],
                                            a2_ref[...], g2_ref[...], kkw_ref[...], kaw_ref[...], ones_bd)
    y = _rwkv_chunks(r, ld, k, v, kk, b, s_scr)
    o_ref[...] = _rwkv_post(y, r, k, v, g, rk_ref[...], lnw_ref[...], lnb_ref[...], ones_bd)

    @pl.when(j == pl.num_programs(1) - 1)
    def _():
        for p in range(n_pairs):
            s = s_scr[p]
            st_ref[2 * p] = s[:R_HEAD, :R_HEAD]
            st_ref[2 * p + 1] = s[R_HEAD:, R_HEAD:]


def _rwkv_seq(x, shift, s0, weights):
    bsz, t, d = x.shape
    tm = min(RWKV_TILE, t)
    tile = lambda c: pl.BlockSpec((None, tm, c), lambda b, j: (b, j, 0))
    edge = pl.BlockSpec((None, SUBLANES, R_COLS), lambda b, j: (b, 0, 0))
    st = pl.BlockSpec((None, R_HEADS, R_HEAD, R_HEAD), lambda b, j: (b, 0, 0, 0))
    return pl.pallas_call(
        _rwkv_seq_kernel,
        grid=(bsz, t // tm),
        in_specs=[tile(d), edge, st] + [_const(w.shape) for w in weights],
        out_specs=[tile(WIDTH), edge, st],
        out_shape=[jax.ShapeDtypeStruct((bsz, t, WIDTH), F32),
                   jax.ShapeDtypeStruct((bsz, SUBLANES, R_COLS), F32),
                   jax.ShapeDtypeStruct((bsz, R_HEADS, R_HEAD, R_HEAD), F32)],
        scratch_shapes=[pltpu.VMEM((SUBLANES, R_COLS), F32),
                        pltpu.VMEM((R_HEADS // 2, 2 * R_HEAD, 2 * R_HEAD), F32)],
        compiler_params=_cp(("arbitrary", "arbitrary")),
        name="rwkv_seq",
    )(x, shift, s0, *weights)


def _rwkv_step_kernel(z_ref, prev_ref, s0_ref, mu_ref, w0_ref, w2_ref, a0_ref, a2_ref, g2_ref, kkw_ref, kaw_ref,
                      ones_ref, rk_ref, lnw_ref, lnb_ref, o_ref, st_ref, y_scr):
    bt = z_ref.shape[0]
    ones_bd = ones_ref[...]
    r, ld, k, v, kk, b, g = _rwkv_prep_math(z_ref[...], prev_ref[...], mu_ref[...], w0_ref[...], w2_ref[...],
                                            a0_ref[...], a2_ref[...], g2_ref[...], kkw_ref[...], kaw_ref[...],
                                            ones_bd)
    w = jnp.exp(ld)
    wr = w * r
    b_dot_r = _dot_exact_rhs(b * r, ones_bd)
    k_dot_r = _dot_exact_rhs(k * r, ones_bd)

    for i in range(bt):
        one = slice(i, i + 1)
        lhs = _rows(kk[one], wr[one])
        sa_rows = []
        for h in range(R_HEADS):
            cs = slice(h * R_HEAD, (h + 1) * R_HEAD)
            sa_rows.append(_dot_nt(lhs[:, cs], s0_ref[i, h]))
        proj = jnp.concatenate(sa_rows, axis=1)
        sa = proj[0:1]
        y_scr[one, :] = proj[1:2] - sa * b_dot_r[one] + v[one] * k_dot_r[one]
        left = _rows(v[one], -sa)
        right = _rows(k[one], b[one])
        l_hi, l_lo = _split(left, 2)
        r_hi, r_lo = _split(right, 2)
        for h in range(R_HEADS):
            cs = slice(h * R_HEAD, (h + 1) * R_HEAD)
            tn = lambda x, y: lax.dot_general(x[:, cs], y[:, cs], (((0,), (0,)), ((), ())),
                                              preferred_element_type=F32)
            upd = tn(l_hi, r_hi) + tn(l_hi, r_lo) + tn(l_lo, r_hi)
            st_ref[i, h] = s0_ref[i, h] * w[one, cs] + upd
    o_ref[...] = _rwkv_post(y_scr[...], r, k, v, g, rk_ref[...], lnw_ref[...], lnb_ref[...], ones_bd)


def _rwkv_step(z, shift, s0, weights):
    n = z.shape[0]
    bt = SUBLANES
    tile = lambda c: pl.BlockSpec((bt, c), lambda i: (i, 0))
    st = pl.BlockSpec((bt, R_HEADS, R_HEAD, R_HEAD), lambda i: (i, 0, 0, 0))
    return pl.pallas_call(
        _rwkv_step_kernel,
        grid=(n // bt,),
        in_specs=[tile(R_COLS), tile(R_COLS), st] + [_const(w.shape) for w in weights],
        out_specs=[tile(WIDTH), st],
        out_shape=[jax.ShapeDtypeStruct((n, WIDTH), F32),
                   jax.ShapeDtypeStruct((n, R_HEADS, R_HEAD, R_HEAD), F32)],
        scratch_shapes=[pltpu.VMEM((bt, WIDTH), F32)],
        compiler_params=_cp(("arbitrary",)),
        name="rwkv_step",
    )(z, shift, s0, *weights)


S5_BLOCKS = 4
S5_BLOCK_IN = WIDTH // S5_BLOCKS
S5_BLOCK_STATE = S5_LANES // S5_BLOCKS
S5_BLOCK_TILES = S5_BLOCK_STATE // LANES


def _s5_param_kernel(lr_ref, li_ref, ls_ref, br_ref, bi_ref, abr_ref, abi_ref, bbr_ref, bbi_ref):
    lr, li = lr_ref[...], li_ref[...]
    delta = jnp.exp(ls_ref[...])
    mag = jnp.exp(lr * delta)
    ab_re = mag * jnp.cos(li * delta)
    ab_im = mag * jnp.sin(li * delta)
    den = lr * lr + li * li
    nr, ni = ab_re - 1.0, ab_im
    cf_re = (nr * lr + ni * li) / den
    cf_im = (ni * lr - nr * li) / den
    abr_ref[...] = ab_re
    abi_ref[...] = ab_im
    br, bi = br_ref[...], bi_ref[...]
    bbr_ref[...] = cf_re[:, None, :] * br - cf_im[:, None, :] * bi
    bbi_ref[...] = cf_re[:, None, :] * bi + cf_im[:, None, :] * br


def _s5_params(lam_re, lam_im, log_step, b_re, b_im):
    g, p, h = b_re.shape
    gp = jax.ShapeDtypeStruct((g, p), F32)
    ghp = jax.ShapeDtypeStruct((g, h, p), F32)
    args = (lam_re, lam_im, log_step.reshape(g, 1), jnp.swapaxes(b_re, 1, 2), jnp.swapaxes(b_im, 1, 2))
    return pl.pallas_call(
        _s5_param_kernel,
        grid=(1,),
        in_specs=[_full(a.shape) for a in args],
        out_specs=[_full((g, p)), _full((g, p)), _full((g, h, p)), _full((g, h, p))],
        out_shape=[gp, gp, ghp, ghp],
        compiler_params=_cp(("arbitrary",)),
        name="s5_params",
    )(*args)


def _s5_drive(ub, wbr_ref, wbi_ref, blk):
    cols = slice(blk * S5_BLOCK_IN, (blk + 1) * S5_BLOCK_IN)
    return (jnp.dot(ub[:, cols], wbr_ref[blk], preferred_element_type=F32),
            jnp.dot(ub[:, cols], wbi_ref[blk], preferred_element_type=F32))


def _s5_readout(u, xr_blocks, xi_blocks, wcr_ref, wci_ref, d, wg, bg):
    y = jnp.concatenate([_dot(xr_blocks[q], wcr_ref[q]) - _dot(xi_blocks[q], wci_ref[q])
                         for q in range(S5_BLOCKS)], axis=1) + d * u
    y = jax.nn.gelu(y)
    return y * jax.nn.sigmoid(_dot(y, wg) + bg)


def _s5_seq_kernel(x_ref, s0r_ref, s0i_ref, gn_ref, ws_ref, wbr_ref, wbi_ref, wcr_ref, wci_ref, ar_ref, ai_ref,
                   d_ref, wg_ref, bg_ref, o_ref, str_ref, sti_ref, xr_scr, xi_scr, cr_scr, ci_scr, *, bt, tc):
    @pl.when(pl.program_id(0) == 0)
    def _():
        cr_scr[...] = s0r_ref[...]
        ci_scr[...] = s0i_ref[...]

    pitch = tc + S5_ROW_PAD
    x = x_ref[...].reshape(bt * tc, D_MODEL)
    u = jnp.dot(_bf(_rms(x, gn_ref[...])), ws_ref[...], preferred_element_type=F32)
    ub = _bf(u)
    for q in range(S5_BLOCKS):
        bu_re, bu_im = _s5_drive(ub, wbr_ref, wbi_ref, q)
        for c in range(S5_BLOCK_TILES):
            for s in range(bt):
                xr_scr[q * S5_BLOCK_TILES + c, s * pitch:s * pitch + tc, :] = (
                    bu_re[s * tc:(s + 1) * tc, c * LANES:(c + 1) * LANES])
                xi_scr[q * S5_BLOCK_TILES + c, s * pitch:s * pitch + tc, :] = (
                    bu_im[s * tc:(s + 1) * tc, c * LANES:(c + 1) * LANES])

    for q in range(S5_BLOCKS):
        tiles = range(q * S5_BLOCK_TILES, (q + 1) * S5_BLOCK_TILES)
        ar = [jnp.broadcast_to(ar_ref[:, c * LANES:(c + 1) * LANES], (bt, LANES)) for c in tiles]
        ai = [jnp.broadcast_to(ai_ref[:, c * LANES:(c + 1) * LANES], (bt, LANES)) for c in tiles]

        def body(t, carry, tiles=tiles, ar=ar, ai=ai):
            rows = pl.ds(t, bt, stride=pitch)
            out = []
            for n, c in enumerate(tiles):
                xr, xi = carry[2 * n], carry[2 * n + 1]
                nr = ar[n] * xr - ai[n] * xi + xr_scr[c, rows, :]
                ni = ar[n] * xi + ai[n] * xr + xi_scr[c, rows, :]
                xr_scr[c, rows, :] = nr
                xi_scr[c, rows, :] = ni
                out += [nr, ni]
            return tuple(out)

        init = []
        for c in tiles:
            init += [cr_scr[:, c * LANES:(c + 1) * LANES], ci_scr[:, c * LANES:(c + 1) * LANES]]
        fin = lax.fori_loop(0, tc, body, tuple(init), unroll=2)
        for n, c in enumerate(tiles):
            cr_scr[:, c * LANES:(c + 1) * LANES] = fin[2 * n]
            ci_scr[:, c * LANES:(c + 1) * LANES] = fin[2 * n + 1]
    str_ref[...] = cr_scr[...]
    sti_ref[...] = ci_scr[...]

    def blocks(scr):
        out = []
        for q in range(S5_BLOCKS):
            cols = [jnp.concatenate([scr[q * S5_BLOCK_TILES + c, s * pitch:s * pitch + tc, :] for s in range(bt)],
                                    axis=0) for c in range(S5_BLOCK_TILES)]
            out.append(jnp.concatenate(cols, axis=1))
        return out

    y = _s5_readout(u, blocks(xr_scr), blocks(xi_scr), wcr_ref, wci_ref, d_ref[...], wg_ref[...], bg_ref[...])
    o_ref[...] = y.reshape(bt, tc, WIDTH)


def _s5_step_kernel(x_ref, s0r_ref, s0i_ref, gn_ref, ws_ref, wbr_ref, wbi_ref, wcr_ref, wci_ref, ar_ref, ai_ref,
                    d_ref, wg_ref, bg_ref, o_ref, str_ref, sti_ref):
    u = jnp.dot(_bf(_rms(x_ref[...], gn_ref[...])), ws_ref[...], preferred_element_type=F32)
    ub = _bf(u)
    xr_blocks, xi_blocks = [], []
    for q in range(S5_BLOCKS):
        lanes = slice(q * S5_BLOCK_STATE, (q + 1) * S5_BLOCK_STATE)
        ar, ai = ar_ref[:, lanes], ai_ref[:, lanes]
        sr, si = s0r_ref[:, lanes], s0i_ref[:, lanes]
        bu_re, bu_im = _s5_drive(ub, wbr_ref, wbi_ref, q)
        xr = ar * sr - ai * si + bu_re
        xi = ar * si + ai * sr + bu_im
        str_ref[:, lanes] = xr
        sti_ref[:, lanes] = xi
        xr_blocks.append(xr)
        xi_blocks.append(xi)
    o_ref[...] = _s5_readout(u, xr_blocks, xi_blocks, wcr_ref, wci_ref, d_ref[...], wg_ref[...], bg_ref[...])


def _s5(x, s0r, s0i, weights, seq):
    wspecs = [_const(w.shape) for w in weights]
    if seq:
        bsz, t, d = x.shape
        tc = min(S5_CHUNK, t)
        st = _full((bsz, S5_LANES))
        return pl.pallas_call(
            functools.partial(_s5_seq_kernel, bt=bsz, tc=tc),
            grid=(t // tc,),
            in_specs=[pl.BlockSpec((bsz, tc, d), lambda j: (0, j, 0)), st, st] + wspecs,
            out_specs=[pl.BlockSpec((bsz, tc, WIDTH), lambda j: (0, j, 0)), st, st],
            out_shape=[jax.ShapeDtypeStruct((bsz, t, WIDTH), F32)] + [jax.ShapeDtypeStruct((bsz, S5_LANES), F32)] * 2,
            scratch_shapes=[pltpu.VMEM((S5_LANES // LANES, bsz * (tc + S5_ROW_PAD), LANES), F32)] * 2
            + [pltpu.VMEM((bsz, S5_LANES), F32)] * 2,
            compiler_params=_cp(("arbitrary",)),
            name="s5_seq",
        )(x, s0r, s0i, *weights)
    n, d = x.shape
    st = _full((n, S5_LANES))
    return pl.pallas_call(
        _s5_step_kernel,
        grid=(1,),
        in_specs=[_full((n, d)), st, st] + wspecs,
        out_specs=[_full((n, WIDTH)), st, st],
        out_shape=[jax.ShapeDtypeStruct((n, WIDTH), F32)] + [jax.ShapeDtypeStruct((n, S5_LANES), F32)] * 2,
        compiler_params=_cp(("arbitrary",)),
        name="s5_step",
    )(x, s0r, s0i, *weights)


def _mamba_prep_math(xbc_taps, dt_raw, cw, cb, dtb):
    acc = xbc_taps[M_CONV - 1] * cw[0:1]
    for kk in range(1, M_CONV):
        acc = acc + xbc_taps[M_CONV - 1 - kk] * cw[kk:kk + 1]
    acc = acc + cb
    return acc * jax.nn.sigmoid(acc), _softplus(dt_raw + dtb)


def _ssd_post(y, z, nw):
    y = y * (z * jax.nn.sigmoid(z))
    half = WIDTH // M_GROUPS
    parts = []
    for gi in range(M_GROUPS):
        yg = y[:, gi * half:(gi + 1) * half]
        parts.append(yg * lax.rsqrt(jnp.mean(yg * yg, axis=-1, keepdims=True) + M_EPS))
    return jnp.concatenate(parts, axis=1) * nw


def _ssd_chunk_math(xbc, dt, alog_row, acol, dsk, expand, h_scr):
    q = xbc.shape[0]
    pair = 2 * M_HEAD
    xs = xbc[:, :WIDTH]
    row = lax.broadcasted_iota(jnp.int32, (q, q), 0)
    col = lax.broadcasted_iota(jnp.int32, (q, q), 1)
    lower = row >= col
    dt_full = _dot_exact_rhs(dt, expand, terms=3)
    cum = _dot_exact_lhs(_bf(lower.astype(F32)), dt_full * -jnp.exp(alog_row))
    da_t = jnp.transpose(dt[:, :LANES])[:SUBLANES] * -jnp.exp(acol)
    cum_t = _dot_exact_rhs(da_t, _bf((row <= col).astype(F32)), terms=3)
    ecum = jnp.exp(cum)
    cum_last = cum[q - 1:q, :]
    xd = xs * dt_full
    xdec = xd * jnp.exp(cum_last - cum)
    lane = lax.broadcasted_iota(jnp.int32, (q, pair), 1)
    prow = lax.broadcasted_iota(jnp.int32, (pair, M_STATE), 0)
    heads_per_group = M_HEADS // M_GROUPS
    ys = []
    for gi in range(M_GROUPS):
        bg = xbc[:, WIDTH + gi * M_STATE:WIDTH + (gi + 1) * M_STATE]
        cg = xbc[:, WIDTH + (M_GROUPS + gi) * M_STATE:WIDTH + (M_GROUPS + gi + 1) * M_STATE]
        cb = _dot_nt(cg, bg)
        for pp in range(heads_per_group // 2):
            p = gi * (heads_per_group // 2) + pp
            lanes = slice(p * pair, (p + 1) * pair)
            ms = []
            for h in (2 * p, 2 * p + 1):
                ch = cum[:, h * M_HEAD:(h + 1) * M_HEAD]
                seg = jnp.concatenate([ch] * (q // M_HEAD), axis=1) - cum_t[h:h + 1, :]
                ms.append(jnp.where(lower, jnp.exp(seg), 0.0) * cb)
            xd_pair = xd[:, lanes]
            stacked = jnp.concatenate([jnp.where(lane < M_HEAD, xd_pair, 0.0),
                                       jnp.where(lane >= M_HEAD, xd_pair, 0.0)], axis=0)
            hs = h_scr[p]
            ys.append(_dot(jnp.concatenate(ms, axis=1), stacked) + _dot_nt(cg, hs) * ecum[:, lanes])
            keep = jnp.where(prow < M_HEAD, jnp.exp(cum_t[2 * p:2 * p + 1, q - 1:q]),
                             jnp.exp(cum_t[2 * p + 1:2 * p + 2, q - 1:q]))
            h_scr[p] = hs * keep + _dot_tn(xdec[:, lanes], bg)
    return jnp.concatenate(ys, axis=1) + dsk * xs


def _mamba_seq_kernel(x_ref, init_ref, h0_ref, gn_ref, wm_ref, cw_ref, cb_ref, dtb_ref, alog_ref, acol_ref, dsk_ref,
                      nw_ref, expand_ref, o_ref, last_ref, ht_ref, carry, h_scr):
    j = pl.program_id(1)
    n_pairs = M_HEADS // 2

    @pl.when(j == 0)
    def _():
        carry[...] = init_ref[...]
        for p in range(n_pairs):
            h_scr[p] = jnp.concatenate([h0_ref[2 * p], h0_ref[2 * p + 1]], axis=0)

    zm = jnp.dot(_bf(_rms(x_ref[...], gn_ref[...])), wm_ref[...], preferred_element_type=F32)
    raw = zm[:, :M_CONV_CH]
    z = zm[:, M_CONV_CH:M_CONV_CH + WIDTH]
    c8 = carry[...]
    taps = [raw] + [_delayed(raw, c8, kk) for kk in range(1, M_CONV)]
    tail = raw[raw.shape[0] - SUBLANES:, :]
    carry[...] = tail
    last_ref[...] = tail
    xbc, dt = _mamba_prep_math(taps, zm[:, M_CONV_CH + WIDTH:], cw_ref[...], cb_ref[...], dtb_ref[...])
    y = _ssd_chunk_math(xbc, dt, alog_ref[...], acol_ref[...], dsk_ref[...], expand_ref[...], h_scr)
    o_ref[...] = _ssd_post(y, z, nw_ref[...])

    @pl.when(j == pl.num_programs(1) - 1)
    def _():
        for p in range(n_pairs):
            hs = h_scr[p]
            ht_ref[2 * p] = hs[:M_HEAD]
            ht_ref[2 * p + 1] = hs[M_HEAD:]


def _mamba_seq(x, conv_state, h0, weights):
    bsz, t, d = x.shape
    q = min(SSD_CHUNK, t)
    tile = lambda c: pl.BlockSpec((None, q, c), lambda b, j: (b, j, 0))
    edge = pl.BlockSpec((None, SUBLANES, M_CONV_CH), lambda b, j: (b, 0, 0))
    st = pl.BlockSpec((None, M_HEADS, M_HEAD, M_STATE), lambda b, j: (b, 0, 0, 0))
    return pl.pallas_call(
        _mamba_seq_kernel,
        grid=(bsz, t // q),
        in_specs=[tile(d), edge, st] + [_const(w.shape) for w in weights],
        out_specs=[tile(WIDTH), edge, st],
        out_shape=[jax.ShapeDtypeStruct((bsz, t, WIDTH), F32),
                   jax.ShapeDtypeStruct((bsz, SUBLANES, M_CONV_CH), F32),
                   jax.ShapeDtypeStruct((bsz, M_HEADS, M_HEAD, M_STATE), F32)],
        scratch_shapes=[pltpu.VMEM((SUBLANES, M_CONV_CH), F32),
                        pltpu.VMEM((M_HEADS // 2, 2 * M_HEAD, M_STATE), F32)],
        compiler_params=_cp(("arbitrary", "arbitrary")),
        name="mamba_seq",
    )(x, conv_state, h0, *weights)


def _mamba_step_kernel(zm_ref, past_ref, h0_ref, cw_ref, cb_ref, dtb_ref, alog_ref, dsk_ref, nw_ref, expand_ref,
                       o_ref, ht_ref, y_scr):
    bt = zm_ref.shape[0]
    zm = zm_ref[...]
    raw = zm[:, :M_CONV_CH]
    z = zm[:, M_CONV_CH:M_CONV_CH + WIDTH]
    past = past_ref[...]
    taps = [raw] + [past[:, (M_CONV - 1 - kk) * M_CONV_CH:(M_CONV - kk) * M_CONV_CH] for kk in range(1, M_CONV)]
    xbc, dt = _mamba_prep_math(taps, zm[:, M_CONV_CH + WIDTH:], cw_ref[...], cb_ref[...], dtb_ref[...])
    xs = xbc[:, :WIDTH]
    dt_full = _dot_exact_rhs(dt, expand_ref[...], terms=3)
    keep = jnp.exp(dt_full * -jnp.exp(alog_ref[...]))
    xd = xs * dt_full
    heads_per_group = M_HEADS // M_GROUPS
    zero_x = jnp.zeros((1, WIDTH), F32)
    zero_g = jnp.zeros((1, M_GROUPS * M_STATE), F32)
    for i in range(bt):
        one = slice(i, i + 1)
        xrow = _rows(xd[one], zero_x)
        brow = _rows(xbc[one, WIDTH:WIDTH + M_GROUPS * M_STATE], zero_g)
        crow = _rows(xbc[one, WIDTH + M_GROUPS * M_STATE:], zero_g)
        outs = []
        for h in range(M_HEADS):
            gi = h // heads_per_group
            cs_ = slice(h * M_HEAD, (h + 1) * M_HEAD)
            gs = slice(gi * M_STATE, (gi + 1) * M_STATE)
            kp = keep[one, cs_]
            hn = h0_ref[i, h] * jnp.concatenate([kp, kp], axis=1) + _dot_tn(xrow[:, cs_], brow[:, gs])
            ht_ref[i, h] = hn
            outs.append(_dot_nt(crow[:, gs], hn))
        y_scr[one, :] = jnp.concatenate(outs, axis=1)[0:1]
    y = y_scr[...] + dsk_ref[...] * xs
    o_ref[...] = _ssd_post(y, z, nw_ref[...])


def _mamba_step(zm, past, h0, weights):
    n = zm.shape[0]
    bt = SUBLANES
    tile = lambda c: pl.BlockSpec((bt, c), lambda i: (i, 0))
    st = pl.BlockSpec((bt, M_HEADS, M_HEAD, M_STATE), lambda i: (i, 0, 0, 0))
    return pl.pallas_call(
        _mamba_step_kernel,
        grid=(n // bt,),
        in_specs=[tile(M_COLS_PAD), tile((M_CONV - 1) * M_CONV_CH), st] + [_const(w.shape) for w in weights],
        out_specs=[tile(WIDTH), st],
        out_shape=[jax.ShapeDtypeStruct((n, WIDTH), F32),
                   jax.ShapeDtypeStruct((n, M_HEADS, M_HEAD, M_STATE), F32)],
        scratch_shapes=[pltpu.VMEM((bt, WIDTH), F32)],
        compiler_params=_cp(("arbitrary",)),
        name="mamba_step",
    )(zm, past, h0, *weights)


def _merge_kernel(x_ref, or_ref, os_ref, om_ref, gn_ref, wg_ref, wb_ref, wo_ref, gp_ref, o_ref):
    x = x_ref[...]
    hb = _bf(_rms(x, gn_ref[...]))
    mixed = None
    for kk, ref in enumerate((or_ref, os_ref, om_ref)):
        gate = jax.nn.sigmoid(jnp.dot(hb, wg_ref[:, kk * D_MODEL:(kk + 1) * D_MODEL], preferred_element_type=F32))
        term = gate * jnp.dot(_bf(ref[...]), wb_ref[kk], preferred_element_type=F32)
        mixed = term if mixed is None else mixed + term
    out = jnp.dot(_bf(mixed), wo_ref[...], preferred_element_type=F32)
    o_ref[...] = x + _rms(out, gp_ref[...])


def _merge(x, o_r, o_s, o_m, weights):
    n = x.shape[0]
    tm = min(TOKEN_TILE, n)
    tile = lambda c: pl.BlockSpec((tm, c), lambda i: (i, 0))
    return pl.pallas_call(
        _merge_kernel,
        grid=(n // tm,),
        in_specs=[tile(D_MODEL)] + [tile(WIDTH)] * 3 + [_const(w.shape) for w in weights],
        out_specs=tile(D_MODEL),
        out_shape=jax.ShapeDtypeStruct((n, D_MODEL), F32),
        compiler_params=_cp(("arbitrary",)),
        name="merge",
    )(x, o_r, o_s, o_m, *weights)


def _ffn_chunk(hb, gate_taps_of, cols, wu_ref, cw_ref, cb_ref, wd_ref):
    gate = jnp.dot(hb, wu_ref[:, cols], preferred_element_type=F32)
    val = jnp.dot(hb, wu_ref[:, D_FF + cols.start:D_FF + cols.stop], preferred_element_type=F32)
    taps = gate_taps_of(gate)
    acc = taps[FFN_CONV - 1] * cw_ref[0:1, cols]
    for kk in range(1, FFN_CONV):
        acc = acc + taps[FFN_CONV - 1 - kk] * cw_ref[kk:kk + 1, cols]
    acc = acc + cb_ref[:, cols]
    return jnp.dot(_bf(jax.nn.gelu(acc) * val), wd_ref[cols, :], preferred_element_type=F32), gate


def _ffn_seq_kernel(x_ref, init_ref, gn_ref, wu_ref, cw_ref, cb_ref, wd_ref, gp_ref, o_ref, last_ref, carry):
    @pl.when(pl.program_id(1) == 0)
    def _():
        carry[...] = init_ref[...]

    x = x_ref[...]
    hb = _bf(_rms(x, gn_ref[...]))
    f = None
    for c0 in range(0, D_FF, FFN_COL_CHUNK):
        cols = slice(c0, c0 + FFN_COL_CHUNK)
        c8 = carry[:, cols]
        part, gate = _ffn_chunk(hb, lambda g: [g] + [_delayed(g, c8, kk) for kk in range(1, FFN_CONV)], cols,
                                wu_ref, cw_ref, cb_ref, wd_ref)
        tail = gate[gate.shape[0] - SUBLANES:, :]
        carry[:, cols] = tail
        last_ref[:, cols] = tail
        f = part if f is None else f + part
    o_ref[...] = x + _rms(f, gp_ref[...])


def _ffn_step_kernel(x_ref, past_ref, gn_ref, wu_ref, cw_ref, cb_ref, wd_ref, gp_ref, o_ref, gate_ref):
    x = x_ref[...]
    hb = _bf(_rms(x, gn_ref[...]))
    f = None
    for c0 in range(0, D_FF, FFN_COL_CHUNK):
        cols = slice(c0, c0 + FFN_COL_CHUNK)
        past = [past_ref[:, (FFN_CONV - 1 - kk) * D_FF + c0:(FFN_CONV - 1 - kk) * D_FF + c0 + FFN_COL_CHUNK]
                for kk in range(1, FFN_CONV)]
        part, gate = _ffn_chunk(hb, lambda g: [g] + past, cols, wu_ref, cw_ref, cb_ref, wd_ref)
        gate_ref[:, cols] = gate
        f = part if f is None else f + part
    o_ref[...] = x + _rms(f, gp_ref[...])


def _ffn(x, conv_state, weights, seq):
    wspecs = [_const(w.shape) for w in weights]
    if seq:
        bsz, t, d = x.shape
        tm = min(TOKEN_TILE, t)
        tile = pl.BlockSpec((None, tm, d), lambda b, j: (b, j, 0))
        edge = pl.BlockSpec((None, SUBLANES, D_FF), lambda b, j: (b, 0, 0))
        return pl.pallas_call(
            _ffn_seq_kernel,
            grid=(bsz, t // tm),
            in_specs=[tile, edge] + wspecs,
            out_specs=[tile, edge],
            out_shape=[jax.ShapeDtypeStruct((bsz, t, d), F32),
                       jax.ShapeDtypeStruct((bsz, SUBLANES, D_FF), F32)],
            scratch_shapes=[pltpu.VMEM((SUBLANES, D_FF), F32)],
            compiler_params=_cp(("arbitrary", "arbitrary")),
            name="ffn_seq",
        )(x, conv_state, *weights)
    n, d = x.shape
    return pl.pallas_call(
        _ffn_step_kernel,
        grid=(1,),
        in_specs=[_full((n, d)), _full(conv_state.shape)] + wspecs,
        out_specs=[_full((n, d)), _full((n, D_FF))],
        out_shape=[jax.ShapeDtypeStruct((n, d), F32), jax.ShapeDtypeStruct((n, D_FF), F32)],
        compiler_params=_cp(("arbitrary",)),
        name="ffn_step",
    )(x, conv_state, *weights)


def _row(v):
    return v.reshape(1, -1).astype(F32)


def _block_diag_ones():
    head = jnp.arange(WIDTH) // R_HEAD
    return (head[:, None] == head[None, :]).astype(BF16)


def _layer_weights(lw):
    (g_pre_mix, g_post_mix, g_pre_ffn, g_post_ffn, w_in,
     r_mu, r_w0, r_w2, r_a0, r_a2, r_g2, r_kk, r_ka, r_rk, r_ln_w, r_ln_b,
     s5_lam_re, s5_lam_im, s5_log_step, s5_b_re, s5_b_im, s5_c_re, s5_c_im, s5_d, s5_w_glu, s5_b_glu,
     m_conv_w, m_conv_b, m_dt_bias, m_a_log, m_d, m_norm_w,
     w_branch, w_out, w_up, f_conv_w, f_conv_b, w_down) = lw
    c0 = R_COLS
    c1 = c0 + WIDTH
    c2 = c1 + WIDTH + M_CONV_CH
    c3 = c2 + M_HEADS
    ones_bd = _block_diag_ones()
    gn = _row(g_pre_mix)
    w = {"g_pre_mix": gn}
    w_r = _bf(w_in[:, :c0])
    w_m = _bf(jnp.concatenate(
        [w_in[:, c1 + WIDTH:c2], w_in[:, c1:c1 + WIDTH], w_in[:, c2:c3],
         jnp.zeros((D_MODEL, M_DT_PAD - M_HEADS), F32)], axis=1))
    w["w_r"], w["w_m"] = w_r, w_m
    rwkv_mix = (_row(r_mu), _row(r_w0), _bf(r_w2), _row(r_a0), _bf(r_a2), _bf(r_g2), _row(r_kk), _row(r_ka),
                ones_bd, _row(r_rk), _row(r_ln_w), _row(r_ln_b))
    w["rwkv_seq"] = (gn, w_r) + rwkv_mix
    w["rwkv_step"] = rwkv_mix
    ab_re, ab_im, bb_re, bb_im = _s5_params(s5_lam_re, s5_lam_im, s5_log_step, s5_b_re, s5_b_im)
    gpb = S5_GROUPS // S5_BLOCKS
    eye = jnp.eye(gpb, dtype=F32)
    blocked = lambda m: m.reshape(S5_BLOCKS, gpb, S5_GROUP, S5_STATE)
    to_state = lambda bb: _bf(jnp.einsum('qghp,gk->qghkp', blocked(bb), eye)
                              .reshape(S5_BLOCKS, S5_BLOCK_IN, S5_BLOCK_STATE))
    from_state = lambda c: _bf(jnp.einsum('qghp,gk->qgpkh', blocked(c), eye)
                               .reshape(S5_BLOCKS, S5_BLOCK_STATE, S5_BLOCK_IN))
    w["s5"] = (gn, _bf(w_in[:, c0:c1]), to_state(bb_re), to_state(bb_im), from_state(s5_c_re), from_state(s5_c_im),
               ab_re.reshape(1, S5_LANES), ab_im.reshape(1, S5_LANES), _row(s5_d), _bf(s5_w_glu), _row(s5_b_glu))
    conv = (m_conv_w.T.astype(F32), _row(m_conv_b), _row(jnp.pad(m_dt_bias, (0, M_DT_PAD - M_HEADS))))
    alog_row = _row(jnp.repeat(m_a_log, M_HEAD))
    tail = (_row(jnp.repeat(m_d, M_HEAD)), _row(m_norm_w),
            (jnp.arange(M_DT_PAD)[:, None] == (jnp.arange(WIDTH) // M_HEAD)[None, :]).astype(BF16))
    w["mamba_seq"] = (gn, w_m) + conv + (alog_row, m_a_log.reshape(M_HEADS, 1).astype(F32)) + tail
    w["mamba_step"] = conv + (alog_row,) + tail
    w["merge"] = (gn, _bf(w_in[:, c3:]), _bf(w_branch), _bf(w_out), _row(g_post_mix))
    w["ffn"] = (_row(g_pre_ffn), _bf(w_up), f_conv_w.T.astype(F32), _row(f_conv_b), _bf(w_down), _row(g_post_ffn))
    return w


def _pad_rows(state):
    return jnp.pad(state, ((0, 0), (SUBLANES - state.shape[1], 0), (0, 0)))


def _layer(x, states, w, bsz, t):
    shift0, wkv0, s5r0, s5i0, ssd0, mconv0, fconv0 = states
    seq = t > 1
    n = bsz * t
    d = x.shape[-1]
    flat = lambda a: a.reshape(n, a.shape[-1])
    s5r0 = s5r0.reshape(bsz, S5_LANES)
    s5i0 = s5i0.reshape(bsz, S5_LANES)
    if seq:
        x3 = x.reshape(bsz, t, d)
        o_r, last, wkv1 = _rwkv_seq(x3, jnp.broadcast_to(shift0[:, None], (bsz, SUBLANES, R_COLS)), wkv0,
                                    w["rwkv_seq"])
        shift1 = last[:, SUBLANES - 1]
        o_s, s5r1, s5i1 = _s5(x3, s5r0, s5i0, w["s5"], True)
        o_m, last, ssd1 = _mamba_seq(x3, _pad_rows(mconv0), ssd0, w["mamba_seq"])
        mconv1 = last[:, SUBLANES - (M_CONV - 1):]
        o_r, o_s, o_m = flat(o_r), flat(o_s), flat(o_m)
    else:
        z_r = _norm_matmul(x, w["g_pre_mix"], w["w_r"], "in_proj_rwkv")
        z_m = _norm_matmul(x, w["g_pre_mix"], w["w_m"], "in_proj_mamba")
        shift1 = z_r
        o_r, wkv1 = _rwkv_step(z_r, shift0, wkv0, w["rwkv_step"])
        o_s, s5r1, s5i1 = _s5(x, s5r0, s5i0, w["s5"], False)
        o_m, ssd1 = _mamba_step(z_m, mconv0.reshape(bsz, (M_CONV - 1) * M_CONV_CH), ssd0, w["mamba_step"])
        mconv1 = jnp.concatenate([mconv0[:, 1:], z_m[:, None, :M_CONV_CH]], axis=1)
    s5r1 = s5r1.reshape(bsz, S5_GROUPS, S5_STATE)
    s5i1 = s5i1.reshape(bsz, S5_GROUPS, S5_STATE)

    x = _merge(x, o_r, o_s, o_m, w["merge"])

    if seq:
        x3, last = _ffn(x.reshape(bsz, t, d), _pad_rows(fconv0), w["ffn"], True)
        x = flat(x3)
        fconv1 = last[:, SUBLANES - (FFN_CONV - 1):]
    else:
        x, gate = _ffn(x, fconv0.reshape(bsz, (FFN_CONV - 1) * D_FF), w["ffn"], False)
        fconv1 = jnp.concatenate([fconv0[:, 1:], gate[:, None]], axis=1)
    return x, (shift1, wkv1, s5r1, s5i1, ssd1, mconv1, fconv1)


def _zero_states(n):
    return (jnp.zeros((n, R_COLS), F32),
            jnp.zeros((n, R_HEADS, R_HEAD, R_HEAD), F32),
            jnp.zeros((n, S5_GROUPS, S5_STATE), F32),
            jnp.zeros((n, S5_GROUPS, S5_STATE), F32),
            jnp.zeros((n, M_HEADS, M_HEAD, M_STATE), F32),
            jnp.zeros((n, M_CONV - 1, M_CONV_CH), F32),
            jnp.zeros((n, FFN_CONV - 1, D_FF), F32))


def kernel(x_prompt, x_sample, state_rwkv_shift, state_rwkv_wkv, state_s5_re, state_s5_im, state_ssd, state_ssd_conv, state_ffn_conv, g_pre_mix, g_post_mix, g_pre_ffn, g_post_ffn, w_in, r_mu, r_w0, r_w2, r_a0, r_a2, r_g2, r_kk, r_ka, r_rk, r_ln_w, r_ln_b, s5_lam_re, s5_lam_im, s5_log_step, s5_b_re, s5_b_im, s5_c_re, s5_c_im, s5_d, s5_w_glu, s5_b_glu, m_conv_w, m_conv_b, m_dt_bias, m_a_log, m_d, m_norm_w, w_branch, w_out, w_up, f_conv_w, f_conv_b, w_down):
    stacked = (g_pre_mix, g_post_mix, g_pre_ffn, g_post_ffn, w_in,
               r_mu, r_w0, r_w2, r_a0, r_a2, r_g2, r_kk, r_ka, r_rk, r_ln_w, r_ln_b,
               s5_lam_re, s5_lam_im, s5_log_step, s5_b_re, s5_b_im, s5_c_re, s5_c_im, s5_d,
               s5_w_glu, s5_b_glu,
               m_conv_w, m_conv_b, m_dt_bias, m_a_log, m_d, m_norm_w,
               w_branch, w_out, w_up, f_conv_w, f_conv_b, w_down)
    cache_in = (state_rwkv_shift, state_rwkv_wkv, state_s5_re, state_s5_im,
                state_ssd, state_ssd_conv, state_ffn_conv)
    depth = w_in.shape[0]
    pb, pt, d = x_prompt.shape
    sb, s_t, _ = x_sample.shape
    xp = x_prompt.reshape(pb * pt, d)
    xs = x_sample.reshape(sb * s_t, d)
    new_p = [[] for _ in cache_in]
    new_s = [[] for _ in cache_in]
    for l in range(depth):
        w = _layer_weights(tuple(a[l] for a in stacked))
        xp, sp = _layer(xp, _zero_states(pb), w, pb, pt)
        xs, ss = _layer(xs, tuple(c[l] for c in cache_in), w, sb, s_t)
        for i in range(len(cache_in)):
            new_p[i].append(sp[i])
            new_s[i].append(ss[i])
    outs = [xp.reshape(pb, pt, d), xs.reshape(sb, s_t, d)]
    for p_list, s_list in zip(new_p, new_s):
        outs.append(jnp.stack(p_list, 0))
        outs.append(jnp.stack(s_list, 0))
    return tuple(outs)
```

```python
import functools

import jax
import jax.numpy as jnp
from jax import lax
from jax.experimental import pallas as pl
from jax.experimental.pallas import tpu as pltpu

F32 = jnp.float32
BF16 = jnp.bfloat16

D_MODEL = 1024
WIDTH = 512
R_HEADS, R_HEAD = 8, 64
R_COLS = 1792
R_LN_EPS = 64e-5
S5_GROUPS, S5_GROUP, S5_STATE = 32, 16, 64
S5_LANES = S5_GROUPS * S5_STATE
M_HEADS, M_HEAD, M_GROUPS, M_STATE = 8, 64, 2, 128
M_CONV, M_CONV_CH = 4, 1024
M_DT_PAD = 256
M_COLS_PAD = M_CONV_CH + WIDTH + M_DT_PAD
M_EPS = 1e-5
D_FF = 2816
FFN_CONV = 3
EPS = 1e-6
SUBLANES = 8
LANES = 128

RWKV_CHUNK = 64
RWKV_TILE = 256
SSD_CHUNK = 128
S5_CHUNK = 128
S5_ROW_PAD = 8
TOKEN_TILE = 512
FFN_COL_STARTS = (0, D_FF // 2, D_FF)
VMEM_LIMIT = 56 * 1024 * 1024


def _cp(sem):
    return pltpu.CompilerParams(dimension_semantics=sem, vmem_limit_bytes=VMEM_LIMIT)


def _const(shape):
    nd = len(shape)
    return pl.BlockSpec(shape, lambda *_: (0,) * nd, pipeline_mode=pl.Buffered(1))


def _full(shape):
    nd = len(shape)
    return pl.BlockSpec(shape, lambda *_: (0,) * nd)


def _bf(x):
    return x.astype(BF16)


def _dot(a, b):
    return jnp.dot(_bf(a), _bf(b), preferred_element_type=F32)


def _dot_nt(a, b):
    return lax.dot_general(_bf(a), _bf(b), (((1,), (1,)), ((), ())), preferred_element_type=F32)


def _dot_tn(a, b):
    return lax.dot_general(_bf(a), _bf(b), (((0,), (0,)), ((), ())), preferred_element_type=F32)


def _split(x, terms):
    out = []
    for _ in range(terms - 1):
        h = _bf(x)
        out.append(h)
        x = x - h.astype(F32)
    out.append(_bf(x))
    return out


def _dot_exact_lhs(m_bf16, x, terms=3):
    acc = None
    for h in _split(x, terms):
        p = jnp.dot(m_bf16, h, preferred_element_type=F32)
        acc = p if acc is None else acc + p
    return acc


def _dot_exact_rhs(x, m_bf16, terms=2):
    acc = None
    for h in _split(x, terms):
        p = jnp.dot(h, m_bf16, preferred_element_type=F32)
        acc = p if acc is None else acc + p
    return acc


def _softplus(x):
    return jnp.maximum(x, 0.0) + jnp.log1p(jnp.exp(-jnp.abs(x)))


def _rms(x, g, eps=EPS):
    return x * lax.rsqrt(jnp.mean(x * x, axis=-1, keepdims=True) + eps) * g


def _delayed(x, carry8, k):
    rx = pltpu.roll(x, k, 0)
    rc = pltpu.roll(carry8, k, 0)
    row = lax.broadcasted_iota(jnp.int32, (SUBLANES, x.shape[1]), 0)
    head = jnp.where(row < k, rc, rx[:SUBLANES])
    if x.shape[0] == SUBLANES:
        return head
    return jnp.concatenate([head, rx[SUBLANES:]], axis=0)


def _rows(first, second):
    row = lax.broadcasted_iota(jnp.int32, (SUBLANES, first.shape[1]), 0)
    return jnp.where(row == 0, first, jnp.where(row == 1, second, 0.0))


def _norm_matmul_kernel(x_ref, g_ref, w_ref, o_ref):
    o_ref[...] = jnp.dot(_bf(_rms(x_ref[...], g_ref[...])), w_ref[...], preferred_element_type=F32)


def _norm_matmul(x, g, w, name):
    n, d = x.shape
    c = w.shape[1]
    return pl.pallas_call(
        _norm_matmul_kernel,
        grid=(1,),
        in_specs=[_full((n, d)), _full((1, d)), _full((d, c))],
        out_specs=_full((n, c)),
        out_shape=jax.ShapeDtypeStruct((n, c), F32),
        compiler_params=_cp(("arbitrary",)),
        name=name,
    )(x, g, w)


def _rwkv_prep_math(z, prev, mu, w0, w2, a0, a2, g2, kkw, kaw, ones_bd):
    zm = z + (prev - z) * mu
    r = zm[:, 0:WIDTH]
    k = zm[:, WIDTH:2 * WIDTH]
    v = zm[:, 2 * WIDTH:3 * WIDTH]
    dw = zm[:, 1536:1600]
    da = zm[:, 1600:1664]
    dg = zm[:, 1664:1792]
    logw = -_softplus(-(w0 + _dot(jnp.tanh(dw), w2))) - 0.5
    ld = -jnp.exp(logw)
    a = jax.nn.sigmoid(a0 + _dot(da, a2))
    g = _dot(jax.nn.sigmoid(dg), g2)
    kk = k * kkw
    ss = _dot_exact_rhs(kk * kk, ones_bd)
    kk = kk / jnp.maximum(jnp.sqrt(ss), 1e-12)
    k2 = k * (1.0 + (a - 1.0) * kaw)
    return r, ld, k2, v, kk, kk * a, g


def _rwkv_post(y, r, k, v, g, rk, lnw, lnb, ones_bd):
    inv = 1.0 / R_HEAD
    mean = _dot_exact_rhs(y, ones_bd) * inv
    d = y - mean
    var = _dot_exact_rhs(d * d, ones_bd) * inv
    yn = d * lax.rsqrt(var + R_LN_EPS) * lnw + lnb
    bonus = _dot_exact_rhs(r * k * rk, ones_bd) * v
    return (yn + bonus) * g


def _rwkv_chunks(r, ld, k, v, kk, b, s_scr):
    L = RWKV_CHUNK
    pair = 2 * R_HEAD
    n_pairs = R_HEADS // 2
    tm = r.shape[0]
    n_chunks = tm // L
    trow = lax.broadcasted_iota(jnp.int32, (tm, tm), 0)
    tcol = lax.broadcasted_iota(jnp.int32, (tm, tm), 1)
    same_chunk = (trow // L) == (tcol // L)
    cum = _dot_exact_lhs(_bf(((trow >= tcol) & same_chunk).astype(F32)), ld)
    wc = jnp.exp(cum)
    winv = jnp.exp(-cum)
    r_t = r * wc
    kk_t = kk * jnp.exp(cum - ld)
    k_h = k * winv
    b_h = b * winv

    row = lax.broadcasted_iota(jnp.int32, (pair, pair), 0)
    col = lax.broadcasted_iota(jnp.int32, (pair, pair), 1)
    same_head = (row // R_HEAD) == (col // R_HEAD)
    lrow = lax.broadcasted_iota(jnp.int32, (L, pair), 0)
    lcol = lax.broadcasted_iota(jnp.int32, (L, pair), 1) % R_HEAD
    strict = lrow > lcol
    lower = lrow >= lcol

    def bd(x):
        return jnp.where(same_head, jnp.concatenate([x, x], axis=0), 0.0)

    units = [(c, p) for c in range(n_chunks) for p in range(n_pairs)]
    pre = {}
    for c, p in units:
        rows = slice(c * L, (c + 1) * L)
        lanes = slice(p * pair, (p + 1) * pair)
        wl = wc[(c + 1) * L - 1:(c + 1) * L, lanes]
        pre[c, p] = dict(kkt=kk_t[rows, lanes], rt=r_t[rows, lanes], kh=k_h[rows, lanes], bh=b_h[rows, lanes],
                         v=v[rows, lanes], vbd=bd(v[rows, lanes]), wl=wl)
    for u in units:
        d = pre[u]
        a = _dot_nt(jnp.concatenate([d["kkt"], d["rt"]], axis=0),
                    jnp.concatenate([bd(d["kh"]), bd(d["bh"])], axis=0))
        d["akk_k"] = jnp.where(strict, a[:L, :pair], 0.0)
        d["n"] = jnp.where(strict, a[:L, pair:], 0.0)
        d["ar_k"] = jnp.where(lower, a[L:, :pair], 0.0)
        d["ar_b"] = jnp.where(lower, a[L:, pair:], 0.0)
        d["q"] = -d["n"]
        d["m"] = d["n"]
    power = 2
    while power < L:
        for u in units:
            d = pre[u]
            d["m"] = _dot(d["m"], bd(d["m"]))
            d["q"] = d["q"] + d["m"] + _dot(d["q"], bd(d["m"]))
        power *= 2
    for u in units:
        d = pre[u]
        xy = _dot(jnp.concatenate([d["akk_k"], d["ar_k"]], axis=0), d["vbd"])
        x, d["y0"] = xy[:L], xy[L:]
        both = jnp.concatenate([d["kkt"], x], axis=1)
        both = both + _dot(d["q"], jnp.concatenate([bd(d["kkt"]), bd(x)], axis=1))
        d["g"], d["u0"] = both[:, :pair], both[:, pair:]
    for u in units:
        d = pre[u]
        t = _dot(d["ar_b"], jnp.concatenate([bd(d["g"]), bd(d["u0"])], axis=1))
        d["ry"] = d["rt"] - t[:, :pair]
        d["y0"] = d["y0"] - t[:, pair:]
        kw = d["kh"] * d["wl"]
        bw = d["bh"] * d["wl"]
        d["pm"] = jnp.where(same_head, _dot_tn(d["g"], bw), 0.0)
        d["c"] = jnp.where(same_head, _dot_tn(jnp.concatenate([d["v"], -d["u0"]], axis=0),
                                               jnp.concatenate([kw, bw], axis=0)), 0.0)
    ys = []
    for c in range(n_chunks):
        parts = []
        for p in range(n_pairs):
            d = pre[c, p]
            s = s_scr[p]
            parts.append(_dot_nt(d["ry"], s) + d["y0"])
            s_scr[p] = s * d["wl"] - _dot(s, d["pm"]) + d["c"]
        ys.append(jnp.concatenate(parts, axis=1))
    return jnp.concatenate(ys, axis=0)


def _rwkv_seq_kernel(x_ref, init_ref, s0_ref, gn_ref, wr_ref, mu_ref, w0_ref, w2_ref, a0_ref, a2_ref, g2_ref,
                     kkw_ref, kaw_ref, ones_ref, rk_ref, lnw_ref, lnb_ref, o_ref, last_ref, st_ref, carry, s_scr):
    j = pl.program_id(1)
    n_pairs = R_HEADS // 2

    @pl.when(j == 0)
    def _():
        carry[...] = init_ref[...]
        zero = jnp.zeros((R_HEAD, R_HEAD), F32)
        for p in range(n_pairs):
            top = jnp.concatenate([s0_ref[2 * p], zero], axis=1)
            bot = jnp.concatenate([zero, s0_ref[2 * p + 1]], axis=1)
            s_scr[p] = jnp.concatenate([top, bot], axis=0)

    z = jnp.dot(_bf(_rms(x_ref[...], gn_ref[...])), wr_ref[...], preferred_element_type=F32)
    prev = _delayed(z, carry[...], 1)
    tail = z[z.shape[0] - SUBLANES:, :]
    carry[...] = tail
    last_ref[...] = tail
    ones_bd = ones_ref[...]
    r, ld, k, v, kk, b, g = _rwkv_prep_math(z, prev, mu_ref[...], w0_ref[...], w2_ref[...], a0_ref[...],
                                            a2_ref[...], g2_ref[...], kkw_ref[...], kaw_ref[...], ones_bd)
    y = _rwkv_chunks(r, ld, k, v, kk, b, s_scr)
    o_ref[...] = _rwkv_post(y, r, k, v, g, rk_ref[...], lnw_ref[...], lnb_ref[...], ones_bd)

    @pl.when(j == pl.num_programs(1) - 1)
    def _():
        for p in range(n_pairs):
            s = s_scr[p]
            st_ref[2 * p] = s[:R_HEAD, :R_HEAD]
            st_ref[2 * p + 1] = s[R_HEAD:, R_HEAD:]


def _rwkv_seq(x, shift, s0, weights):
    bsz, t, d = x.shape
    tm = min(RWKV_TILE, t)
    tile = lambda c: pl.BlockSpec((None, tm, c), lambda b, j: (b, j, 0))
    edge = pl.BlockSpec((None, SUBLANES, R_COLS), lambda b, j: (b, 0, 0))
    st = pl.BlockSpec((None, R_HEADS, R_HEAD, R_HEAD), lambda b, j: (b, 0, 0, 0))
    return pl.pallas_call(
        _rwkv_seq_kernel,
        grid=(bsz, t // tm),
        in_specs=[tile(d), edge, st] + [_const(w.shape) for w in weights],
        out_specs=[tile(WIDTH), edge, st],
        out_shape=[jax.ShapeDtypeStruct((bsz, t, WIDTH), F32),
                   jax.ShapeDtypeStruct((bsz, SUBLANES, R_COLS), F32),
                   jax.ShapeDtypeStruct((bsz, R_HEADS, R_HEAD, R_HEAD), F32)],
        scratch_shapes=[pltpu.VMEM((SUBLANES, R_COLS), F32),
                        pltpu.VMEM((R_HEADS // 2, 2 * R_HEAD, 2 * R_HEAD), F32)],
        compiler_params=_cp(("arbitrary", "arbitrary")),
        name="rwkv_seq",
    )(x, shift, s0, *weights)


def _rwkv_step_kernel(z_ref, prev_ref, s0_ref, mu_ref, w0_ref, w2_ref, a0_ref, a2_ref, g2_ref, kkw_ref, kaw_ref,
                      ones_ref, rk_ref, lnw_ref, lnb_ref, o_ref, st_ref, y_scr):
    bt = z_ref.shape[0]
    ones_bd = ones_ref[...]
    r, ld, k, v, kk, b, g = _rwkv_prep_math(z_ref[...], prev_ref[...], mu_ref[...], w0_ref[...], w2_ref[...],
                                            a0_ref[...], a2_ref[...], g2_ref[...], kkw_ref[...], kaw_ref[...],
                                            ones_bd)
    w = jnp.exp(ld)
    wr = w * r
    b_dot_r = _dot_exact_rhs(b * r, ones_bd)
    k_dot_r = _dot_exact_rhs(k * r, ones_bd)

    projs = []
    for i in range(bt):
        one = slice(i, i + 1)
        lhs = _rows(kk[one], wr[one])
        projs.append(jnp.concatenate(
            [_dot_nt(lhs[:, h * R_HEAD:(h + 1) * R_HEAD], s0_ref[i, h]) for h in range(R_HEADS)], axis=1))
    pieces = []
    for i in range(bt):
        one = slice(i, i + 1)
        sa = projs[i][0:1]
        y_scr[one, :] = projs[i][1:2] - sa * b_dot_r[one] + v[one] * k_dot_r[one]
        pieces.append(_split(_rows(v[one], -sa), 2) + _split(_rows(k[one], b[one]), 2))
    for i in range(bt):
        one = slice(i, i + 1)
        l_hi, l_lo, r_hi, r_lo = pieces[i]
        for h in range(R_HEADS):
            cs = slice(h * R_HEAD, (h + 1) * R_HEAD)
            tn = lambda x, y: lax.dot_general(x[:, cs], y[:, cs], (((0,), (0,)), ((), ())),
                                              preferred_element_type=F32)
            upd = tn(l_hi, r_hi) + tn(l_hi, r_lo) + tn(l_lo, r_hi)
            st_ref[i, h] = s0_ref[i, h] * w[one, cs] + upd
    o_ref[...] = _rwkv_post(y_scr[...], r, k, v, g, rk_ref[...], lnw_ref[...], lnb_ref[...], ones_bd)


def _rwkv_step(z, shift, s_all, layer, weights):
    n = z.shape[0]
    bt = SUBLANES
    tile = lambda c: pl.BlockSpec((bt, c), lambda i: (i, 0))
    st = pl.BlockSpec((None, bt, R_HEADS, R_HEAD, R_HEAD), lambda i: (layer, i, 0, 0, 0))
    return pl.pallas_call(
        _rwkv_step_kernel,
        grid=(n // bt,),
        in_specs=[tile(R_COLS), tile(R_COLS), st] + [_const(w.shape) for w in weights],
        out_specs=[tile(WIDTH), st],
        out_shape=[jax.ShapeDtypeStruct((n, WIDTH), F32), jax.ShapeDtypeStruct(s_all.shape, F32)],
        scratch_shapes=[pltpu.VMEM((bt, WIDTH), F32)],
        input_output_aliases={2: 1},
        compiler_params=_cp(("arbitrary",)),
        name="rwkv_step",
    )(z, shift, s_all, *weights)


S5_BLOCKS = 4
S5_BLOCK_IN = WIDTH // S5_BLOCKS
S5_BLOCK_STATE = S5_LANES // S5_BLOCKS
S5_BLOCK_TILES = S5_BLOCK_STATE // LANES


def _s5_param_kernel(lr_ref, li_ref, ls_ref, br_ref, bi_ref, abr_ref, abi_ref, bbr_ref, bbi_ref):
    lr, li = lr_ref[...], li_ref[...]
    delta = jnp.exp(ls_ref[...])
    mag = jnp.exp(lr * delta)
    ab_re = mag * jnp.cos(li * delta)
    ab_im = mag * jnp.sin(li * delta)
    den = lr * lr + li * li
    nr, ni = ab_re - 1.0, ab_im
    cf_re = (nr * lr + ni * li) / den
    cf_im = (ni * lr - nr * li) / den
    abr_ref[...] = ab_re
    abi_ref[...] = ab_im
    br, bi = br_ref[...], bi_ref[...]
    bbr_ref[...] = cf_re[:, None, :] * br - cf_im[:, None, :] * bi
    bbi_ref[...] = cf_re[:, None, :] * bi + cf_im[:, None, :] * br


def _s5_params(lam_re, lam_im, log_step, b_re, b_im):
    g, p, h = b_re.shape
    gp = jax.ShapeDtypeStruct((g, p), F32)
    ghp = jax.ShapeDtypeStruct((g, h, p), F32)
    args = (lam_re, lam_im, log_step.reshape(g, 1), jnp.swapaxes(b_re, 1, 2), jnp.swapaxes(b_im, 1, 2))
    return pl.pallas_call(
        _s5_param_kernel,
        grid=(1,),
        in_specs=[_full(a.shape) for a in args],
        out_specs=[_full((g, p)), _full((g, p)), _full((g, h, p)), _full((g, h, p))],
        out_shape=[gp, gp, ghp, ghp],
        compiler_params=_cp(("arbitrary",)),
        name="s5_params",
    )(*args)


def _s5_drive(ub, wbr_ref, wbi_ref, blk):
    cols = slice(blk * S5_BLOCK_IN, (blk + 1) * S5_BLOCK_IN)
    return (jnp.dot(ub[:, cols], wbr_ref[blk], preferred_element_type=F32),
            jnp.dot(ub[:, cols], wbi_ref[blk], preferred_element_type=F32))


def _s5_readout(u, xr_blocks, xi_blocks, wcr_ref, wci_ref, d, wg, bg):
    y = jnp.concatenate([_dot(xr_blocks[q], wcr_ref[q]) - _dot(xi_blocks[q], wci_ref[q])
                         for q in range(S5_BLOCKS)], axis=1) + d * u
    y = jax.nn.gelu(y)
    return y * jax.nn.sigmoid(_dot(y, wg) + bg)


def _s5_seq_kernel(x_ref, s0r_ref, s0i_ref, gn_ref, ws_ref, wbr_ref, wbi_ref, wcr_ref, wci_ref, ar_ref, ai_ref,
                   d_ref, wg_ref, bg_ref, o_ref, str_ref, sti_ref, xr_scr, xi_scr, cr_scr, ci_scr, *, bt, tc):
    @pl.when(pl.program_id(0) == 0)
    def _():
        cr_scr[...] = s0r_ref[...]
        ci_scr[...] = s0i_ref[...]

    pitch = tc + S5_ROW_PAD
    x = x_ref[...].reshape(bt * tc, D_MODEL)
    u = jnp.dot(_bf(_rms(x, gn_ref[...])), ws_ref[...], preferred_element_type=F32)
    ub = _bf(u)
    for q in range(S5_BLOCKS):
        bu_re, bu_im = _s5_drive(ub, wbr_ref, wbi_ref, q)
        for c in range(S5_BLOCK_TILES):
            for s in range(bt):
                xr_scr[q * S5_BLOCK_TILES + c, s * pitch:s * pitch + tc, :] = (
                    bu_re[s * tc:(s + 1) * tc, c * LANES:(c + 1) * LANES])
                xi_scr[q * S5_BLOCK_TILES + c, s * pitch:s * pitch + tc, :] = (
                    bu_im[s * tc:(s + 1) * tc, c * LANES:(c + 1) * LANES])

    for q in range(S5_BLOCKS):
        tiles = range(q * S5_BLOCK_TILES, (q + 1) * S5_BLOCK_TILES)
        ar = [jnp.broadcast_to(ar_ref[:, c * LANES:(c + 1) * LANES], (bt, LANES)) for c in tiles]
        ai = [jnp.broadcast_to(ai_ref[:, c * LANES:(c + 1) * LANES], (bt, LANES)) for c in tiles]

        def body(t, carry, tiles=tiles, ar=ar, ai=ai):
            rows = pl.ds(t, bt, stride=pitch)
            out = []
            for n, c in enumerate(tiles):
                xr, xi = carry[2 * n], carry[2 * n + 1]
                nr = ar[n] * xr - ai[n] * xi + xr_scr[c, rows, :]
                ni = ar[n] * xi + ai[n] * xr + xi_scr[c, rows, :]
                xr_scr[c, rows, :] = nr
                xi_scr[c, rows, :] = ni
                out += [nr, ni]
            return tuple(out)

        init = []
        for c in tiles:
            init += [cr_scr[:, c * LANES:(c + 1) * LANES], ci_scr[:, c * LANES:(c + 1) * LANES]]
        fin = lax.fori_loop(0, tc, body, tuple(init), unroll=2)
        for n, c in enumerate(tiles):
            cr_scr[:, c * LANES:(c + 1) * LANES] = fin[2 * n]
            ci_scr[:, c * LANES:(c + 1) * LANES] = fin[2 * n + 1]
    str_ref[...] = cr_scr[...]
    sti_ref[...] = ci_scr[...]

    def blocks(scr):
        out = []
        for q in range(S5_BLOCKS):
            cols = [jnp.concatenate([scr[q * S5_BLOCK_TILES + c, s * pitch:s * pitch + tc, :] for s in range(bt)],
                                    axis=0) for c in range(S5_BLOCK_TILES)]
            out.append(jnp.concatenate(cols, axis=1))
        return out

    y = _s5_readout(u, blocks(xr_scr), blocks(xi_scr), wcr_ref, wci_ref, d_ref[...], wg_ref[...], bg_ref[...])
    o_ref[...] = y.reshape(bt, tc, WIDTH)


def _s5_step_kernel(x_ref, s0r_ref, s0i_ref, gn_ref, ws_ref, wbr_ref, wbi_ref, wcr_ref, wci_ref, ar_ref, ai_ref,
                    d_ref, wg_ref, bg_ref, o_ref, str_ref, sti_ref):
    u = jnp.dot(_bf(_rms(x_ref[...], gn_ref[...])), ws_ref[...], preferred_element_type=F32)
    ub = _bf(u)
    xr_blocks, xi_blocks = [], []
    for q in range(S5_BLOCKS):
        lanes = slice(q * S5_BLOCK_STATE, (q + 1) * S5_BLOCK_STATE)
        ar, ai = ar_ref[:, lanes], ai_ref[:, lanes]
        sr, si = s0r_ref[:, lanes], s0i_ref[:, lanes]
        bu_re, bu_im = _s5_drive(ub, wbr_ref, wbi_ref, q)
        xr = ar * sr - ai * si + bu_re
        xi = ar * si + ai * sr + bu_im
        str_ref[:, lanes] = xr
        sti_ref[:, lanes] = xi
        xr_blocks.append(xr)
        xi_blocks.append(xi)
    o_ref[...] = _s5_readout(u, xr_blocks, xi_blocks, wcr_ref, wci_ref, d_ref[...], wg_ref[...], bg_ref[...])


def _s5(x, s0r, s0i, weights, seq):
    wspecs = [_const(w.shape) for w in weights]
    if seq:
        bsz, t, d = x.shape
        tc = min(S5_CHUNK, t)
        st = _full((bsz, S5_LANES))
        return pl.pallas_call(
            functools.partial(_s5_seq_kernel, bt=bsz, tc=tc),
            grid=(t // tc,),
            in_specs=[pl.BlockSpec((bsz, tc, d), lambda j: (0, j, 0)), st, st] + wspecs,
            out_specs=[pl.BlockSpec((bsz, tc, WIDTH), lambda j: (0, j, 0)), st, st],
            out_shape=[jax.ShapeDtypeStruct((bsz, t, WIDTH), F32)] + [jax.ShapeDtypeStruct((bsz, S5_LANES), F32)] * 2,
            scratch_shapes=[pltpu.VMEM((S5_LANES // LANES, bsz * (tc + S5_ROW_PAD), LANES), F32)] * 2
            + [pltpu.VMEM((bsz, S5_LANES), F32)] * 2,
            compiler_params=_cp(("arbitrary",)),
            name="s5_seq",
        )(x, s0r, s0i, *weights)
    n, d = x.shape
    st = _full((n, S5_LANES))
    return pl.pallas_call(
        _s5_step_kernel,
        grid=(1,),
        in_specs=[_full((n, d)), st, st] + wspecs,
        out_specs=[_full((n, WIDTH)), st, st],
        out_shape=[jax.ShapeDtypeStruct((n, WIDTH), F32)] + [jax.ShapeDtypeStruct((n, S5_LANES), F32)] * 2,
        compiler_params=_cp(("arbitrary",)),
        name="s5_step",
    )(x, s0r, s0i, *weights)


def _mamba_prep_math(xbc_taps, dt_raw, cw, cb, dtb):
    acc = xbc_taps[M_CONV - 1] * cw[0:1]
    for kk in range(1, M_CONV):
        acc = acc + xbc_taps[M_CONV - 1 - kk] * cw[kk:kk + 1]
    acc = acc + cb
    return acc * jax.nn.sigmoid(acc), _softplus(dt_raw + dtb)


def _ssd_post(y, z, nw):
    y = y * (z * jax.nn.sigmoid(z))
    half = WIDTH // M_GROUPS
    parts = []
    for gi in range(M_GROUPS):
        yg = y[:, gi * half:(gi + 1) * half]
        parts.append(yg * lax.rsqrt(jnp.mean(yg * yg, axis=-1, keepdims=True) + M_EPS))
    return jnp.concatenate(parts, axis=1) * nw


def _ssd_chunk_math(xbc, dt, alog_row, acol, dsk, expand, h_scr):
    q = xbc.shape[0]
    pair = 2 * M_HEAD
    xs = xbc[:, :WIDTH]
    row = lax.broadcasted_iota(jnp.int32, (q, q), 0)
    col = lax.broadcasted_iota(jnp.int32, (q, q), 1)
    lower = row >= col
    dt_full = _dot_exact_rhs(dt, expand, terms=3)
    cum = _dot_exact_lhs(_bf(lower.astype(F32)), dt_full * -jnp.exp(alog_row))
    da_t = jnp.transpose(dt[:, :LANES])[:SUBLANES] * -jnp.exp(acol)
    cum_t = _dot_exact_rhs(da_t, _bf((row <= col).astype(F32)), terms=3)
    ecum = jnp.exp(cum)
    cum_last = cum[q - 1:q, :]
    xd = xs * dt_full
    xdec = xd * jnp.exp(cum_last - cum)
    lane = lax.broadcasted_iota(jnp.int32, (q, pair), 1)
    prow = lax.broadcasted_iota(jnp.int32, (pair, M_STATE), 0)
    heads_per_group = M_HEADS // M_GROUPS
    ys = []
    for gi in range(M_GROUPS):
        bg = xbc[:, WIDTH + gi * M_STATE:WIDTH + (gi + 1) * M_STATE]
        cg = xbc[:, WIDTH + (M_GROUPS + gi) * M_STATE:WIDTH + (M_GROUPS + gi + 1) * M_STATE]
        cb = _dot_nt(cg, bg)
        for pp in range(heads_per_group // 2):
            p = gi * (heads_per_group // 2) + pp
            lanes = slice(p * pair, (p + 1) * pair)
            ms = []
            for h in (2 * p, 2 * p + 1):
                ch = cum[:, h * M_HEAD:(h + 1) * M_HEAD]
                seg = jnp.concatenate([ch] * (q // M_HEAD), axis=1) - cum_t[h:h + 1, :]
                ms.append(jnp.where(lower, jnp.exp(seg), 0.0) * cb)
            xd_pair = xd[:, lanes]
            stacked = jnp.concatenate([jnp.where(lane < M_HEAD, xd_pair, 0.0),
                                       jnp.where(lane >= M_HEAD, xd_pair, 0.0)], axis=0)
            hs = h_scr[p]
            ys.append(_dot(jnp.concatenate(ms, axis=1), stacked) + _dot_nt(cg, hs) * ecum[:, lanes])
            keep = jnp.where(prow < M_HEAD, jnp.exp(cum_t[2 * p:2 * p + 1, q - 1:q]),
                             jnp.exp(cum_t[2 * p + 1:2 * p + 2, q - 1:q]))
            h_scr[p] = hs * keep + _dot_tn(xdec[:, lanes], bg)
    return jnp.concatenate(ys, axis=1) + dsk * xs


def _mamba_seq_kernel(x_ref, init_ref, h0_ref, gn_ref, wm_ref, cw_ref, cb_ref, dtb_ref, alog_ref, acol_ref, dsk_ref,
                      nw_ref, expand_ref, o_ref, last_ref, ht_ref, carry, h_scr):
    j = pl.program_id(1)
    n_pairs = M_HEADS // 2

    @pl.when(j == 0)
    def _():
        carry[...] = init_ref[...]
        for p in range(n_pairs):
            h_scr[p] = jnp.concatenate([h0_ref[2 * p], h0_ref[2 * p + 1]], axis=0)

    zm = jnp.dot(_bf(_rms(x_ref[...], gn_ref[...])), wm_ref[...], preferred_element_type=F32)
    raw = zm[:, :M_CONV_CH]
    z = zm[:, M_CONV_CH:M_CONV_CH + WIDTH]
    c8 = carry[...]
    taps = [raw] + [_delayed(raw, c8, kk) for kk in range(1, M_CONV)]
    tail = raw[raw.shape[0] - SUBLANES:, :]
    carry[...] = tail
    last_ref[...] = tail
    xbc, dt = _mamba_prep_math(taps, zm[:, M_CONV_CH + WIDTH:], cw_ref[...], cb_ref[...], dtb_ref[...])
    y = _ssd_chunk_math(xbc, dt, alog_ref[...], acol_ref[...], dsk_ref[...], expand_ref[...], h_scr)
    o_ref[...] = _ssd_post(y, z, nw_ref[...])

    @pl.when(j == pl.num_programs(1) - 1)
    def _():
        for p in range(n_pairs):
            hs = h_scr[p]
            ht_ref[2 * p] = hs[:M_HEAD]
            ht_ref[2 * p + 1] = hs[M_HEAD:]


def _mamba_seq(x, conv_state, h0, weights):
    bsz, t, d = x.shape
    q = min(SSD_CHUNK, t)
    tile = lambda c: pl.BlockSpec((None, q, c), lambda b, j: (b, j, 0))
    edge = pl.BlockSpec((None, SUBLANES, M_CONV_CH), lambda b, j: (b, 0, 0))
    st = pl.BlockSpec((None, M_HEADS, M_HEAD, M_STATE), lambda b, j: (b, 0, 0, 0))
    return pl.pallas_call(
        _mamba_seq_kernel,
        grid=(bsz, t // q),
        in_specs=[tile(d), edge, st] + [_const(w.shape) for w in weights],
        out_specs=[tile(WIDTH), edge, st],
        out_shape=[jax.ShapeDtypeStruct((bsz, t, WIDTH), F32),
                   jax.ShapeDtypeStruct((bsz, SUBLANES, M_CONV_CH), F32),
                   jax.ShapeDtypeStruct((bsz, M_HEADS, M_HEAD, M_STATE), F32)],
        scratch_shapes=[pltpu.VMEM((SUBLANES, M_CONV_CH), F32),
                        pltpu.VMEM((M_HEADS // 2, 2 * M_HEAD, M_STATE), F32)],
        compiler_params=_cp(("arbitrary", "arbitrary")),
        name="mamba_seq",
    )(x, conv_state, h0, *weights)


def _mamba_step_kernel(zm_ref, past_ref, h0_ref, cw_ref, cb_ref, dtb_ref, alog_ref, dsk_ref, nw_ref, expand_ref,
                       o_ref, ht_ref, y_scr):
    bt = zm_ref.shape[0]
    zm = zm_ref[...]
    raw = zm[:, :M_CONV_CH]
    z = zm[:, M_CONV_CH:M_CONV_CH + WIDTH]
    past = past_ref[...]
    taps = [raw] + [past[:, (M_CONV - 1 - kk) * M_CONV_CH:(M_CONV - kk) * M_CONV_CH] for kk in range(1, M_CONV)]
    xbc, dt = _mamba_prep_math(taps, zm[:, M_CONV_CH + WIDTH:], cw_ref[...], cb_ref[...], dtb_ref[...])
    xs = xbc[:, :WIDTH]
    dt_full = _dot_exact_rhs(dt, expand_ref[...], terms=3)
    keep = jnp.exp(dt_full * -jnp.exp(alog_ref[...]))
    xd = xs * dt_full
    heads_per_group = M_HEADS // M_GROUPS
    zero_x = jnp.zeros((1, WIDTH), F32)
    zero_g = jnp.zeros((1, M_GROUPS * M_STATE), F32)
    new = {}
    for i in range(bt):
        one = slice(i, i + 1)
        xrow = _rows(xd[one], zero_x)
        brow = _rows(xbc[one, WIDTH:WIDTH + M_GROUPS * M_STATE], zero_g)
        for h in range(M_HEADS):
            gi = h // heads_per_group
            cs_ = slice(h * M_HEAD, (h + 1) * M_HEAD)
            kp = keep[one, cs_]
            hn = (h0_ref[i, h] * jnp.concatenate([kp, kp], axis=1)
                  + _dot_tn(xrow[:, cs_], brow[:, gi * M_STATE:(gi + 1) * M_STATE]))
            ht_ref[i, h] = hn
            new[i, h] = hn
    for i in range(bt):
        one = slice(i, i + 1)
        crow = _rows(xbc[one, WIDTH + M_GROUPS * M_STATE:], zero_g)
        outs = [_dot_nt(crow[:, (h // heads_per_group) * M_STATE:(h // heads_per_group + 1) * M_STATE], new[i, h])
                for h in range(M_HEADS)]
        y_scr[one, :] = jnp.concatenate(outs, axis=1)[0:1]
    y = y_scr[...] + dsk_ref[...] * xs
    o_ref[...] = _ssd_post(y, z, nw_ref[...])


def _mamba_step(zm, past, h_all, layer, weights):
    n = zm.shape[0]
    bt = SUBLANES
    tile = lambda c: pl.BlockSpec((bt, c), lambda i: (i, 0))
    st = pl.BlockSpec((None, bt, M_HEADS, M_HEAD, M_STATE), lambda i: (layer, i, 0, 0, 0))
    return pl.pallas_call(
        _mamba_step_kernel,
        grid=(n // bt,),
        in_specs=[tile(M_COLS_PAD), tile((M_CONV - 1) * M_CONV_CH), st] + [_const(w.shape) for w in weights],
        out_specs=[tile(WIDTH), st],
        out_shape=[jax.ShapeDtypeStruct((n, WIDTH), F32), jax.ShapeDtypeStruct(h_all.shape, F32)],
        scratch_shapes=[pltpu.VMEM((bt, WIDTH), F32)],
        input_output_aliases={2: 1},
        compiler_params=_cp(("arbitrary",)),
        name="mamba_step",
    )(zm, past, h_all, *weights)


def _merge_kernel(x_ref, or_ref, os_ref, om_ref, gn_ref, wg_ref, wb_ref, wo_ref, gp_ref, o_ref):
    x = x_ref[...]
    hb = _bf(_rms(x, gn_ref[...]))
    mixed = None
    for kk, ref in enumerate((or_ref, os_ref, om_ref)):
        gate = jax.nn.sigmoid(jnp.dot(hb, wg_ref[:, kk * D_MODEL:(kk + 1) * D_MODEL], preferred_element_type=F32))
        term = gate * jnp.dot(_bf(ref[...]), wb_ref[kk], preferred_element_type=F32)
        mixed = term if mixed is None else mixed + term
    out = jnp.dot(_bf(mixed), wo_ref[...], preferred_element_type=F32)
    o_ref[...] = x + _rms(out, gp_ref[...])


def _merge(x, o_r, o_s, o_m, weights):
    n = x.shape[0]
    tm = min(TOKEN_TILE, n)
    tile = lambda c: pl.BlockSpec((tm, c), lambda i: (i, 0))
    return pl.pallas_call(
        _merge_kernel,
        grid=(n // tm,),
        in_specs=[tile(D_MODEL)] + [tile(WIDTH)] * 3 + [_const(w.shape) for w in weights],
        out_specs=tile(D_MODEL),
        out_shape=jax.ShapeDtypeStruct((n, D_MODEL), F32),
        compiler_params=_cp(("arbitrary",)),
        name="merge",
    )(x, o_r, o_s, o_m, *weights)


def _ffn_chunk(hb, gate_taps_of, cols, wu_ref, cw_ref, cb_ref, wd_ref):
    gate = jnp.dot(hb, wu_ref[:, cols], preferred_element_type=F32)
    val = jnp.dot(hb, wu_ref[:, D_FF + cols.start:D_FF + cols.stop], preferred_element_type=F32)
    taps = gate_taps_of(gate)
    acc = taps[FFN_CONV - 1] * cw_ref[0:1, cols]
    for kk in range(1, FFN_CONV):
        acc = acc + taps[FFN_CONV - 1 - kk] * cw_ref[kk:kk + 1, cols]
    acc = acc + cb_ref[:, cols]
    return jnp.dot(_bf(jax.nn.gelu(acc) * val), wd_ref[cols, :], preferred_element_type=F32), gate


def _ffn_seq_kernel(x_ref, init_ref, gn_ref, wu_ref, cw_ref, cb_ref, wd_ref, gp_ref, o_ref, last_ref, carry):
    @pl.when(pl.program_id(1) == 0)
    def _():
        carry[...] = init_ref[...]

    x = x_ref[...]
    hb = _bf(_rms(x, gn_ref[...]))
    f = None
    for c0, c1 in zip(FFN_COL_STARTS[:-1], FFN_COL_STARTS[1:]):
        cols = slice(c0, c1)
        c8 = carry[:, cols]
        part, gate = _ffn_chunk(hb, lambda g: [g] + [_delayed(g, c8, kk) for kk in range(1, FFN_CONV)], cols,
                                wu_ref, cw_ref, cb_ref, wd_ref)
        tail = gate[gate.shape[0] - SUBLANES:, :]
        carry[:, cols] = tail
        last_ref[:, cols] = tail
        f = part if f is None else f + part
    o_ref[...] = x + _rms(f, gp_ref[...])


def _ffn_step_kernel(x_ref, past_ref, gn_ref, wu_ref, cw_ref, cb_ref, wd_ref, gp_ref, o_ref, gate_ref):
    x = x_ref[...]
    hb = _bf(_rms(x, gn_ref[...]))
    f = None
    for c0, c1 in zip(FFN_COL_STARTS[:-1], FFN_COL_STARTS[1:]):
        cols = slice(c0, c1)
        past = [past_ref[:, (FFN_CONV - 1 - kk) * D_FF + c0:(FFN_CONV - 1 - kk) * D_FF + c1]
                for kk in range(1, FFN_CONV)]
        part, gate = _ffn_chunk(hb, lambda g: [g] + past, cols, wu_ref, cw_ref, cb_ref, wd_ref)
        gate_ref[:, cols] = gate
        f = part if f is None else f + part
    o_ref[...] = x + _rms(f, gp_ref[...])


def _ffn(x, conv_state, weights, seq):
    wspecs = [_const(w.shape) for w in weights]
    if seq:
        bsz, t, d = x.shape
        tm = min(TOKEN_TILE, t)
        tile = pl.BlockSpec((None, tm, d), lambda b, j: (b, j, 0))
        edge = pl.BlockSpec((None, SUBLANES, D_FF), lambda b, j: (b, 0, 0))
        return pl.pallas_call(
            _ffn_seq_kernel,
            grid=(bsz, t // tm),
            in_specs=[tile, edge] + wspecs,
            out_specs=[tile, edge],
            out_shape=[jax.ShapeDtypeStruct((bsz, t, d), F32),
                       jax.ShapeDtypeStruct((bsz, SUBLANES, D_FF), F32)],
            scratch_shapes=[pltpu.VMEM((SUBLANES, D_FF), F32)],
            compiler_params=_cp(("arbitrary", "arbitrary")),
            name="ffn_seq",
        )(x, conv_state, *weights)
    n, d = x.shape
    return pl.pallas_call(
        _ffn_step_kernel,
        grid=(1,),
        in_specs=[_full((n, d)), _full(conv_state.shape)] + wspecs,
        out_specs=[_full((n, d)), _full((n, D_FF))],
        out_shape=[jax.ShapeDtypeStruct((n, d), F32), jax.ShapeDtypeStruct((n, D_FF), F32)],
        compiler_params=_cp(("arbitrary",)),
        name="ffn_step",
    )(x, conv_state, *weights)


def _row(v):
    return v.reshape(1, -1).astype(F32)


def _block_diag_ones():
    head = jnp.arange(WIDTH) // R_HEAD
    return (head[:, None] == head[None, :]).astype(BF16)


def _layer_weights(lw):
    (g_pre_mix, g_post_mix, g_pre_ffn, g_post_ffn, w_in,
     r_mu, r_w0, r_w2, r_a0, r_a2, r_g2, r_kk, r_ka, r_rk, r_ln_w, r_ln_b,
     s5_lam_re, s5_lam_im, s5_log_step, s5_b_re, s5_b_im, s5_c_re, s5_c_im, s5_d, s5_w_glu, s5_b_glu,
     m_conv_w, m_conv_b, m_dt_bias, m_a_log, m_d, m_norm_w,
     w_branch, w_out, w_up, f_conv_w, f_conv_b, w_down) = lw
    c0 = R_COLS
    c1 = c0 + WIDTH
    c2 = c1 + WIDTH + M_CONV_CH
    c3 = c2 + M_HEADS
    ones_bd = _block_diag_ones()
    gn = _row(g_pre_mix)
    w = {"g_pre_mix": gn}
    w_r = _bf(w_in[:, :c0])
    w_m = _bf(jnp.concatenate(
        [w_in[:, c1 + WIDTH:c2], w_in[:, c1:c1 + WIDTH], w_in[:, c2:c3],
         jnp.zeros((D_MODEL, M_DT_PAD - M_HEADS), F32)], axis=1))
    w["w_r"], w["w_m"] = w_r, w_m
    rwkv_mix = (_row(r_mu), _row(r_w0), _bf(r_w2), _row(r_a0), _bf(r_a2), _bf(r_g2), _row(r_kk), _row(r_ka),
                ones_bd, _row(r_rk), _row(r_ln_w), _row(r_ln_b))
    w["rwkv_seq"] = (gn, w_r) + rwkv_mix
    w["rwkv_step"] = rwkv_mix
    ab_re, ab_im, bb_re, bb_im = _s5_params(s5_lam_re, s5_lam_im, s5_log_step, s5_b_re, s5_b_im)
    gpb = S5_GROUPS // S5_BLOCKS
    eye = jnp.eye(gpb, dtype=F32)
    blocked = lambda m: m.reshape(S5_BLOCKS, gpb, S5_GROUP, S5_STATE)
    to_state = lambda bb: _bf(jnp.einsum('qghp,gk->qghkp', blocked(bb), eye)
                              .reshape(S5_BLOCKS, S5_BLOCK_IN, S5_BLOCK_STATE))
    from_state = lambda c: _bf(jnp.einsum('qghp,gk->qgpkh', blocked(c), eye)
                               .reshape(S5_BLOCKS, S5_BLOCK_STATE, S5_BLOCK_IN))
    w["s5"] = (gn, _bf(w_in[:, c0:c1]), to_state(bb_re), to_state(bb_im), from_state(s5_c_re), from_state(s5_c_im),
               ab_re.reshape(1, S5_LANES), ab_im.reshape(1, S5_LANES), _row(s5_d), _bf(s5_w_glu), _row(s5_b_glu))
    conv = (m_conv_w.T.astype(F32), _row(m_conv_b), _row(jnp.pad(m_dt_bias, (0, M_DT_PAD - M_HEADS))))
    alog_row = _row(jnp.repeat(m_a_log, M_HEAD))
    tail = (_row(jnp.repeat(m_d, M_HEAD)), _row(m_norm_w),
            (jnp.arange(M_DT_PAD)[:, None] == (jnp.arange(WIDTH) // M_HEAD)[None, :]).astype(BF16))
    w["mamba_seq"] = (gn, w_m) + conv + (alog_row, m_a_log.reshape(M_HEADS, 1).astype(F32)) + tail
    w["mamba_step"] = conv + (alog_row,) + tail
    w["merge"] = (gn, _bf(w_in[:, c3:]), _bf(w_branch), _bf(w_out), _row(g_post_mix))
    w["ffn"] = (_row(g_pre_ffn), _bf(w_up), f_conv_w.T.astype(F32), _row(f_conv_b), _bf(w_down), _row(g_post_ffn))
    return w


def _pad_rows(state):
    return jnp.pad(state, ((0, 0), (SUBLANES - state.shape[1], 0), (0, 0)))


def _layer(x, states, w, bsz, t, layer=0):
    shift0, wkv0, s5r0, s5i0, ssd0, mconv0, fconv0 = states
    seq = t > 1
    n = bsz * t
    d = x.shape[-1]
    flat = lambda a: a.reshape(n, a.shape[-1])
    s5r0 = s5r0.reshape(bsz, S5_LANES)
    s5i0 = s5i0.reshape(bsz, S5_LANES)
    if seq:
        x3 = x.reshape(bsz, t, d)
        o_r, last, wkv1 = _rwkv_seq(x3, jnp.broadcast_to(shift0[:, None], (bsz, SUBLANES, R_COLS)), wkv0,
                                    w["rwkv_seq"])
        shift1 = last[:, SUBLANES - 1]
        o_s, s5r1, s5i1 = _s5(x3, s5r0, s5i0, w["s5"], True)
        o_m, last, ssd1 = _mamba_seq(x3, _pad_rows(mconv0), ssd0, w["mamba_seq"])
        mconv1 = last[:, SUBLANES - (M_CONV - 1):]
        o_r, o_s, o_m = flat(o_r), flat(o_s), flat(o_m)
    else:
        z_r = _norm_matmul(x, w["g_pre_mix"], w["w_r"], "in_proj_rwkv")
        z_m = _norm_matmul(x, w["g_pre_mix"], w["w_m"], "in_proj_mamba")
        shift1 = z_r
        o_r, wkv1 = _rwkv_step(z_r, shift0, wkv0, layer, w["rwkv_step"])
        o_s, s5r1, s5i1 = _s5(x, s5r0, s5i0, w["s5"], False)
        o_m, ssd1 = _mamba_step(z_m, mconv0.reshape(bsz, (M_CONV - 1) * M_CONV_CH), ssd0, layer, w["mamba_step"])
        mconv1 = jnp.concatenate([mconv0[:, 1:], z_m[:, None, :M_CONV_CH]], axis=1)
    s5r1 = s5r1.reshape(bsz, S5_GROUPS, S5_STATE)
    s5i1 = s5i1.reshape(bsz, S5_GROUPS, S5_STATE)

    x = _merge(x, o_r, o_s, o_m, w["merge"])

    if seq:
        x3, last = _ffn(x.reshape(bsz, t, d), _pad_rows(fconv0), w["ffn"], True)
        x = flat(x3)
        fconv1 = last[:, SUBLANES - (FFN_CONV - 1):]
    else:
        x, gate = _ffn(x, fconv0.reshape(bsz, (FFN_CONV - 1) * D_FF), w["ffn"], False)
        fconv1 = jnp.concatenate([fconv0[:, 1:], gate[:, None]], axis=1)
    return x, (shift1, wkv1, s5r1, s5i1, ssd1, mconv1, fconv1)


def _zero_states(n):
    return (jnp.zeros((n, R_COLS), F32),
            jnp.zeros((n, R_HEADS, R_HEAD, R_HEAD), F32),
            jnp.zeros((n, S5_GROUPS, S5_STATE), F32),
            jnp.zeros((n, S5_GROUPS, S5_STATE), F32),
            jnp.zeros((n, M_HEADS, M_HEAD, M_STATE), F32),
            jnp.zeros((n, M_CONV - 1, M_CONV_CH), F32),
            jnp.zeros((n, FFN_CONV - 1, D_FF), F32))


def kernel(x_prompt, x_sample, state_rwkv_shift, state_rwkv_wkv, state_s5_re, state_s5_im, state_ssd, state_ssd_conv, state_ffn_conv, g_pre_mix, g_post_mix, g_pre_ffn, g_post_ffn, w_in, r_mu, r_w0, r_w2, r_a0, r_a2, r_g2, r_kk, r_ka, r_rk, r_ln_w, r_ln_b, s5_lam_re, s5_lam_im, s5_log_step, s5_b_re, s5_b_im, s5_c_re, s5_c_im, s5_d, s5_w_glu, s5_b_glu, m_conv_w, m_conv_b, m_dt_bias, m_a_log, m_d, m_norm_w, w_branch, w_out, w_up, f_conv_w, f_conv_b, w_down):
    stacked = (g_pre_mix, g_post_mix, g_pre_ffn, g_post_ffn, w_in,
               r_mu, r_w0, r_w2, r_a0, r_a2, r_g2, r_kk, r_ka, r_rk, r_ln_w, r_ln_b,
               s5_lam_re, s5_lam_im, s5_log_step, s5_b_re, s5_b_im, s5_c_re, s5_c_im, s5_d,
               s5_w_glu, s5_b_glu,
               m_conv_w, m_conv_b, m_dt_bias, m_a_log, m_d, m_norm_w,
               w_branch, w_out, w_up, f_conv_w, f_conv_b, w_down)
    cache_in = (state_rwkv_shift, state_rwkv_wkv, state_s5_re, state_s5_im,
                state_ssd, state_ssd_conv, state_ffn_conv)
    depth = w_in.shape[0]
    pb, pt, d = x_prompt.shape
    sb, s_t, _ = x_sample.shape
    xp = x_prompt.reshape(pb * pt, d)
    xs = x_sample.reshape(sb * s_t, d)
    new_p = [[] for _ in cache_in]
    new_s = [[] for _ in cache_in]
    in_place = (1, 4)
    wkv_all, ssd_all = state_rwkv_wkv, state_ssd
    for l in range(depth):
        w = _layer_weights(tuple(a[l] for a in stacked))
        xp, sp = _layer(xp, _zero_states(pb), w, pb, pt)
        layer_states = [c[l] for c in cache_in]
        layer_states[1], layer_states[4] = wkv_all, ssd_all
        xs, ss = _layer(xs, tuple(layer_states), w, sb, s_t, layer=l)
        wkv_all, ssd_all = ss[1], ss[4]
        for i in range(len(cache_in)):
            new_p[i].append(sp[i])
            if i not in in_place:
                new_s[i].append(ss[i])
    outs = [xp.reshape(pb, pt, d), xs.reshape(sb, s_t, d)]
    for i, (p_list, s_list) in enumerate(zip(new_p, new_s)):
        outs.append(jnp.stack(p_list, 0))
        outs.append({1: wkv_all, 4: ssd_all}[i] if i in in_place else jnp.stack(s_list, 0))
    return tuple(outs)
```

```python
import functools

import jax
import jax.numpy as jnp
from jax import lax
from jax.experimental import pallas as pl
from jax.experimental.pallas import tpu as pltpu

F32 = jnp.float32
BF16 = jnp.bfloat16

D_MODEL = 1024
WIDTH = 512
R_HEADS, R_HEAD = 8, 64
R_COLS = 1792
R_LN_EPS = 64e-5
S5_GROUPS, S5_GROUP, S5_STATE = 32, 16, 64
S5_LANES = S5_GROUPS * S5_STATE
M_HEADS, M_HEAD, M_GROUPS, M_STATE = 8, 64, 2, 128
M_CONV, M_CONV_CH = 4, 1024
M_DT_PAD = 256
M_COLS_PAD = M_CONV_CH + WIDTH + M_DT_PAD
M_EPS = 1e-5
D_FF = 2816
FFN_CONV = 3
EPS = 1e-6
SUBLANES = 8
LANES = 128

RWKV_CHUNK = 64
RWKV_TILE = 256
SSD_CHUNK = 128
S5_CHUNK = 128
S5_ROW_PAD = 8
S5_SCAN_TILES = 8
SSD_TILE = 256
BRANCH_DTYPE = BF16
TOKEN_TILE = 512
FFN_COL_STARTS = (0, D_FF // 2, D_FF)
VMEM_LIMIT = 56 * 1024 * 1024


def _cp(sem):
    return pltpu.CompilerParams(dimension_semantics=sem, vmem_limit_bytes=VMEM_LIMIT)


def _const(shape):
    nd = len(shape)
    return pl.BlockSpec(shape, lambda *_: (0,) * nd, pipeline_mode=pl.Buffered(1))


def _full(shape):
    nd = len(shape)
    return pl.BlockSpec(shape, lambda *_: (0,) * nd)


def _bf(x):
    return x.astype(BF16)


def _dot(a, b):
    return jnp.dot(_bf(a), _bf(b), preferred_element_type=F32)


def _dot_nt(a, b):
    return lax.dot_general(_bf(a), _bf(b), (((1,), (1,)), ((), ())), preferred_element_type=F32)


def _dot_tn(a, b):
    return lax.dot_general(_bf(a), _bf(b), (((0,), (0,)), ((), ())), preferred_element_type=F32)


def _split(x, terms):
    out = []
    for _ in range(terms - 1):
        h = _bf(x)
        out.append(h)
        x = x - h.astype(F32)
    out.append(_bf(x))
    return out


def _dot_exact_lhs(m_bf16, x, terms=3):
    acc = None
    for h in _split(x, terms):
        p = jnp.dot(m_bf16, h, preferred_element_type=F32)
        acc = p if acc is None else acc + p
    return acc


def _dot_exact_rhs(x, m_bf16, terms=2):
    acc = None
    for h in _split(x, terms):
        p = jnp.dot(h, m_bf16, preferred_element_type=F32)
        acc = p if acc is None else acc + p
    return acc


def _softplus(x):
    return jnp.maximum(x, 0.0) + jnp.log1p(jnp.exp(-jnp.abs(x)))


def _rms(x, g, eps=EPS):
    return x * lax.rsqrt(jnp.mean(x * x, axis=-1, keepdims=True) + eps) * g


def _delayed(x, carry8, k):
    rx = pltpu.roll(x, k, 0)
    rc = pltpu.roll(carry8, k, 0)
    row = lax.broadcasted_iota(jnp.int32, (SUBLANES, x.shape[1]), 0)
    head = jnp.where(row < k, rc, rx[:SUBLANES])
    if x.shape[0] == SUBLANES:
        return head
    return jnp.concatenate([head, rx[SUBLANES:]], axis=0)


def _row_group(t, rows):
    start = t * rows
    return pl.multiple_of(start, SUBLANES) if rows % SUBLANES == 0 else start


def _rows(first, second):
    row = lax.broadcasted_iota(jnp.int32, (SUBLANES, first.shape[1]), 0)
    return jnp.where(row == 0, first, jnp.where(row == 1, second, 0.0))


def _norm_matmul_kernel(x_ref, g_ref, w_ref, o_ref):
    o_ref[...] = jnp.dot(_bf(_rms(x_ref[...], g_ref[...])), w_ref[...], preferred_element_type=F32)


def _norm_matmul(x, g, w, name):
    n, d = x.shape
    c = w.shape[1]
    return pl.pallas_call(
        _norm_matmul_kernel,
        grid=(1,),
        in_specs=[_full((n, d)), _full((1, d)), _full((d, c))],
        out_specs=_full((n, c)),
        out_shape=jax.ShapeDtypeStruct((n, c), F32),
        compiler_params=_cp(("arbitrary",)),
        name=name,
    )(x, g, w)


def _rwkv_prep_math(z, prev, mu, w0, w2, a0, a2, g2, kkw, kaw, ones_bd):
    zm = z + (prev - z) * mu
    r = zm[:, 0:WIDTH]
    k = zm[:, WIDTH:2 * WIDTH]
    v = zm[:, 2 * WIDTH:3 * WIDTH]
    dw = zm[:, 1536:1600]
    da = zm[:, 1600:1664]
    dg = zm[:, 1664:1792]
    logw = -_softplus(-(w0 + _dot(jnp.tanh(dw), w2))) - 0.5
    ld = -jnp.exp(logw)
    a = jax.nn.sigmoid(a0 + _dot(da, a2))
    g = _dot(jax.nn.sigmoid(dg), g2)
    kk = k * kkw
    ss = _dot_exact_rhs(kk * kk, ones_bd)
    kk = kk / jnp.maximum(jnp.sqrt(ss), 1e-12)
    k2 = k * (1.0 + (a - 1.0) * kaw)
    return r, ld, k2, v, kk, kk * a, g


def _rwkv_post(y, r, k, v, g, rk, lnw, lnb, ones_bd):
    inv = 1.0 / R_HEAD
    mean = _dot_exact_rhs(y, ones_bd) * inv
    d = y - mean
    var = _dot_exact_rhs(d * d, ones_bd) * inv
    yn = d * lax.rsqrt(var + R_LN_EPS) * lnw + lnb
    bonus = _dot_exact_rhs(r * k * rk, ones_bd) * v
    return (yn + bonus) * g


def _rwkv_chunks(r, ld, k, v, kk, b, s_scr):
    L = RWKV_CHUNK
    pair = 2 * R_HEAD
    n_pairs = R_HEADS // 2
    tm = r.shape[0]
    n_chunks = tm // L
    trow = lax.broadcasted_iota(jnp.int32, (tm, tm), 0)
    tcol = lax.broadcasted_iota(jnp.int32, (tm, tm), 1)
    same_chunk = (trow // L) == (tcol // L)
    cum = _dot_exact_lhs(_bf(((trow >= tcol) & same_chunk).astype(F32)), ld)
    wc = jnp.exp(cum)
    winv = jnp.exp(-cum)
    r_t = r * wc
    kk_t = kk * jnp.exp(cum - ld)
    k_h = k * winv
    b_h = b * winv

    row = lax.broadcasted_iota(jnp.int32, (pair, pair), 0)
    col = lax.broadcasted_iota(jnp.int32, (pair, pair), 1)
    same_head = (row // R_HEAD) == (col // R_HEAD)
    lrow = lax.broadcasted_iota(jnp.int32, (L, pair), 0)
    lcol = lax.broadcasted_iota(jnp.int32, (L, pair), 1) % R_HEAD
    strict = lrow > lcol
    lower = lrow >= lcol

    def bd(x):
        return jnp.where(same_head, jnp.concatenate([x, x], axis=0), 0.0)

    units = [(c, p) for c in range(n_chunks) for p in range(n_pairs)]
    pre = {}
    for c, p in units:
        rows = slice(c * L, (c + 1) * L)
        lanes = slice(p * pair, (p + 1) * pair)
        wl = wc[(c + 1) * L - 1:(c + 1) * L, lanes]
        pre[c, p] = dict(kkt=kk_t[rows, lanes], rt=r_t[rows, lanes], kh=k_h[rows, lanes], bh=b_h[rows, lanes],
                         v=v[rows, lanes], vbd=bd(v[rows, lanes]), wl=wl)
    for u in units:
        d = pre[u]
        a = _dot_nt(jnp.concatenate([d["kkt"], d["rt"]], axis=0),
                    jnp.concatenate([bd(d["kh"]), bd(d["bh"])], axis=0))
        d["akk_k"] = jnp.where(strict, a[:L, :pair], 0.0)
        d["n"] = jnp.where(strict, a[:L, pair:], 0.0)
        d["ar_k"] = jnp.where(lower, a[L:, :pair], 0.0)
        d["ar_b"] = jnp.where(lower, a[L:, pair:], 0.0)
        d["q"] = -d["n"]
        d["m"] = d["n"]
    power = 2
    while power < L:
        for u in units:
            d = pre[u]
            d["m"] = _dot(d["m"], bd(d["m"]))
            d["q"] = d["q"] + d["m"] + _dot(d["q"], bd(d["m"]))
        power *= 2
    for u in units:
        d = pre[u]
        xy = _dot(jnp.concatenate([d["akk_k"], d["ar_k"]], axis=0), d["vbd"])
        x, d["y0"] = xy[:L], xy[L:]
        both = jnp.concatenate([d["kkt"], x], axis=1)
        both = both + _dot(d["q"], jnp.concatenate([bd(d["kkt"]), bd(x)], axis=1))
        d["g"], d["u0"] = both[:, :pair], both[:, pair:]
    for u in units:
        d = pre[u]
        t = _dot(d["ar_b"], jnp.concatenate([bd(d["g"]), bd(d["u0"])], axis=1))
        d["ry"] = d["rt"] - t[:, :pair]
        d["y0"] = d["y0"] - t[:, pair:]
        kw = d["kh"] * d["wl"]
        bw = d["bh"] * d["wl"]
        d["pm"] = jnp.where(same_head, _dot_tn(d["g"], bw), 0.0)
        d["c"] = jnp.where(same_head, _dot_tn(jnp.concatenate([d["v"], -d["u0"]], axis=0),
                                               jnp.concatenate([kw, bw], axis=0)), 0.0)
    ys = []
    for c in range(n_chunks):
        parts = []
        for p in range(n_pairs):
            d = pre[c, p]
            s = s_scr[p]
            parts.append(_dot_nt(d["ry"], s) + d["y0"])
            s_scr[p] = s * d["wl"] - _dot(s, d["pm"]) + d["c"]
        ys.append(jnp.concatenate(parts, axis=1))
    return jnp.concatenate(ys, axis=0)


def _rwkv_seq_kernel(x_ref, init_ref, s0_ref, gn_ref, wr_ref, mu_ref, w0_ref, w2_ref, a0_ref, a2_ref, g2_ref,
                     kkw_ref, kaw_ref, ones_ref, rk_ref, lnw_ref, lnb_ref, o_ref, last_ref, st_ref, carry, s_scr):
    j = pl.program_id(1)
    n_pairs = R_HEADS // 2

    @pl.when(j == 0)
    def _():
        carry[...] = init_ref[...]
        zero = jnp.zeros((R_HEAD, R_HEAD), F32)
        for p in range(n_pairs):
            top = jnp.concatenate([s0_ref[2 * p], zero], axis=1)
            bot = jnp.concatenate([zero, s0_ref[2 * p + 1]], axis=1)
            s_scr[p] = jnp.concatenate([top, bot], axis=0)

    z = jnp.dot(_bf(_rms(x_ref[...], gn_ref[...])), wr_ref[...], preferred_element_type=F32)
    prev = _delayed(z, carry[...], 1)
    tail = z[z.shape[0] - SUBLANES:, :]
    carry[...] = tail
    last_ref[...] = tail
    ones_bd = ones_ref[...]
    r, ld, k, v, kk, b, g = _rwkv_prep_math(z, prev, mu_ref[...], w0_ref[...], w2_ref[...], a0_ref[...],
                                            a2_ref[...], g2_ref[...], kkw_ref[...], kaw_ref[...], ones_bd)
    y = _rwkv_chunks(r, ld, k, v, kk, b, s_scr)
    o_ref[...] = _rwkv_post(y, r, k, v, g, rk_ref[...], lnw_ref[...], lnb_ref[...], ones_bd).astype(o_ref.dtype)

    @pl.when(j == pl.num_programs(1) - 1)
    def _():
        for p in range(n_pairs):
            s = s_scr[p]
            st_ref[2 * p] = s[:R_HEAD, :R_HEAD]
            st_ref[2 * p + 1] = s[R_HEAD:, R_HEAD:]


def _rwkv_seq(x, shift, s0, weights):
    bsz, t, d = x.shape
    tm = min(RWKV_TILE, t)
    tile = lambda c: pl.BlockSpec((None, tm, c), lambda b, j: (b, j, 0))
    edge = pl.BlockSpec((None, SUBLANES, R_COLS), lambda b, j: (b, 0, 0))
    st = pl.BlockSpec((None, R_HEADS, R_HEAD, R_HEAD), lambda b, j: (b, 0, 0, 0))
    return pl.pallas_call(
        _rwkv_seq_kernel,
        grid=(bsz, t // tm),
        in_specs=[tile(d), edge, st] + [_const(w.shape) for w in weights],
        out_specs=[tile(WIDTH), edge, st],
        out_shape=[jax.ShapeDtypeStruct((bsz, t, WIDTH), BRANCH_DTYPE),
                   jax.ShapeDtypeStruct((bsz, SUBLANES, R_COLS), F32),
                   jax.ShapeDtypeStruct((bsz, R_HEADS, R_HEAD, R_HEAD), F32)],
        scratch_shapes=[pltpu.VMEM((SUBLANES, R_COLS), F32),
                        pltpu.VMEM((R_HEADS // 2, 2 * R_HEAD, 2 * R_HEAD), F32)],
        compiler_params=_cp(("arbitrary", "arbitrary")),
        name="rwkv_seq",
    )(x, shift, s0, *weights)


def _rwkv_step_kernel(z_ref, prev_ref, s0_ref, all_ref, mu_ref, w0_ref, w2_ref, a0_ref, a2_ref, g2_ref, kkw_ref,
                      kaw_ref, ones_ref, rk_ref, lnw_ref, lnb_ref, o_ref, st_ref, y_scr):
    del all_ref
    bt = z_ref.shape[0]
    ones_bd = ones_ref[...]
    r, ld, k, v, kk, b, g = _rwkv_prep_math(z_ref[...], prev_ref[...], mu_ref[...], w0_ref[...], w2_ref[...],
                                            a0_ref[...], a2_ref[...], g2_ref[...], kkw_ref[...], kaw_ref[...],
                                            ones_bd)
    w = jnp.exp(ld)
    wr = w * r
    b_dot_r = _dot_exact_rhs(b * r, ones_bd)
    k_dot_r = _dot_exact_rhs(k * r, ones_bd)

    projs = []
    for i in range(bt):
        one = slice(i, i + 1)
        lhs = _rows(kk[one], wr[one])
        projs.append(jnp.concatenate(
            [_dot_nt(lhs[:, h * R_HEAD:(h + 1) * R_HEAD], s0_ref[i, h]) for h in range(R_HEADS)], axis=1))
    pieces = []
    for i in range(bt):
        one = slice(i, i + 1)
        sa = projs[i][0:1]
        y_scr[one, :] = projs[i][1:2] - sa * b_dot_r[one] + v[one] * k_dot_r[one]
        pieces.append(_split(_rows(v[one], -sa), 2) + _split(_rows(k[one], b[one]), 2))
    for i in range(bt):
        one = slice(i, i + 1)
        l_hi, l_lo, r_hi, r_lo = pieces[i]
        for h in range(R_HEADS):
            cs = slice(h * R_HEAD, (h + 1) * R_HEAD)
            tn = lambda x, y: lax.dot_general(x[:, cs], y[:, cs], (((0,), (0,)), ((), ())),
                                              preferred_element_type=F32)
            upd = tn(l_hi, r_hi) + tn(l_hi, r_lo) + tn(l_lo, r_hi)
            st_ref[i, h] = s0_ref[i, h] * w[one, cs] + upd
    o_ref[...] = _rwkv_post(y_scr[...], r, k, v, g, rk_ref[...], lnw_ref[...], lnb_ref[...], ones_bd)


def _rwkv_step(z, shift, s_all, s_new, layer, weights):
    n = z.shape[0]
    bt = SUBLANES
    tile = lambda c: pl.BlockSpec((bt, c), lambda i: (i, 0))
    st = pl.BlockSpec((None, bt, R_HEADS, R_HEAD, R_HEAD), lambda i: (layer, i, 0, 0, 0))
    return pl.pallas_call(
        _rwkv_step_kernel,
        grid=(n // bt,),
        in_specs=[tile(R_COLS), tile(R_COLS), st, pl.BlockSpec(memory_space=pl.ANY)]
        + [_const(w.shape) for w in weights],
        out_specs=[tile(WIDTH), st],
        out_shape=[jax.ShapeDtypeStruct((n, WIDTH), F32), jax.ShapeDtypeStruct(s_new.shape, F32)],
        scratch_shapes=[pltpu.VMEM((bt, WIDTH), F32)],
        input_output_aliases={3: 1},
        compiler_params=_cp(("arbitrary",)),
        name="rwkv_step",
    )(z, shift, s_all, s_new, *weights)


S5_BLOCKS = 4
S5_BLOCK_IN = WIDTH // S5_BLOCKS
S5_BLOCK_STATE = S5_LANES // S5_BLOCKS
S5_BLOCK_TILES = S5_BLOCK_STATE // LANES


def _s5_param_kernel(lr_ref, li_ref, ls_ref, br_ref, bi_ref, abr_ref, abi_ref, bbr_ref, bbi_ref):
    lr, li = lr_ref[...], li_ref[...]
    delta = jnp.exp(ls_ref[...])
    mag = jnp.exp(lr * delta)
    ab_re = mag * jnp.cos(li * delta)
    ab_im = mag * jnp.sin(li * delta)
    den = lr * lr + li * li
    nr, ni = ab_re - 1.0, ab_im
    cf_re = (nr * lr + ni * li) / den
    cf_im = (ni * lr - nr * li) / den
    abr_ref[...] = ab_re
    abi_ref[...] = ab_im
    br, bi = br_ref[...], bi_ref[...]
    bbr_ref[...] = cf_re[:, None, :] * br - cf_im[:, None, :] * bi
    bbi_ref[...] = cf_re[:, None, :] * bi + cf_im[:, None, :] * br


def _s5_params(lam_re, lam_im, log_step, b_re, b_im):
    g, p, h = b_re.shape
    gp = jax.ShapeDtypeStruct((g, p), F32)
    ghp = jax.ShapeDtypeStruct((g, h, p), F32)
    args = (lam_re, lam_im, log_step.reshape(g, 1), jnp.swapaxes(b_re, 1, 2), jnp.swapaxes(b_im, 1, 2))
    return pl.pallas_call(
        _s5_param_kernel,
        grid=(1,),
        in_specs=[_full(a.shape) for a in args],
        out_specs=[_full((g, p)), _full((g, p)), _full((g, h, p)), _full((g, h, p))],
        out_shape=[gp, gp, ghp, ghp],
        compiler_params=_cp(("arbitrary",)),
        name="s5_params",
    )(*args)


def _s5_drive(ub, wbr_ref, wbi_ref, blk):
    cols = slice(blk * S5_BLOCK_IN, (blk + 1) * S5_BLOCK_IN)
    return (jnp.dot(ub[:, cols], wbr_ref[blk], preferred_element_type=F32),
            jnp.dot(ub[:, cols], wbi_ref[blk], preferred_element_type=F32))


def _s5_readout(u, xr_blocks, xi_blocks, wcr_ref, wci_ref, d, wg, bg):
    y = jnp.concatenate([_dot(xr_blocks[q], wcr_ref[q]) - _dot(xi_blocks[q], wci_ref[q])
                         for q in range(S5_BLOCKS)], axis=1) + d * u
    y = jax.nn.gelu(y)
    return y * jax.nn.sigmoid(_dot(y, wg) + bg)


def _s5_seq_kernel(x_ref, s0r_ref, s0i_ref, gn_ref, ws_ref, wbr_ref, wbi_ref, wcr_ref, wci_ref, ar_ref, ai_ref,
                   d_ref, wg_ref, bg_ref, o_ref, str_ref, sti_ref, xr_scr, xi_scr, cr_scr, ci_scr, seq_scr, tm_scr,
                   *, bt, tc):
    @pl.when(pl.program_id(0) == 0)
    def _():
        cr_scr[...] = s0r_ref[...]
        ci_scr[...] = s0i_ref[...]

    pitch = tc + S5_ROW_PAD
    in_tiles = WIDTH // LANES
    n_tiles = S5_LANES // LANES
    x = x_ref[...].reshape(bt * tc, D_MODEL)
    u_seq = jnp.dot(_bf(_rms(x, gn_ref[...])), ws_ref[...], preferred_element_type=F32)
    for c in range(in_tiles):
        for s in range(bt):
            seq_scr[c, s * pitch:s * pitch + tc, :] = u_seq[s * tc:(s + 1) * tc, c * LANES:(c + 1) * LANES]

    def to_time_major(t, carry):
        dst = pl.ds(_row_group(t, bt), bt)
        for c in range(in_tiles):
            tm_scr[c, dst, :] = seq_scr[c, pl.ds(t, bt, stride=pitch), :]
        return carry

    lax.fori_loop(0, tc, to_time_major, 0, unroll=2)
    u = jnp.concatenate([tm_scr[c] for c in range(in_tiles)], axis=1)
    ub = _bf(u)
    for q in range(S5_BLOCKS):
        bu_re, bu_im = _s5_drive(ub, wbr_ref, wbi_ref, q)
        for c in range(S5_BLOCK_TILES):
            xr_scr[q * S5_BLOCK_TILES + c] = bu_re[:, c * LANES:(c + 1) * LANES]
            xi_scr[q * S5_BLOCK_TILES + c] = bu_im[:, c * LANES:(c + 1) * LANES]

    for g0 in range(0, n_tiles, S5_SCAN_TILES):
        tiles = range(g0, g0 + S5_SCAN_TILES)
        ar = [jnp.broadcast_to(ar_ref[:, c * LANES:(c + 1) * LANES], (bt, LANES)) for c in tiles]
        ai = [jnp.broadcast_to(ai_ref[:, c * LANES:(c + 1) * LANES], (bt, LANES)) for c in tiles]

        def body(t, carry, tiles=tiles, ar=ar, ai=ai):
            rows = pl.ds(_row_group(t, bt), bt)
            out = []
            for n, c in enumerate(tiles):
                xr, xi = carry[2 * n], carry[2 * n + 1]
                nr = ar[n] * xr - ai[n] * xi + xr_scr[c, rows, :]
                ni = ar[n] * xi + ai[n] * xr + xi_scr[c, rows, :]
                xr_scr[c, rows, :] = nr
                xi_scr[c, rows, :] = ni
                out += [nr, ni]
            return tuple(out)

        init = []
        for c in tiles:
            init += [cr_scr[:, c * LANES:(c + 1) * LANES], ci_scr[:, c * LANES:(c + 1) * LANES]]
        fin = lax.fori_loop(0, tc, body, tuple(init), unroll=2)
        for n, c in enumerate(tiles):
            cr_scr[:, c * LANES:(c + 1) * LANES] = fin[2 * n]
            ci_scr[:, c * LANES:(c + 1) * LANES] = fin[2 * n + 1]
    str_ref[...] = cr_scr[...]
    sti_ref[...] = ci_scr[...]

    blocks = lambda scr: [jnp.concatenate([scr[q * S5_BLOCK_TILES + c] for c in range(S5_BLOCK_TILES)], axis=1)
                          for q in range(S5_BLOCKS)]
    y = _s5_readout(u, blocks(xr_scr), blocks(xi_scr), wcr_ref, wci_ref, d_ref[...], wg_ref[...], bg_ref[...])
    for c in range(in_tiles):
        tm_scr[c] = y[:, c * LANES:(c + 1) * LANES]

    def to_sequence_major(t, carry):
        src = pl.ds(_row_group(t, bt), bt)
        for c in range(in_tiles):
            seq_scr[c, pl.ds(t, bt, stride=pitch), :] = tm_scr[c, src, :]
        return carry

    lax.fori_loop(0, tc, to_sequence_major, 0, unroll=2)
    for s in range(bt):
        o_ref[s] = jnp.concatenate([seq_scr[c, s * pitch:s * pitch + tc, :] for c in range(in_tiles)],
                                   axis=1).astype(o_ref.dtype)


def _s5_step_kernel(x_ref, s0r_ref, s0i_ref, gn_ref, ws_ref, wbr_ref, wbi_ref, wcr_ref, wci_ref, ar_ref, ai_ref,
                    d_ref, wg_ref, bg_ref, o_ref, str_ref, sti_ref):
    u = jnp.dot(_bf(_rms(x_ref[...], gn_ref[...])), ws_ref[...], preferred_element_type=F32)
    ub = _bf(u)
    xr_blocks, xi_blocks = [], []
    for q in range(S5_BLOCKS):
        lanes = slice(q * S5_BLOCK_STATE, (q + 1) * S5_BLOCK_STATE)
        ar, ai = ar_ref[:, lanes], ai_ref[:, lanes]
        sr, si = s0r_ref[:, lanes], s0i_ref[:, lanes]
        bu_re, bu_im = _s5_drive(ub, wbr_ref, wbi_ref, q)
        xr = ar * sr - ai * si + bu_re
        xi = ar * si + ai * sr + bu_im
        str_ref[:, lanes] = xr
        sti_ref[:, lanes] = xi
        xr_blocks.append(xr)
        xi_blocks.append(xi)
    o_ref[...] = _s5_readout(u, xr_blocks, xi_blocks, wcr_ref, wci_ref, d_ref[...], wg_ref[...], bg_ref[...])


def _s5(x, s0r, s0i, weights, seq):
    wspecs = [_const(w.shape) for w in weights]
    if seq:
        bsz, t, d = x.shape
        tc = min(S5_CHUNK, t)
        st = _full((bsz, S5_LANES))
        return pl.pallas_call(
            functools.partial(_s5_seq_kernel, bt=bsz, tc=tc),
            grid=(t // tc,),
            in_specs=[pl.BlockSpec((bsz, tc, d), lambda j: (0, j, 0)), st, st] + wspecs,
            out_specs=[pl.BlockSpec((bsz, tc, WIDTH), lambda j: (0, j, 0)), st, st],
            out_shape=[jax.ShapeDtypeStruct((bsz, t, WIDTH), BRANCH_DTYPE)]
            + [jax.ShapeDtypeStruct((bsz, S5_LANES), F32)] * 2,
            scratch_shapes=[pltpu.VMEM((S5_LANES // LANES, bsz * tc, LANES), F32)] * 2
            + [pltpu.VMEM((bsz, S5_LANES), F32)] * 2
            + [pltpu.VMEM((WIDTH // LANES, bsz * (tc + S5_ROW_PAD), LANES), F32),
               pltpu.VMEM((WIDTH // LANES, bsz * tc, LANES), F32)],
            compiler_params=_cp(("arbitrary",)),
            name="s5_seq",
        )(x, s0r, s0i, *weights)
    n, d = x.shape
    st = _full((n, S5_LANES))
    return pl.pallas_call(
        _s5_step_kernel,
        grid=(1,),
        in_specs=[_full((n, d)), st, st] + wspecs,
        out_specs=[_full((n, WIDTH)), st, st],
        out_shape=[jax.ShapeDtypeStruct((n, WIDTH), F32)] + [jax.ShapeDtypeStruct((n, S5_LANES), F32)] * 2,
        compiler_params=_cp(("arbitrary",)),
        name="s5_step",
    )(x, s0r, s0i, *weights)


def _mamba_prep_math(xbc_taps, dt_raw, cw, cb, dtb):
    acc = xbc_taps[M_CONV - 1] * cw[0:1]
    for kk in range(1, M_CONV):
        acc = acc + xbc_taps[M_CONV - 1 - kk] * cw[kk:kk + 1]
    acc = acc + cb
    return acc * jax.nn.sigmoid(acc), _softplus(dt_raw + dtb)


def _ssd_post(y, z, nw):
    y = y * (z * jax.nn.sigmoid(z))
    half = WIDTH // M_GROUPS
    parts = []
    for gi in range(M_GROUPS):
        yg = y[:, gi * half:(gi + 1) * half]
        parts.append(yg * lax.rsqrt(jnp.mean(yg * yg, axis=-1, keepdims=True) + M_EPS))
    return jnp.concatenate(parts, axis=1) * nw


def _ssd_chunk_math(xbc, dt, alog_row, acol, dsk, expand, h_scr):
    q = xbc.shape[0]
    pair = 2 * M_HEAD
    xs = xbc[:, :WIDTH]
    row = lax.broadcasted_iota(jnp.int32, (q, q), 0)
    col = lax.broadcasted_iota(jnp.int32, (q, q), 1)
    lower = row >= col
    dt_full = _dot_exact_rhs(dt, expand, terms=3)
    cum = _dot_exact_lhs(_bf(lower.astype(F32)), dt_full * -jnp.exp(alog_row))
    da_t = jnp.transpose(dt[:, :LANES])[:SUBLANES] * -jnp.exp(acol)
    cum_t = _dot_exact_rhs(da_t, _bf((row <= col).astype(F32)), terms=3)
    ecum = jnp.exp(cum)
    cum_last = cum[q - 1:q, :]
    xd = xs * dt_full
    xdec = xd * jnp.exp(cum_last - cum)
    lane = lax.broadcasted_iota(jnp.int32, (q, pair), 1)
    prow = lax.broadcasted_iota(jnp.int32, (pair, M_STATE), 0)
    heads_per_group = M_HEADS // M_GROUPS
    ys = []
    for gi in range(M_GROUPS):
        bg = xbc[:, WIDTH + gi * M_STATE:WIDTH + (gi + 1) * M_STATE]
        cg = xbc[:, WIDTH + (M_GROUPS + gi) * M_STATE:WIDTH + (M_GROUPS + gi + 1) * M_STATE]
        cb = _dot_nt(cg, bg)
        for pp in range(heads_per_group // 2):
            p = gi * (heads_per_group // 2) + pp
            lanes = slice(p * pair, (p + 1) * pair)
            ms = []
            for h in (2 * p, 2 * p + 1):
                ch = cum[:, h * M_HEAD:(h + 1) * M_HEAD]
                seg = jnp.concatenate([ch] * (q // M_HEAD), axis=1) - cum_t[h:h + 1, :]
                ms.append(jnp.where(lower, jnp.exp(seg), 0.0) * cb)
            xd_pair = xd[:, lanes]
            stacked = jnp.concatenate([jnp.where(lane < M_HEAD, xd_pair, 0.0),
                                       jnp.where(lane >= M_HEAD, xd_pair, 0.0)], axis=0)
            hs = h_scr[p]
            ys.append(_dot(jnp.concatenate(ms, axis=1), stacked) + _dot_nt(cg, hs) * ecum[:, lanes])
            keep = jnp.where(prow < M_HEAD, jnp.exp(cum_t[2 * p:2 * p + 1, q - 1:q]),
                             jnp.exp(cum_t[2 * p + 1:2 * p + 2, q - 1:q]))
            h_scr[p] = hs * keep + _dot_tn(xdec[:, lanes], bg)
    return jnp.concatenate(ys, axis=1) + dsk * xs


def _mamba_seq_kernel(x_ref, init_ref, h0_ref, gn_ref, wm_ref, cw_ref, cb_ref, dtb_ref, alog_ref, acol_ref, dsk_ref,
                      nw_ref, expand_ref, o_ref, last_ref, ht_ref, carry, h_scr):
    j = pl.program_id(1)
    n_pairs = M_HEADS // 2

    @pl.when(j == 0)
    def _():
        carry[...] = init_ref[...]
        for p in range(n_pairs):
            h_scr[p] = jnp.concatenate([h0_ref[2 * p], h0_ref[2 * p + 1]], axis=0)

    zm = jnp.dot(_bf(_rms(x_ref[...], gn_ref[...])), wm_ref[...], preferred_element_type=F32)
    raw = zm[:, :M_CONV_CH]
    z = zm[:, M_CONV_CH:M_CONV_CH + WIDTH]
    c8 = carry[...]
    taps = [raw] + [_delayed(raw, c8, kk) for kk in range(1, M_CONV)]
    tail = raw[raw.shape[0] - SUBLANES:, :]
    carry[...] = tail
    last_ref[...] = tail
    xbc, dt = _mamba_prep_math(taps, zm[:, M_CONV_CH + WIDTH:], cw_ref[...], cb_ref[...], dtb_ref[...])
    q = min(SSD_CHUNK, xbc.shape[0])
    y = jnp.concatenate(
        [_ssd_chunk_math(xbc[r0:r0 + q], dt[r0:r0 + q], alog_ref[...], acol_ref[...], dsk_ref[...], expand_ref[...],
                         h_scr) for r0 in range(0, xbc.shape[0], q)], axis=0)
    o_ref[...] = _ssd_post(y, z, nw_ref[...]).astype(o_ref.dtype)

    @pl.when(j == pl.num_programs(1) - 1)
    def _():
        for p in range(n_pairs):
            hs = h_scr[p]
            ht_ref[2 * p] = hs[:M_HEAD]
            ht_ref[2 * p + 1] = hs[M_HEAD:]


def _mamba_seq(x, conv_state, h0, weights):
    bsz, t, d = x.shape
    q = min(SSD_TILE, t)
    tile = lambda c: pl.BlockSpec((None, q, c), lambda b, j: (b, j, 0))
    edge = pl.BlockSpec((None, SUBLANES, M_CONV_CH), lambda b, j: (b, 0, 0))
    st = pl.BlockSpec((None, M_HEADS, M_HEAD, M_STATE), lambda b, j: (b, 0, 0, 0))
    return pl.pallas_call(
        _mamba_seq_kernel,
        grid=(bsz, t // q),
        in_specs=[tile(d), edge, st] + [_const(w.shape) for w in weights],
        out_specs=[tile(WIDTH), edge, st],
        out_shape=[jax.ShapeDtypeStruct((bsz, t, WIDTH), BRANCH_DTYPE),
                   jax.ShapeDtypeStruct((bsz, SUBLANES, M_CONV_CH), F32),
                   jax.ShapeDtypeStruct((bsz, M_HEADS, M_HEAD, M_STATE), F32)],
        scratch_shapes=[pltpu.VMEM((SUBLANES, M_CONV_CH), F32),
                        pltpu.VMEM((M_HEADS // 2, 2 * M_HEAD, M_STATE), F32)],
        compiler_params=_cp(("arbitrary", "arbitrary")),
        name="mamba_seq",
    )(x, conv_state, h0, *weights)


def _mamba_step_kernel(zm_ref, past_ref, h0_ref, all_ref, cw_ref, cb_ref, dtb_ref, alog_ref, dsk_ref, nw_ref,
                       expand_ref, o_ref, ht_ref, y_scr):
    del all_ref
    bt = zm_ref.shape[0]
    zm = zm_ref[...]
    raw = zm[:, :M_CONV_CH]
    z = zm[:, M_CONV_CH:M_CONV_CH + WIDTH]
    past = past_ref[...]
    taps = [raw] + [past[:, (M_CONV - 1 - kk) * M_CONV_CH:(M_CONV - kk) * M_CONV_CH] for kk in range(1, M_CONV)]
    xbc, dt = _mamba_prep_math(taps, zm[:, M_CONV_CH + WIDTH:], cw_ref[...], cb_ref[...], dtb_ref[...])
    xs = xbc[:, :WIDTH]
    dt_full = _dot_exact_rhs(dt, expand_ref[...], terms=3)
    keep = jnp.exp(dt_full * -jnp.exp(alog_ref[...]))
    xd = xs * dt_full
    heads_per_group = M_HEADS // M_GROUPS
    zero_x = jnp.zeros((1, WIDTH), F32)
    zero_g = jnp.zeros((1, M_GROUPS * M_STATE), F32)
    new = {}
    for i in range(bt):
        one = slice(i, i + 1)
        xrow = _rows(xd[one], zero_x)
        brow = _rows(xbc[one, WIDTH:WIDTH + M_GROUPS * M_STATE], zero_g)
        for h in range(M_HEADS):
            gi = h // heads_per_group
            cs_ = slice(h * M_HEAD, (h + 1) * M_HEAD)
            kp = keep[one, cs_]
            hn = (h0_ref[i, h] * jnp.concatenate([kp, kp], axis=1)
                  + _dot_tn(xrow[:, cs_], brow[:, gi * M_STATE:(gi + 1) * M_STATE]))
            ht_ref[i, h] = hn
            new[i, h] = hn
    for i in range(bt):
        one = slice(i, i + 1)
        crow = _rows(xbc[one, WIDTH + M_GROUPS * M_STATE:], zero_g)
        outs = [_dot_nt(crow[:, (h // heads_per_group) * M_STATE:(h // heads_per_group + 1) * M_STATE], new[i, h])
                for h in range(M_HEADS)]
        y_scr[one, :] = jnp.concatenate(outs, axis=1)[0:1]
    y = y_scr[...] + dsk_ref[...] * xs
    o_ref[...] = _ssd_post(y, z, nw_ref[...])


def _mamba_step(zm, past, h_all, h_new, layer, weights):
    n = zm.shape[0]
    bt = SUBLANES
    tile = lambda c: pl.BlockSpec((bt, c), lambda i: (i, 0))
    st = pl.BlockSpec((None, bt, M_HEADS, M_HEAD, M_STATE), lambda i: (layer, i, 0, 0, 0))
    return pl.pallas_call(
        _mamba_step_kernel,
        grid=(n // bt,),
        in_specs=[tile(M_COLS_PAD), tile((M_CONV - 1) * M_CONV_CH), st, pl.BlockSpec(memory_space=pl.ANY)]
        + [_const(w.shape) for w in weights],
        out_specs=[tile(WIDTH), st],
        out_shape=[jax.ShapeDtypeStruct((n, WIDTH), F32), jax.ShapeDtypeStruct(h_new.shape, F32)],
        scratch_shapes=[pltpu.VMEM((bt, WIDTH), F32)],
        input_output_aliases={3: 1},
        compiler_params=_cp(("arbitrary",)),
        name="mamba_step",
    )(zm, past, h_all, h_new, *weights)


def _merge_kernel(x_ref, or_ref, os_ref, om_ref, gn_ref, wg_ref, wb_ref, wo_ref, gp_ref, o_ref):
    x = x_ref[...]
    hb = _bf(_rms(x, gn_ref[...]))
    mixed = None
    for kk, ref in enumerate((or_ref, os_ref, om_ref)):
        gate = jax.nn.sigmoid(jnp.dot(hb, wg_ref[:, kk * D_MODEL:(kk + 1) * D_MODEL], preferred_element_type=F32))
        term = gate * jnp.dot(_bf(ref[...]), wb_ref[kk], preferred_element_type=F32)
        mixed = term if mixed is None else mixed + term
    out = jnp.dot(_bf(mixed), wo_ref[...], preferred_element_type=F32)
    o_ref[...] = x + _rms(out, gp_ref[...])


def _merge(x, o_r, o_s, o_m, weights):
    n = x.shape[0]
    tm = min(TOKEN_TILE, n)
    tile = lambda c: pl.BlockSpec((tm, c), lambda i: (i, 0))
    return pl.pallas_call(
        _merge_kernel,
        grid=(n // tm,),
        in_specs=[tile(D_MODEL)] + [tile(WIDTH)] * 3 + [_const(w.shape) for w in weights],
        out_specs=tile(D_MODEL),
        out_shape=jax.ShapeDtypeStruct((n, D_MODEL), F32),
        compiler_params=_cp(("arbitrary",)),
        name="merge",
    )(x, o_r, o_s, o_m, *weights)


def _ffn_chunk(hb, gate_taps_of, cols, wu_ref, cw_ref, cb_ref, wd_ref):
    gate = jnp.dot(hb, wu_ref[:, cols], preferred_element_type=F32)
    val = jnp.dot(hb, wu_ref[:, D_FF + cols.start:D_FF + cols.stop], preferred_element_type=F32)
    taps = gate_taps_of(gate)
    acc = taps[FFN_CONV - 1] * cw_ref[0:1, cols]
    for kk in range(1, FFN_CONV):
        acc = acc + taps[FFN_CONV - 1 - kk] * cw_ref[kk:kk + 1, cols]
    acc = acc + cb_ref[:, cols]
    return jnp.dot(_bf(jax.nn.gelu(acc) * val), wd_ref[cols, :], preferred_element_type=F32), gate


def _ffn_seq_kernel(x_ref, init_ref, gn_ref, wu_ref, cw_ref, cb_ref, wd_ref, gp_ref, o_ref, last_ref, carry):
    @pl.when(pl.program_id(1) == 0)
    def _():
        carry[...] = init_ref[...]

    x = x_ref[...]
    hb = _bf(_rms(x, gn_ref[...]))
    f = None
    for c0, c1 in zip(FFN_COL_STARTS[:-1], FFN_COL_STARTS[1:]):
        cols = slice(c0, c1)
        c8 = carry[:, cols]
        part, gate = _ffn_chunk(hb, lambda g: [g] + [_delayed(g, c8, kk) for kk in range(1, FFN_CONV)], cols,
                                wu_ref, cw_ref, cb_ref, wd_ref)
        tail = gate[gate.shape[0] - SUBLANES:, :]
        carry[:, cols] = tail
        last_ref[:, cols] = tail
        f = part if f is None else f + part
    o_ref[...] = x + _rms(f, gp_ref[...])


def _ffn_step_kernel(x_ref, past_ref, gn_ref, wu_ref, cw_ref, cb_ref, wd_ref, gp_ref, o_ref, gate_ref):
    x = x_ref[...]
    hb = _bf(_rms(x, gn_ref[...]))
    f = None
    for c0, c1 in zip(FFN_COL_STARTS[:-1], FFN_COL_STARTS[1:]):
        cols = slice(c0, c1)
        past = [past_ref[:, (FFN_CONV - 1 - kk) * D_FF + c0:(FFN_CONV - 1 - kk) * D_FF + c1]
                for kk in range(1, FFN_CONV)]
        part, gate = _ffn_chunk(hb, lambda g: [g] + past, cols, wu_ref, cw_ref, cb_ref, wd_ref)
        gate_ref[:, cols] = gate
        f = part if f is None else f + part
    o_ref[...] = x + _rms(f, gp_ref[...])


def _ffn(x, conv_state, weights, seq):
    wspecs = [_const(w.shape) for w in weights]
    if seq:
        bsz, t, d = x.shape
        tm = min(TOKEN_TILE, t)
        tile = pl.BlockSpec((None, tm, d), lambda b, j: (b, j, 0))
        edge = pl.BlockSpec((None, SUBLANES, D_FF), lambda b, j: (b, 0, 0))
        return pl.pallas_call(
            _ffn_seq_kernel,
            grid=(bsz, t // tm),
            in_specs=[tile, edge] + wspecs,
            out_specs=[tile, edge],
            out_shape=[jax.ShapeDtypeStruct((bsz, t, d), F32),
                       jax.ShapeDtypeStruct((bsz, SUBLANES, D_FF), F32)],
            scratch_shapes=[pltpu.VMEM((SUBLANES, D_FF), F32)],
            compiler_params=_cp(("arbitrary", "arbitrary")),
            name="ffn_seq",
        )(x, conv_state, *weights)
    n, d = x.shape
    return pl.pallas_call(
        _ffn_step_kernel,
        grid=(1,),
        in_specs=[_full((n, d)), _full(conv_state.shape)] + wspecs,
        out_specs=[_full((n, d)), _full((n, D_FF))],
        out_shape=[jax.ShapeDtypeStruct((n, d), F32), jax.ShapeDtypeStruct((n, D_FF), F32)],
        compiler_params=_cp(("arbitrary",)),
        name="ffn_step",
    )(x, conv_state, *weights)


def _row(v):
    return v.reshape(1, -1).astype(F32)


def _block_diag_ones():
    head = jnp.arange(WIDTH) // R_HEAD
    return (head[:, None] == head[None, :]).astype(BF16)


def _layer_weights(lw):
    (g_pre_mix, g_post_mix, g_pre_ffn, g_post_ffn, w_in,
     r_mu, r_w0, r_w2, r_a0, r_a2, r_g2, r_kk, r_ka, r_rk, r_ln_w, r_ln_b,
     s5_lam_re, s5_lam_im, s5_log_step, s5_b_re, s5_b_im, s5_c_re, s5_c_im, s5_d, s5_w_glu, s5_b_glu,
     m_conv_w, m_conv_b, m_dt_bias, m_a_log, m_d, m_norm_w,
     w_branch, w_out, w_up, f_conv_w, f_conv_b, w_down) = lw
    c0 = R_COLS
    c1 = c0 + WIDTH
    c2 = c1 + WIDTH + M_CONV_CH
    c3 = c2 + M_HEADS
    ones_bd = _block_diag_ones()
    gn = _row(g_pre_mix)
    w = {"g_pre_mix": gn}
    w_r = _bf(w_in[:, :c0])
    w_m = _bf(jnp.concatenate(
        [w_in[:, c1 + WIDTH:c2], w_in[:, c1:c1 + WIDTH], w_in[:, c2:c3],
         jnp.zeros((D_MODEL, M_DT_PAD - M_HEADS), F32)], axis=1))
    w["w_r"], w["w_m"] = w_r, w_m
    rwkv_mix = (_row(r_mu), _row(r_w0), _bf(r_w2), _row(r_a0), _bf(r_a2), _bf(r_g2), _row(r_kk), _row(r_ka),
                ones_bd, _row(r_rk), _row(r_ln_w), _row(r_ln_b))
    w["rwkv_seq"] = (gn, w_r) + rwkv_mix
    w["rwkv_step"] = rwkv_mix
    ab_re, ab_im, bb_re, bb_im = _s5_params(s5_lam_re, s5_lam_im, s5_log_step, s5_b_re, s5_b_im)
    gpb = S5_GROUPS // S5_BLOCKS
    eye = jnp.eye(gpb, dtype=F32)
    blocked = lambda m: m.reshape(S5_BLOCKS, gpb, S5_GROUP, S5_STATE)
    to_state = lambda bb: _bf(jnp.einsum('qghp,gk->qghkp', blocked(bb), eye)
                              .reshape(S5_BLOCKS, S5_BLOCK_IN, S5_BLOCK_STATE))
    from_state = lambda c: _bf(jnp.einsum('qghp,gk->qgpkh', blocked(c), eye)
                               .reshape(S5_BLOCKS, S5_BLOCK_STATE, S5_BLOCK_IN))
    w["s5"] = (gn, _bf(w_in[:, c0:c1]), to_state(bb_re), to_state(bb_im), from_state(s5_c_re), from_state(s5_c_im),
               ab_re.reshape(1, S5_LANES), ab_im.reshape(1, S5_LANES), _row(s5_d), _bf(s5_w_glu), _row(s5_b_glu))
    conv = (m_conv_w.T.astype(F32), _row(m_conv_b), _row(jnp.pad(m_dt_bias, (0, M_DT_PAD - M_HEADS))))
    alog_row = _row(jnp.repeat(m_a_log, M_HEAD))
    tail = (_row(jnp.repeat(m_d, M_HEAD)), _row(m_norm_w),
            (jnp.arange(M_DT_PAD)[:, None] == (jnp.arange(WIDTH) // M_HEAD)[None, :]).astype(BF16))
    w["mamba_seq"] = (gn, w_m) + conv + (alog_row, m_a_log.reshape(M_HEADS, 1).astype(F32)) + tail
    w["mamba_step"] = conv + (alog_row,) + tail
    w["merge"] = (gn, _bf(w_in[:, c3:]), _bf(w_branch), _bf(w_out), _row(g_post_mix))
    w["ffn"] = (_row(g_pre_ffn), _bf(w_up), f_conv_w.T.astype(F32), _row(f_conv_b), _bf(w_down), _row(g_post_ffn))
    return w


def _pad_rows(state):
    return jnp.pad(state, ((0, 0), (SUBLANES - state.shape[1], 0), (0, 0)))


def _layer(x, states, w, bsz, t, layer=0):
    shift0, wkv0, s5r0, s5i0, ssd0, mconv0, fconv0 = states
    seq = t > 1
    n = bsz * t
    d = x.shape[-1]
    flat = lambda a: a.reshape(n, a.shape[-1])
    s5r0 = s5r0.reshape(bsz, S5_LANES)
    s5i0 = s5i0.reshape(bsz, S5_LANES)
    if seq:
        x3 = x.reshape(bsz, t, d)
        o_r, last, wkv1 = _rwkv_seq(x3, jnp.broadcast_to(shift0[:, None], (bsz, SUBLANES, R_COLS)), wkv0,
                                    w["rwkv_seq"])
        shift1 = last[:, SUBLANES - 1]
        o_s, s5r1, s5i1 = _s5(x3, s5r0, s5i0, w["s5"], True)
        o_m, last, ssd1 = _mamba_seq(x3, _pad_rows(mconv0), ssd0, w["mamba_seq"])
        mconv1 = last[:, SUBLANES - (M_CONV - 1):]
        o_r, o_s, o_m = flat(o_r), flat(o_s), flat(o_m)
    else:
        z_r = _norm_matmul(x, w["g_pre_mix"], w["w_r"], "in_proj_rwkv")
        z_m = _norm_matmul(x, w["g_pre_mix"], w["w_m"], "in_proj_mamba")
        shift1 = z_r
        o_r, wkv1 = _rwkv_step(z_r, shift0, *wkv0, layer, w["rwkv_step"])
        o_s, s5r1, s5i1 = _s5(x, s5r0, s5i0, w["s5"], False)
        o_m, ssd1 = _mamba_step(z_m, mconv0.reshape(bsz, (M_CONV - 1) * M_CONV_CH), *ssd0, layer, w["mamba_step"])
        mconv1 = jnp.concatenate([mconv0[:, 1:], z_m[:, None, :M_CONV_CH]], axis=1)
    s5r1 = s5r1.reshape(bsz, S5_GROUPS, S5_STATE)
    s5i1 = s5i1.reshape(bsz, S5_GROUPS, S5_STATE)

    x = _merge(x, o_r, o_s, o_m, w["merge"])

    if seq:
        x3, last = _ffn(x.reshape(bsz, t, d), _pad_rows(fconv0), w["ffn"], True)
        x = flat(x3)
        fconv1 = last[:, SUBLANES - (FFN_CONV - 1):]
    else:
        x, gate = _ffn(x, fconv0.reshape(bsz, (FFN_CONV - 1) * D_FF), w["ffn"], False)
        fconv1 = jnp.concatenate([fconv0[:, 1:], gate[:, None]], axis=1)
    return x, (shift1, wkv1, s5r1, s5i1, ssd1, mconv1, fconv1)


def _zero_states(n):
    return (jnp.zeros((n, R_COLS), F32),
            jnp.zeros((n, R_HEADS, R_HEAD, R_HEAD), F32),
            jnp.zeros((n, S5_GROUPS, S5_STATE), F32),
            jnp.zeros((n, S5_GROUPS, S5_STATE), F32),
            jnp.zeros((n, M_HEADS, M_HEAD, M_STATE), F32),
            jnp.zeros((n, M_CONV - 1, M_CONV_CH), F32),
            jnp.zeros((n, FFN_CONV - 1, D_FF), F32))


def kernel(x_prompt, x_sample, state_rwkv_shift, state_rwkv_wkv, state_s5_re, state_s5_im, state_ssd, state_ssd_conv, state_ffn_conv, g_pre_mix, g_post_mix, g_pre_ffn, g_post_ffn, w_in, r_mu, r_w0, r_w2, r_a0, r_a2, r_g2, r_kk, r_ka, r_rk, r_ln_w, r_ln_b, s5_lam_re, s5_lam_im, s5_log_step, s5_b_re, s5_b_im, s5_c_re, s5_c_im, s5_d, s5_w_glu, s5_b_glu, m_conv_w, m_conv_b, m_dt_bias, m_a_log, m_d, m_norm_w, w_branch, w_out, w_up, f_conv_w, f_conv_b, w_down):
    stacked = (g_pre_mix, g_post_mix, g_pre_ffn, g_post_ffn, w_in,
               r_mu, r_w0, r_w2, r_a0, r_a2, r_g2, r_kk, r_ka, r_rk, r_ln_w, r_ln_b,
               s5_lam_re, s5_lam_im, s5_log_step, s5_b_re, s5_b_im, s5_c_re, s5_c_im, s5_d,
               s5_w_glu, s5_b_glu,
               m_conv_w, m_conv_b, m_dt_bias, m_a_log, m_d, m_norm_w,
               w_branch, w_out, w_up, f_conv_w, f_conv_b, w_down)
    cache_in = (state_rwkv_shift, state_rwkv_wkv, state_s5_re, state_s5_im,
                state_ssd, state_ssd_conv, state_ffn_conv)
    depth = w_in.shape[0]
    pb, pt, d = x_prompt.shape
    sb, s_t, _ = x_sample.shape
    xp = x_prompt.reshape(pb * pt, d)
    xs = x_sample.reshape(sb * s_t, d)
    new_p = [[] for _ in cache_in]
    new_s = [[] for _ in cache_in]
    in_place = (1, 4)
    wkv_all, ssd_all = jnp.zeros_like(state_rwkv_wkv), jnp.zeros_like(state_ssd)
    for l in range(depth):
        w = _layer_weights(tuple(a[l] for a in stacked))
        xp, sp = _layer(xp, _zero_states(pb), w, pb, pt)
        layer_states = [c[l] for c in cache_in]
        layer_states[1], layer_states[4] = (state_rwkv_wkv, wkv_all), (state_ssd, ssd_all)
        xs, ss = _layer(xs, tuple(layer_states), w, sb, s_t, layer=l)
        wkv_all, ssd_all = ss[1], ss[4]
        for i in range(len(cache_in)):
            new_p[i].append(sp[i])
            if i not in in_place:
                new_s[i].append(ss[i])
    outs = [xp.reshape(pb, pt, d), xs.reshape(sb, s_t, d)]
    for i, (p_list, s_list) in enumerate(zip(new_p, new_s)):
        outs.append(jnp.stack(p_list, 0))
        outs.append({1: wkv_all, 4: ssd_all}[i] if i in in_place else jnp.stack(s_list, 0))
    return tuple(outs)
```

```python
import functools

import jax
import jax.numpy as jnp
from jax import lax
from jax.experimental import pallas as pl
from jax.experimental.pallas import tpu as pltpu

F32 = jnp.float32
BF16 = jnp.bfloat16

D_MODEL = 1024
WIDTH = 512
R_HEADS, R_HEAD = 8, 64
R_COLS = 1792
R_LN_EPS = 64e-5
S5_GROUPS, S5_GROUP, S5_STATE = 32, 16, 64
S5_LANES = S5_GROUPS * S5_STATE
M_HEADS, M_HEAD, M_GROUPS, M_STATE = 8, 64, 2, 128
M_CONV, M_CONV_CH = 4, 1024
M_DT_PAD = 256
M_COLS_PAD = M_CONV_CH + WIDTH + M_DT_PAD
M_EPS = 1e-5
D_FF = 2816
FFN_CONV = 3
EPS = 1e-6
SUBLANES = 8
LANES = 128

RWKV_CHUNK = 64
RWKV_TILE = 256
SSD_CHUNK = 128
S5_CHUNK = 128
S5_ROW_PAD = 8
S5_SCAN_TILES = 8
SSD_TILE = 256
BRANCH_DTYPE = BF16
TOKEN_TILE = 512
FFN_COL_STARTS = (0, D_FF // 2, D_FF)
VMEM_LIMIT = 56 * 1024 * 1024


def _cp(sem):
    return pltpu.CompilerParams(dimension_semantics=sem, vmem_limit_bytes=VMEM_LIMIT)


def _layer_const(w, layer):
    shape = w.shape[1:]
    nd = len(shape)
    return pl.BlockSpec((None,) + shape, lambda *_: (layer,) + (0,) * nd, pipeline_mode=pl.Buffered(1))


def _full(shape):
    nd = len(shape)
    return pl.BlockSpec(shape, lambda *_: (0,) * nd)


def _bf(x):
    return x.astype(BF16)


def _dot(a, b):
    return jnp.dot(_bf(a), _bf(b), preferred_element_type=F32)


def _dot_nt(a, b):
    return lax.dot_general(_bf(a), _bf(b), (((1,), (1,)), ((), ())), preferred_element_type=F32)


def _dot_tn(a, b):
    return lax.dot_general(_bf(a), _bf(b), (((0,), (0,)), ((), ())), preferred_element_type=F32)


def _split(x, terms):
    out = []
    for _ in range(terms - 1):
        h = _bf(x)
        out.append(h)
        x = x - h.astype(F32)
    out.append(_bf(x))
    return out


def _dot_exact_lhs(m_bf16, x, terms=3):
    acc = None
    for h in _split(x, terms):
        p = jnp.dot(m_bf16, h, preferred_element_type=F32)
        acc = p if acc is None else acc + p
    return acc


def _dot_exact_rhs(x, m_bf16, terms=2):
    acc = None
    for h in _split(x, terms):
        p = jnp.dot(h, m_bf16, preferred_element_type=F32)
        acc = p if acc is None else acc + p
    return acc


def _softplus(x):
    return jnp.maximum(x, 0.0) + jnp.log1p(jnp.exp(-jnp.abs(x)))


def _rms(x, g, eps=EPS):
    return x * lax.rsqrt(jnp.mean(x * x, axis=-1, keepdims=True) + eps) * g


def _delayed(x, carry8, k):
    rx = pltpu.roll(x, k, 0)
    rc = pltpu.roll(carry8, k, 0)
    row = lax.broadcasted_iota(jnp.int32, (SUBLANES, x.shape[1]), 0)
    head = jnp.where(row < k, rc, rx[:SUBLANES])
    if x.shape[0] == SUBLANES:
        return head
    return jnp.concatenate([head, rx[SUBLANES:]], axis=0)


def _row_group(t, rows):
    start = t * rows
    return pl.multiple_of(start, SUBLANES) if rows % SUBLANES == 0 else start


def _rows(first, second):
    row = lax.broadcasted_iota(jnp.int32, (SUBLANES, first.shape[1]), 0)
    return jnp.where(row == 0, first, jnp.where(row == 1, second, 0.0))


def _norm_matmul_kernel(x_ref, g_ref, w_ref, o_ref):
    o_ref[...] = jnp.dot(_bf(_rms(x_ref[...], g_ref[...])), w_ref[...], preferred_element_type=F32)


def _norm_matmul(x, g, w, layer, name):
    n, d = x.shape
    c = w.shape[-1]
    return pl.pallas_call(
        _norm_matmul_kernel,
        grid=(1,),
        in_specs=[_full((n, d)), _layer_const(g, layer), _layer_const(w, layer)],
        out_specs=_full((n, c)),
        out_shape=jax.ShapeDtypeStruct((n, c), F32),
        compiler_params=_cp(("arbitrary",)),
        name=name,
    )(x, g, w)


def _rwkv_prep_math(z, prev, mu, w0, w2, a0, a2, g2, kkw, kaw, ones_bd):
    zm = z + (prev - z) * mu
    r = zm[:, 0:WIDTH]
    k = zm[:, WIDTH:2 * WIDTH]
    v = zm[:, 2 * WIDTH:3 * WIDTH]
    dw = zm[:, 1536:1600]
    da = zm[:, 1600:1664]
    dg = zm[:, 1664:1792]
    logw = -_softplus(-(w0 + _dot(jnp.tanh(dw), w2))) - 0.5
    ld = -jnp.exp(logw)
    a = jax.nn.sigmoid(a0 + _dot(da, a2))
    g = _dot(jax.nn.sigmoid(dg), g2)
    kk = k * kkw
    ss = _dot_exact_rhs(kk * kk, ones_bd)
    kk = kk / jnp.maximum(jnp.sqrt(ss), 1e-12)
    k2 = k * (1.0 + (a - 1.0) * kaw)
    return r, ld, k2, v, kk, kk * a, g


def _rwkv_post(y, r, k, v, g, rk, lnw, lnb, ones_bd):
    inv = 1.0 / R_HEAD
    mean = _dot_exact_rhs(y, ones_bd) * inv
    d = y - mean
    var = _dot_exact_rhs(d * d, ones_bd) * inv
    yn = d * lax.rsqrt(var + R_LN_EPS) * lnw + lnb
    bonus = _dot_exact_rhs(r * k * rk, ones_bd) * v
    return (yn + bonus) * g


def _rwkv_chunks(r, ld, k, v, kk, b, s_scr):
    L = RWKV_CHUNK
    pair = 2 * R_HEAD
    n_pairs = R_HEADS // 2
    tm = r.shape[0]
    n_chunks = tm // L
    trow = lax.broadcasted_iota(jnp.int32, (tm, tm), 0)
    tcol = lax.broadcasted_iota(jnp.int32, (tm, tm), 1)
    same_chunk = (trow // L) == (tcol // L)
    cum = _dot_exact_lhs(_bf(((trow >= tcol) & same_chunk).astype(F32)), ld)
    wc = jnp.exp(cum)
    winv = jnp.exp(-cum)
    r_t = r * wc
    kk_t = kk * jnp.exp(cum - ld)
    k_h = k * winv
    b_h = b * winv

    row = lax.broadcasted_iota(jnp.int32, (pair, pair), 0)
    col = lax.broadcasted_iota(jnp.int32, (pair, pair), 1)
    same_head = (row // R_HEAD) == (col // R_HEAD)
    lrow = lax.broadcasted_iota(jnp.int32, (L, pair), 0)
    lcol = lax.broadcasted_iota(jnp.int32, (L, pair), 1) % R_HEAD
    strict = lrow > lcol
    lower = lrow >= lcol

    def bd(x):
        return jnp.where(same_head, jnp.concatenate([x, x], axis=0), 0.0)

    units = [(c, p) for c in range(n_chunks) for p in range(n_pairs)]
    pre = {}
    for c, p in units:
        rows = slice(c * L, (c + 1) * L)
        lanes = slice(p * pair, (p + 1) * pair)
        wl = wc[(c + 1) * L - 1:(c + 1) * L, lanes]
        pre[c, p] = dict(kkt=kk_t[rows, lanes], rt=r_t[rows, lanes], kh=k_h[rows, lanes], bh=b_h[rows, lanes],
                         v=v[rows, lanes], vbd=bd(v[rows, lanes]), wl=wl)
    for u in units:
        d = pre[u]
        a = _dot_nt(jnp.concatenate([d["kkt"], d["rt"]], axis=0),
                    jnp.concatenate([bd(d["kh"]), bd(d["bh"])], axis=0))
        d["akk_k"] = jnp.where(strict, a[:L, :pair], 0.0)
        d["n"] = jnp.where(strict, a[:L, pair:], 0.0)
        d["ar_k"] = jnp.where(lower, a[L:, :pair], 0.0)
        d["ar_b"] = jnp.where(lower, a[L:, pair:], 0.0)
        d["q"] = -d["n"]
        d["m"] = d["n"]
    power = 2
    while power < L:
        for u in units:
            d = pre[u]
            d["m"] = _dot(d["m"], bd(d["m"]))
            d["q"] = d["q"] + d["m"] + _dot(d["q"], bd(d["m"]))
        power *= 2
    for u in units:
        d = pre[u]
        xy = _dot(jnp.concatenate([d["akk_k"], d["ar_k"]], axis=0), d["vbd"])
        x, d["y0"] = xy[:L], xy[L:]
        both = jnp.concatenate([d["kkt"], x], axis=1)
        both = both + _dot(d["q"], jnp.concatenate([bd(d["kkt"]), bd(x)], axis=1))
        d["g"], d["u0"] = both[:, :pair], both[:, pair:]
    for u in units:
        d = pre[u]
        t = _dot(d["ar_b"], jnp.concatenate([bd(d["g"]), bd(d["u0"])], axis=1))
        d["ry"] = d["rt"] - t[:, :pair]
        d["y0"] = d["y0"] - t[:, pair:]
        kw = d["kh"] * d["wl"]
        bw = d["bh"] * d["wl"]
        d["pm"] = jnp.where(same_head, _dot_tn(d["g"], bw), 0.0)
        d["c"] = jnp.where(same_head, _dot_tn(jnp.concatenate([d["v"], -d["u0"]], axis=0),
                                               jnp.concatenate([kw, bw], axis=0)), 0.0)
    ys = []
    for c in range(n_chunks):
        parts = []
        for p in range(n_pairs):
            d = pre[c, p]
            s = s_scr[p]
            parts.append(_dot_nt(d["ry"], s) + d["y0"])
            s_scr[p] = s * d["wl"] - _dot(s, d["pm"]) + d["c"]
        ys.append(jnp.concatenate(parts, axis=1))
    return jnp.concatenate(ys, axis=0)


def _rwkv_seq_kernel(x_ref, init_ref, s0_ref, gn_ref, wr_ref, mu_ref, w0_ref, w2_ref, a0_ref, a2_ref, g2_ref,
                     kkw_ref, kaw_ref, ones_ref, rk_ref, lnw_ref, lnb_ref, o_ref, last_ref, st_ref, carry, s_scr):
    j = pl.program_id(1)
    n_pairs = R_HEADS // 2

    @pl.when(j == 0)
    def _():
        carry[...] = init_ref[...]
        zero = jnp.zeros((R_HEAD, R_HEAD), F32)
        for p in range(n_pairs):
            top = jnp.concatenate([s0_ref[2 * p], zero], axis=1)
            bot = jnp.concatenate([zero, s0_ref[2 * p + 1]], axis=1)
            s_scr[p] = jnp.concatenate([top, bot], axis=0)

    z = jnp.dot(_bf(_rms(x_ref[...], gn_ref[...])), wr_ref[...], preferred_element_type=F32)
    prev = _delayed(z, carry[...], 1)
    tail = z[z.shape[0] - SUBLANES:, :]
    carry[...] = tail
    last_ref[...] = tail
    ones_bd = ones_ref[...]
    r, ld, k, v, kk, b, g = _rwkv_prep_math(z, prev, mu_ref[...], w0_ref[...], w2_ref[...], a0_ref[...],
                                            a2_ref[...], g2_ref[...], kkw_ref[...], kaw_ref[...], ones_bd)
    y = _rwkv_chunks(r, ld, k, v, kk, b, s_scr)
    o_ref[...] = _rwkv_post(y, r, k, v, g, rk_ref[...], lnw_ref[...], lnb_ref[...], ones_bd).astype(o_ref.dtype)

    @pl.when(j == pl.num_programs(1) - 1)
    def _():
        for p in range(n_pairs):
            s = s_scr[p]
            st_ref[2 * p] = s[:R_HEAD, :R_HEAD]
            st_ref[2 * p + 1] = s[R_HEAD:, R_HEAD:]


def _rwkv_seq(x, shift, s0, layer, weights):
    bsz, t, d = x.shape
    tm = min(RWKV_TILE, t)
    tile = lambda c: pl.BlockSpec((None, tm, c), lambda b, j: (b, j, 0))
    edge = pl.BlockSpec((None, SUBLANES, R_COLS), lambda b, j: (b, 0, 0))
    st = pl.BlockSpec((None, R_HEADS, R_HEAD, R_HEAD), lambda b, j: (b, 0, 0, 0))
    return pl.pallas_call(
        _rwkv_seq_kernel,
        grid=(bsz, t // tm),
        in_specs=[tile(d), edge, st] + [_layer_const(w, layer) for w in weights],
        out_specs=[tile(WIDTH), edge, st],
        out_shape=[jax.ShapeDtypeStruct((bsz, t, WIDTH), BRANCH_DTYPE),
                   jax.ShapeDtypeStruct((bsz, SUBLANES, R_COLS), F32),
                   jax.ShapeDtypeStruct((bsz, R_HEADS, R_HEAD, R_HEAD), F32)],
        scratch_shapes=[pltpu.VMEM((SUBLANES, R_COLS), F32),
                        pltpu.VMEM((R_HEADS // 2, 2 * R_HEAD, 2 * R_HEAD), F32)],
        compiler_params=_cp(("arbitrary", "arbitrary")),
        name="rwkv_seq",
    )(x, shift, s0, *weights)


def _rwkv_step_kernel(z_ref, prev_ref, s0_ref, *refs, chained):
    (mu_ref, w0_ref, w2_ref, a0_ref, a2_ref, g2_ref, kkw_ref, kaw_ref, ones_ref, rk_ref, lnw_ref, lnb_ref,
     o_ref, st_ref, y_scr) = refs[1:] if chained else refs
    bt = z_ref.shape[0]
    ones_bd = ones_ref[...]
    r, ld, k, v, kk, b, g = _rwkv_prep_math(z_ref[...], prev_ref[...], mu_ref[...], w0_ref[...], w2_ref[...],
                                            a0_ref[...], a2_ref[...], g2_ref[...], kkw_ref[...], kaw_ref[...],
                                            ones_bd)
    w = jnp.exp(ld)
    wr = w * r
    b_dot_r = _dot_exact_rhs(b * r, ones_bd)
    k_dot_r = _dot_exact_rhs(k * r, ones_bd)

    projs = []
    for i in range(bt):
        one = slice(i, i + 1)
        lhs = _rows(kk[one], wr[one])
        projs.append(jnp.concatenate(
            [_dot_nt(lhs[:, h * R_HEAD:(h + 1) * R_HEAD], s0_ref[i, h]) for h in range(R_HEADS)], axis=1))
    pieces = []
    for i in range(bt):
        one = slice(i, i + 1)
        sa = projs[i][0:1]
        y_scr[one, :] = projs[i][1:2] - sa * b_dot_r[one] + v[one] * k_dot_r[one]
        pieces.append(_split(_rows(v[one], -sa), 2) + _split(_rows(k[one], b[one]), 2))
    for i in range(bt):
        one = slice(i, i + 1)
        l_hi, l_lo, r_hi, r_lo = pieces[i]
        for h in range(R_HEADS):
            cs = slice(h * R_HEAD, (h + 1) * R_HEAD)
            tn = lambda x, y: lax.dot_general(x[:, cs], y[:, cs], (((0,), (0,)), ((), ())),
                                              preferred_element_type=F32)
            upd = tn(l_hi, r_hi) + tn(l_hi, r_lo) + tn(l_lo, r_hi)
            st_ref[i, h] = s0_ref[i, h] * w[one, cs] + upd
    o_ref[...] = _rwkv_post(y_scr[...], r, k, v, g, rk_ref[...], lnw_ref[...], lnb_ref[...], ones_bd)


def _chain_args(result):
    if result is None:
        return [], [], {}
    return [result], [pl.BlockSpec(memory_space=pl.ANY)], {3: 1}


def _rwkv_step(z, shift_all, s_all, s_new, layer, weights):
    n = z.shape[0]
    bt = SUBLANES
    tile = lambda c: pl.BlockSpec((bt, c), lambda i: (i, 0))
    prev = pl.BlockSpec((None, bt, R_COLS), lambda i: (layer, i, 0))
    st = pl.BlockSpec((None, bt, R_HEADS, R_HEAD, R_HEAD), lambda i: (layer, i, 0, 0, 0))
    chain, chain_specs, aliases = _chain_args(s_new)
    return pl.pallas_call(
        functools.partial(_rwkv_step_kernel, chained=bool(chain)),
        grid=(n // bt,),
        in_specs=[tile(R_COLS), prev, st] + chain_specs + [_layer_const(w, layer) for w in weights],
        out_specs=[tile(WIDTH), st],
        out_shape=[jax.ShapeDtypeStruct((n, WIDTH), F32), jax.ShapeDtypeStruct(s_all.shape, F32)],
        scratch_shapes=[pltpu.VMEM((bt, WIDTH), F32)],
        input_output_aliases=aliases,
        compiler_params=_cp(("arbitrary",)),
        name="rwkv_step",
    )(z, shift_all, s_all, *chain, *weights)


S5_BLOCKS = 4
S5_BLOCK_IN = WIDTH // S5_BLOCKS
S5_BLOCK_STATE = S5_LANES // S5_BLOCKS
S5_BLOCK_TILES = S5_BLOCK_STATE // LANES


def _s5_param_kernel(lr_ref, li_ref, ls_ref, br_ref, bi_ref, abr_ref, abi_ref, bbr_ref, bbi_ref):
    lr, li = lr_ref[...], li_ref[...]
    delta = jnp.exp(ls_ref[...])
    mag = jnp.exp(lr * delta)
    ab_re = mag * jnp.cos(li * delta)
    ab_im = mag * jnp.sin(li * delta)
    den = lr * lr + li * li
    nr, ni = ab_re - 1.0, ab_im
    cf_re = (nr * lr + ni * li) / den
    cf_im = (ni * lr - nr * li) / den
    abr_ref[...] = ab_re
    abi_ref[...] = ab_im
    br, bi = br_ref[...], bi_ref[...]
    bbr_ref[...] = cf_re[:, None, :] * br - cf_im[:, None, :] * bi
    bbi_ref[...] = cf_re[:, None, :] * bi + cf_im[:, None, :] * br


def _s5_params(lam_re, lam_im, log_step, b_re, b_im):
    depth, g, p, h = b_re.shape
    gp = jax.ShapeDtypeStruct((depth, g, p), F32)
    ghp = jax.ShapeDtypeStruct((depth, g, h, p), F32)
    args = (lam_re, lam_im, log_step.reshape(depth, g, 1), jnp.swapaxes(b_re, 2, 3), jnp.swapaxes(b_im, 2, 3))
    per_layer = lambda shape: pl.BlockSpec((None,) + shape, lambda l: (l,) + (0,) * len(shape))
    return pl.pallas_call(
        _s5_param_kernel,
        grid=(depth,),
        in_specs=[per_layer(a.shape[1:]) for a in args],
        out_specs=[per_layer((g, p)), per_layer((g, p)), per_layer((g, h, p)), per_layer((g, h, p))],
        out_shape=[gp, gp, ghp, ghp],
        compiler_params=_cp(("arbitrary",)),
        name="s5_params",
    )(*args)


def _s5_drive(ub, wbr_ref, wbi_ref, blk):
    cols = slice(blk * S5_BLOCK_IN, (blk + 1) * S5_BLOCK_IN)
    return (jnp.dot(ub[:, cols], wbr_ref[blk], preferred_element_type=F32),
            jnp.dot(ub[:, cols], wbi_ref[blk], preferred_element_type=F32))


def _s5_readout(u, xr_blocks, xi_blocks, wcr_ref, wci_ref, d, wg, bg):
    y = jnp.concatenate([_dot(xr_blocks[q], wcr_ref[q]) - _dot(xi_blocks[q], wci_ref[q])
                         for q in range(S5_BLOCKS)], axis=1) + d * u
    y = jax.nn.gelu(y)
    return y * jax.nn.sigmoid(_dot(y, wg) + bg)


def _s5_seq_kernel(x_ref, s0r_ref, s0i_ref, gn_ref, ws_ref, wbr_ref, wbi_ref, wcr_ref, wci_ref, ar_ref, ai_ref,
                   d_ref, wg_ref, bg_ref, o_ref, str_ref, sti_ref, xr_scr, xi_scr, cr_scr, ci_scr, seq_scr, tm_scr,
                   *, bt, tc):
    @pl.when(pl.program_id(0) == 0)
    def _():
        cr_scr[...] = s0r_ref[...]
        ci_scr[...] = s0i_ref[...]

    pitch = tc + S5_ROW_PAD
    in_tiles = WIDTH // LANES
    n_tiles = S5_LANES // LANES
    x = x_ref[...].reshape(bt * tc, D_MODEL)
    u_seq = jnp.dot(_bf(_rms(x, gn_ref[...])), ws_ref[...], preferred_element_type=F32)
    for c in range(in_tiles):
        for s in range(bt):
            seq_scr[c, s * pitch:s * pitch + tc, :] = u_seq[s * tc:(s + 1) * tc, c * LANES:(c + 1) * LANES]

    def to_time_major(t, carry):
        dst = pl.ds(_row_group(t, bt), bt)
        for c in range(in_tiles):
            tm_scr[c, dst, :] = seq_scr[c, pl.ds(t, bt, stride=pitch), :]
        return carry

    lax.fori_loop(0, tc, to_time_major, 0, unroll=2)
    u = jnp.concatenate([tm_scr[c] for c in range(in_tiles)], axis=1)
    ub = _bf(u)
    for q in range(S5_BLOCKS):
        bu_re, bu_im = _s5_drive(ub, wbr_ref, wbi_ref, q)
        for c in range(S5_BLOCK_TILES):
            xr_scr[q * S5_BLOCK_TILES + c] = bu_re[:, c * LANES:(c + 1) * LANES]
            xi_scr[q * S5_BLOCK_TILES + c] = bu_im[:, c * LANES:(c + 1) * LANES]

    for g0 in range(0, n_tiles, S5_SCAN_TILES):
        tiles = range(g0, g0 + S5_SCAN_TILES)
        ar = [jnp.broadcast_to(ar_ref[:, c * LANES:(c + 1) * LANES], (bt, LANES)) for c in tiles]
        ai = [jnp.broadcast_to(ai_ref[:, c * LANES:(c + 1) * LANES], (bt, LANES)) for c in tiles]

        def body(t, carry, tiles=tiles, ar=ar, ai=ai):
            rows = pl.ds(_row_group(t, bt), bt)
            out = []
            for n, c in enumerate(tiles):
                xr, xi = carry[2 * n], carry[2 * n + 1]
                nr = ar[n] * xr - ai[n] * xi + xr_scr[c, rows, :]
                ni = ar[n] * xi + ai[n] * xr + xi_scr[c, rows, :]
                xr_scr[c, rows, :] = nr
                xi_scr[c, rows, :] = ni
                out += [nr, ni]
            return tuple(out)

        init = []
        for c in tiles:
            init += [cr_scr[:, c * LANES:(c + 1) * LANES], ci_scr[:, c * LANES:(c + 1) * LANES]]
        fin = lax.fori_loop(0, tc, body, tuple(init), unroll=2)
        for n, c in enumerate(tiles):
            cr_scr[:, c * LANES:(c + 1) * LANES] = fin[2 * n]
            ci_scr[:, c * LANES:(c + 1) * LANES] = fin[2 * n + 1]
    str_ref[...] = cr_scr[...]
    sti_ref[...] = ci_scr[...]

    blocks = lambda scr: [jnp.concatenate([scr[q * S5_BLOCK_TILES + c] for c in range(S5_BLOCK_TILES)], axis=1)
                          for q in range(S5_BLOCKS)]
    y = _s5_readout(u, blocks(xr_scr), blocks(xi_scr), wcr_ref, wci_ref, d_ref[...], wg_ref[...], bg_ref[...])
    for c in range(in_tiles):
        tm_scr[c] = y[:, c * LANES:(c + 1) * LANES]

    def to_sequence_major(t, carry):
        src = pl.ds(_row_group(t, bt), bt)
        for c in range(in_tiles):
            seq_scr[c, pl.ds(t, bt, stride=pitch), :] = tm_scr[c, src, :]
        return carry

    lax.fori_loop(0, tc, to_sequence_major, 0, unroll=2)
    for s in range(bt):
        o_ref[s] = jnp.concatenate([seq_scr[c, s * pitch:s * pitch + tc, :] for c in range(in_tiles)],
                                   axis=1).astype(o_ref.dtype)


def _s5_step_kernel(x_ref, s0r_ref, s0i_ref, gn_ref, ws_ref, wbr_ref, wbi_ref, wcr_ref, wci_ref, ar_ref, ai_ref,
                    d_ref, wg_ref, bg_ref, o_ref, str_ref, sti_ref):
    u = jnp.dot(_bf(_rms(x_ref[...], gn_ref[...])), ws_ref[...], preferred_element_type=F32)
    ub = _bf(u)
    xr_blocks, xi_blocks = [], []
    for q in range(S5_BLOCKS):
        lanes = slice(q * S5_BLOCK_STATE, (q + 1) * S5_BLOCK_STATE)
        ar, ai = ar_ref[:, lanes], ai_ref[:, lanes]
        sr, si = s0r_ref[:, lanes], s0i_ref[:, lanes]
        bu_re, bu_im = _s5_drive(ub, wbr_ref, wbi_ref, q)
        xr = ar * sr - ai * si + bu_re
        xi = ar * si + ai * sr + bu_im
        str_ref[:, lanes] = xr
        sti_ref[:, lanes] = xi
        xr_blocks.append(xr)
        xi_blocks.append(xi)
    o_ref[...] = _s5_readout(u, xr_blocks, xi_blocks, wcr_ref, wci_ref, d_ref[...], wg_ref[...], bg_ref[...])


def _s5(x, s0r, s0i, layer, weights, seq):
    wspecs = [_layer_const(w, layer) for w in weights]
    if seq:
        bsz, t, d = x.shape
        tc = min(S5_CHUNK, t)
        st = _full((bsz, S5_LANES))
        return pl.pallas_call(
            functools.partial(_s5_seq_kernel, bt=bsz, tc=tc),
            grid=(t // tc,),
            in_specs=[pl.BlockSpec((bsz, tc, d), lambda j: (0, j, 0)), st, st] + wspecs,
            out_specs=[pl.BlockSpec((bsz, tc, WIDTH), lambda j: (0, j, 0)), st, st],
            out_shape=[jax.ShapeDtypeStruct((bsz, t, WIDTH), BRANCH_DTYPE)]
            + [jax.ShapeDtypeStruct((bsz, S5_LANES), F32)] * 2,
            scratch_shapes=[pltpu.VMEM((S5_LANES // LANES, bsz * tc, LANES), F32)] * 2
            + [pltpu.VMEM((bsz, S5_LANES), F32)] * 2
            + [pltpu.VMEM((WIDTH // LANES, bsz * (tc + S5_ROW_PAD), LANES), F32),
               pltpu.VMEM((WIDTH // LANES, bsz * tc, LANES), F32)],
            compiler_params=_cp(("arbitrary",)),
            name="s5_seq",
        )(x, s0r, s0i, *weights)
    n, d = x.shape
    st = _full((n, S5_LANES))
    st_in = pl.BlockSpec((None, n, S5_LANES), lambda i: (layer, 0, 0))
    return pl.pallas_call(
        _s5_step_kernel,
        grid=(1,),
        in_specs=[_full((n, d)), st_in, st_in] + wspecs,
        out_specs=[_full((n, WIDTH)), st, st],
        out_shape=[jax.ShapeDtypeStruct((n, WIDTH), F32)] + [jax.ShapeDtypeStruct((n, S5_LANES), F32)] * 2,
        compiler_params=_cp(("arbitrary",)),
        name="s5_step",
    )(x, s0r, s0i, *weights)


def _mamba_prep_math(xbc_taps, dt_raw, cw, cb, dtb):
    acc = xbc_taps[M_CONV - 1] * cw[0:1]
    for kk in range(1, M_CONV):
        acc = acc + xbc_taps[M_CONV - 1 - kk] * cw[kk:kk + 1]
    acc = acc + cb
    return acc * jax.nn.sigmoid(acc), _softplus(dt_raw + dtb)


def _ssd_post(y, z, nw):
    y = y * (z * jax.nn.sigmoid(z))
    half = WIDTH // M_GROUPS
    parts = []
    for gi in range(M_GROUPS):
        yg = y[:, gi * half:(gi + 1) * half]
        parts.append(yg * lax.rsqrt(jnp.mean(yg * yg, axis=-1, keepdims=True) + M_EPS))
    return jnp.concatenate(parts, axis=1) * nw


def _ssd_chunk_math(xbc, dt, alog_row, acol, dsk, expand, h_scr):
    q = xbc.shape[0]
    pair = 2 * M_HEAD
    xs = xbc[:, :WIDTH]
    row = lax.broadcasted_iota(jnp.int32, (q, q), 0)
    col = lax.broadcasted_iota(jnp.int32, (q, q), 1)
    lower = row >= col
    dt_full = _dot_exact_rhs(dt, expand, terms=3)
    cum = _dot_exact_lhs(_bf(lower.astype(F32)), dt_full * -jnp.exp(alog_row))
    da_t = jnp.transpose(dt[:, :LANES])[:SUBLANES] * -jnp.exp(acol)
    cum_t = _dot_exact_rhs(da_t, _bf((row <= col).astype(F32)), terms=3)
    ecum = jnp.exp(cum)
    cum_last = cum[q - 1:q, :]
    xd = xs * dt_full
    xdec = xd * jnp.exp(cum_last - cum)
    lane = lax.broadcasted_iota(jnp.int32, (q, pair), 1)
    prow = lax.broadcasted_iota(jnp.int32, (pair, M_STATE), 0)
    heads_per_group = M_HEADS // M_GROUPS
    ys = []
    for gi in range(M_GROUPS):
        bg = xbc[:, WIDTH + gi * M_STATE:WIDTH + (gi + 1) * M_STATE]
        cg = xbc[:, WIDTH + (M_GROUPS + gi) * M_STATE:WIDTH + (M_GROUPS + gi + 1) * M_STATE]
        cb = _dot_nt(cg, bg)
        for pp in range(heads_per_group // 2):
            p = gi * (heads_per_group // 2) + pp
            lanes = slice(p * pair, (p + 1) * pair)
            ms = []
            for h in (2 * p, 2 * p + 1):
                ch = cum[:, h * M_HEAD:(h + 1) * M_HEAD]
                seg = jnp.concatenate([ch] * (q // M_HEAD), axis=1) - cum_t[h:h + 1, :]
                ms.append(jnp.where(lower, jnp.exp(seg), 0.0) * cb)
            xd_pair = xd[:, lanes]
            stacked = jnp.concatenate([jnp.where(lane < M_HEAD, xd_pair, 0.0),
                                       jnp.where(lane >= M_HEAD, xd_pair, 0.0)], axis=0)
            hs = h_scr[p]
            ys.append(_dot(jnp.concatenate(ms, axis=1), stacked) + _dot_nt(cg, hs) * ecum[:, lanes])
            keep = jnp.where(prow < M_HEAD, jnp.exp(cum_t[2 * p:2 * p + 1, q - 1:q]),
                             jnp.exp(cum_t[2 * p + 1:2 * p + 2, q - 1:q]))
            h_scr[p] = hs * keep + _dot_tn(xdec[:, lanes], bg)
    return jnp.concatenate(ys, axis=1) + dsk * xs


def _mamba_seq_kernel(x_ref, init_ref, h0_ref, gn_ref, wm_ref, cw_ref, cb_ref, dtb_ref, alog_ref, acol_ref, dsk_ref,
                      nw_ref, expand_ref, o_ref, last_ref, ht_ref, carry, h_scr):
    j = pl.program_id(1)
    n_pairs = M_HEADS // 2

    @pl.when(j == 0)
    def _():
        carry[...] = init_ref[...]
        for p in range(n_pairs):
            h_scr[p] = jnp.concatenate([h0_ref[2 * p], h0_ref[2 * p + 1]], axis=0)

    zm = jnp.dot(_bf(_rms(x_ref[...], gn_ref[...])), wm_ref[...], preferred_element_type=F32)
    raw = zm[:, :M_CONV_CH]
    z = zm[:, M_CONV_CH:M_CONV_CH + WIDTH]
    c8 = carry[...]
    taps = [raw] + [_delayed(raw, c8, kk) for kk in range(1, M_CONV)]
    tail = raw[raw.shape[0] - SUBLANES:, :]
    carry[...] = tail
    last_ref[...] = tail
    xbc, dt = _mamba_prep_math(taps, zm[:, M_CONV_CH + WIDTH:], cw_ref[...], cb_ref[...], dtb_ref[...])
    q = min(SSD_CHUNK, xbc.shape[0])
    y = jnp.concatenate(
        [_ssd_chunk_math(xbc[r0:r0 + q], dt[r0:r0 + q], alog_ref[...], acol_ref[...], dsk_ref[...], expand_ref[...],
                         h_scr) for r0 in range(0, xbc.shape[0], q)], axis=0)
    o_ref[...] = _ssd_post(y, z, nw_ref[...]).astype(o_ref.dtype)

    @pl.when(j == pl.num_programs(1) - 1)
    def _():
        for p in range(n_pairs):
            hs = h_scr[p]
            ht_ref[2 * p] = hs[:M_HEAD]
            ht_ref[2 * p + 1] = hs[M_HEAD:]


def _mamba_seq(x, conv_state, h0, layer, weights):
    bsz, t, d = x.shape
    q = min(SSD_TILE, t)
    tile = lambda c: pl.BlockSpec((None, q, c), lambda b, j: (b, j, 0))
    edge = pl.BlockSpec((None, SUBLANES, M_CONV_CH), lambda b, j: (b, 0, 0))
    st = pl.BlockSpec((None, M_HEADS, M_HEAD, M_STATE), lambda b, j: (b, 0, 0, 0))
    return pl.pallas_call(
        _mamba_seq_kernel,
        grid=(bsz, t // q),
        in_specs=[tile(d), edge, st] + [_layer_const(w, layer) for w in weights],
        out_specs=[tile(WIDTH), edge, st],
        out_shape=[jax.ShapeDtypeStruct((bsz, t, WIDTH), BRANCH_DTYPE),
                   jax.ShapeDtypeStruct((bsz, SUBLANES, M_CONV_CH), F32),
                   jax.ShapeDtypeStruct((bsz, M_HEADS, M_HEAD, M_STATE), F32)],
        scratch_shapes=[pltpu.VMEM((SUBLANES, M_CONV_CH), F32),
                        pltpu.VMEM((M_HEADS // 2, 2 * M_HEAD, M_STATE), F32)],
        compiler_params=_cp(("arbitrary", "arbitrary")),
        name="mamba_seq",
    )(x, conv_state, h0, *weights)


def _mamba_step_kernel(zm_ref, past_ref, h0_ref, *refs, chained):
    (cw_ref, cb_ref, dtb_ref, alog_ref, dsk_ref, nw_ref, expand_ref,
     o_ref, ht_ref, y_scr) = refs[1:] if chained else refs
    bt = zm_ref.shape[0]
    zm = zm_ref[...]
    raw = zm[:, :M_CONV_CH]
    z = zm[:, M_CONV_CH:M_CONV_CH + WIDTH]
    past = past_ref[...]
    taps = [raw] + [past[:, (M_CONV - 1 - kk) * M_CONV_CH:(M_CONV - kk) * M_CONV_CH] for kk in range(1, M_CONV)]
    xbc, dt = _mamba_prep_math(taps, zm[:, M_CONV_CH + WIDTH:], cw_ref[...], cb_ref[...], dtb_ref[...])
    xs = xbc[:, :WIDTH]
    dt_full = _dot_exact_rhs(dt, expand_ref[...], terms=3)
    keep = jnp.exp(dt_full * -jnp.exp(alog_ref[...]))
    xd = xs * dt_full
    heads_per_group = M_HEADS // M_GROUPS
    zero_x = jnp.zeros((1, WIDTH), F32)
    zero_g = jnp.zeros((1, M_GROUPS * M_STATE), F32)
    new = {}
    for i in range(bt):
        one = slice(i, i + 1)
        xrow = _rows(xd[one], zero_x)
        brow = _rows(xbc[one, WIDTH:WIDTH + M_GROUPS * M_STATE], zero_g)
        for h in range(M_HEADS):
            gi = h // heads_per_group
            cs_ = slice(h * M_HEAD, (h + 1) * M_HEAD)
            kp = keep[one, cs_]
            hn = (h0_ref[i, h] * jnp.concatenate([kp, kp], axis=1)
                  + _dot_tn(xrow[:, cs_], brow[:, gi * M_STATE:(gi + 1) * M_STATE]))
            ht_ref[i, h] = hn
            new[i, h] = hn
    for i in range(bt):
        one = slice(i, i + 1)
        crow = _rows(xbc[one, WIDTH + M_GROUPS * M_STATE:], zero_g)
        outs = [_dot_nt(crow[:, (h // heads_per_group) * M_STATE:(h // heads_per_group + 1) * M_STATE], new[i, h])
                for h in range(M_HEADS)]
        y_scr[one, :] = jnp.concatenate(outs, axis=1)[0:1]
    y = y_scr[...] + dsk_ref[...] * xs
    o_ref[...] = _ssd_post(y, z, nw_ref[...])


def _mamba_step(zm, past_all, h_all, h_new, layer, weights):
    n = zm.shape[0]
    bt = SUBLANES
    tile = lambda c: pl.BlockSpec((bt, c), lambda i: (i, 0))
    past = pl.BlockSpec((None, bt, (M_CONV - 1) * M_CONV_CH), lambda i: (layer, i, 0))
    st = pl.BlockSpec((None, bt, M_HEADS, M_HEAD, M_STATE), lambda i: (layer, i, 0, 0, 0))
    chain, chain_specs, aliases = _chain_args(h_new)
    return pl.pallas_call(
        functools.partial(_mamba_step_kernel, chained=bool(chain)),
        grid=(n // bt,),
        in_specs=[tile(M_COLS_PAD), past, st] + chain_specs + [_layer_const(w, layer) for w in weights],
        out_specs=[tile(WIDTH), st],
        out_shape=[jax.ShapeDtypeStruct((n, WIDTH), F32), jax.ShapeDtypeStruct(h_all.shape, F32)],
        scratch_shapes=[pltpu.VMEM((bt, WIDTH), F32)],
        input_output_aliases=aliases,
        compiler_params=_cp(("arbitrary",)),
        name="mamba_step",
    )(zm, past_all, h_all, *chain, *weights)


def _merge_kernel(x_ref, or_ref, os_ref, om_ref, gn_ref, wg_ref, wb_ref, wo_ref, gp_ref, o_ref):
    x = x_ref[...]
    hb = _bf(_rms(x, gn_ref[...]))
    mixed = None
    for kk, ref in enumerate((or_ref, os_ref, om_ref)):
        gate = jax.nn.sigmoid(jnp.dot(hb, wg_ref[:, kk * D_MODEL:(kk + 1) * D_MODEL], preferred_element_type=F32))
        term = gate * jnp.dot(_bf(ref[...]), wb_ref[kk], preferred_element_type=F32)
        mixed = term if mixed is None else mixed + term
    out = jnp.dot(_bf(mixed), wo_ref[...], preferred_element_type=F32)
    o_ref[...] = x + _rms(out, gp_ref[...])


def _merge(x, o_r, o_s, o_m, layer, weights):
    n = x.shape[0]
    tm = min(TOKEN_TILE, n)
    tile = lambda c: pl.BlockSpec((tm, c), lambda i: (i, 0))
    return pl.pallas_call(
        _merge_kernel,
        grid=(n // tm,),
        in_specs=[tile(D_MODEL)] + [tile(WIDTH)] * 3 + [_layer_const(w, layer) for w in weights],
        out_specs=tile(D_MODEL),
        out_shape=jax.ShapeDtypeStruct((n, D_MODEL), F32),
        compiler_params=_cp(("arbitrary",)),
        name="merge",
    )(x, o_r, o_s, o_m, *weights)


def _ffn_chunk(hb, gate_taps_of, cols, wu_ref, cw_ref, cb_ref, wd_ref):
    gate = jnp.dot(hb, wu_ref[:, cols], preferred_element_type=F32)
    val = jnp.dot(hb, wu_ref[:, D_FF + cols.start:D_FF + cols.stop], preferred_element_type=F32)
    taps = gate_taps_of(gate)
    acc = taps[FFN_CONV - 1] * cw_ref[0:1, cols]
    for kk in range(1, FFN_CONV):
        acc = acc + taps[FFN_CONV - 1 - kk] * cw_ref[kk:kk + 1, cols]
    acc = acc + cb_ref[:, cols]
    return jnp.dot(_bf(jax.nn.gelu(acc) * val), wd_ref[cols, :], preferred_element_type=F32), gate


def _ffn_seq_kernel(x_ref, init_ref, gn_ref, wu_ref, cw_ref, cb_ref, wd_ref, gp_ref, o_ref, last_ref, carry):
    @pl.when(pl.program_id(1) == 0)
    def _():
        carry[...] = init_ref[...]

    x = x_ref[...]
    hb = _bf(_rms(x, gn_ref[...]))
    f = None
    for c0, c1 in zip(FFN_COL_STARTS[:-1], FFN_COL_STARTS[1:]):
        cols = slice(c0, c1)
        c8 = carry[:, cols]
        part, gate = _ffn_chunk(hb, lambda g: [g] + [_delayed(g, c8, kk) for kk in range(1, FFN_CONV)], cols,
                                wu_ref, cw_ref, cb_ref, wd_ref)
        tail = gate[gate.shape[0] - SUBLANES:, :]
        carry[:, cols] = tail
        last_ref[:, cols] = tail
        f = part if f is None else f + part
    o_ref[...] = x + _rms(f, gp_ref[...])


def _ffn_step_kernel(x_ref, past_ref, gn_ref, wu_ref, cw_ref, cb_ref, wd_ref, gp_ref, o_ref, gate_ref):
    x = x_ref[...]
    hb = _bf(_rms(x, gn_ref[...]))
    f = None
    for c0, c1 in zip(FFN_COL_STARTS[:-1], FFN_COL_STARTS[1:]):
        cols = slice(c0, c1)
        past = [past_ref[:, (FFN_CONV - 1 - kk) * D_FF + c0:(FFN_CONV - 1 - kk) * D_FF + c1]
                for kk in range(1, FFN_CONV)]
        part, gate = _ffn_chunk(hb, lambda g: [g] + past, cols, wu_ref, cw_ref, cb_ref, wd_ref)
        gate_ref[:, cols] = gate
        f = part if f is None else f + part
    o_ref[...] = x + _rms(f, gp_ref[...])


def _ffn(x, conv_state, layer, weights, seq):
    wspecs = [_layer_const(w, layer) for w in weights]
    if seq:
        bsz, t, d = x.shape
        tm = min(TOKEN_TILE, t)
        tile = pl.BlockSpec((None, tm, d), lambda b, j: (b, j, 0))
        edge = pl.BlockSpec((None, SUBLANES, D_FF), lambda b, j: (b, 0, 0))
        return pl.pallas_call(
            _ffn_seq_kernel,
            grid=(bsz, t // tm),
            in_specs=[tile, edge] + wspecs,
            out_specs=[tile, edge],
            out_shape=[jax.ShapeDtypeStruct((bsz, t, d), F32),
                       jax.ShapeDtypeStruct((bsz, SUBLANES, D_FF), F32)],
            scratch_shapes=[pltpu.VMEM((SUBLANES, D_FF), F32)],
            compiler_params=_cp(("arbitrary", "arbitrary")),
            name="ffn_seq",
        )(x, conv_state, *weights)
    n, d = x.shape
    return pl.pallas_call(
        _ffn_step_kernel,
        grid=(1,),
        in_specs=[_full((n, d)), pl.BlockSpec((None,) + conv_state.shape[1:], lambda i: (layer, 0, 0))] + wspecs,
        out_specs=[_full((n, d)), _full((n, D_FF))],
        out_shape=[jax.ShapeDtypeStruct((n, d), F32), jax.ShapeDtypeStruct((n, D_FF), F32)],
        compiler_params=_cp(("arbitrary",)),
        name="ffn_step",
    )(x, conv_state, *weights)


def _row(v):
    return v.reshape(v.shape[0], 1, -1).astype(F32)


def _all_layer_weights(params):
    (g_pre_mix, g_post_mix, g_pre_ffn, g_post_ffn, w_in,
     r_mu, r_w0, r_w2, r_a0, r_a2, r_g2, r_kk, r_ka, r_rk, r_ln_w, r_ln_b,
     s5_lam_re, s5_lam_im, s5_log_step, s5_b_re, s5_b_im, s5_c_re, s5_c_im, s5_d, s5_w_glu, s5_b_glu,
     m_conv_w, m_conv_b, m_dt_bias, m_a_log, m_d, m_norm_w,
     w_branch, w_out, w_up, f_conv_w, f_conv_b, w_down) = params
    depth = w_in.shape[0]
    c0 = R_COLS
    c1 = c0 + WIDTH
    c2 = c1 + WIDTH + M_CONV_CH
    c3 = c2 + M_HEADS
    every_layer = lambda m: jnp.broadcast_to(m[None], (depth,) + m.shape)
    head = jnp.arange(WIDTH) // R_HEAD
    ones_bd = every_layer((head[:, None] == head[None, :]).astype(BF16))
    gn = _row(g_pre_mix)
    w = {"g_pre_mix": gn}
    w_r = _bf(w_in[:, :, :c0])
    w_m = _bf(jnp.concatenate(
        [w_in[:, :, c1 + WIDTH:c2], w_in[:, :, c1:c1 + WIDTH], w_in[:, :, c2:c3],
         jnp.zeros((depth, D_MODEL, M_DT_PAD - M_HEADS), F32)], axis=2))
    w["w_r"], w["w_m"] = w_r, w_m
    rwkv_mix = (_row(r_mu), _row(r_w0), _bf(r_w2), _row(r_a0), _bf(r_a2), _bf(r_g2), _row(r_kk), _row(r_ka),
                ones_bd, _row(r_rk), _row(r_ln_w), _row(r_ln_b))
    w["rwkv_seq"] = (gn, w_r) + rwkv_mix
    w["rwkv_step"] = rwkv_mix
    ab_re, ab_im, bb_re, bb_im = _s5_params(s5_lam_re, s5_lam_im, s5_log_step, s5_b_re, s5_b_im)
    gpb = S5_GROUPS // S5_BLOCKS
    eye = jnp.eye(gpb, dtype=F32)
    blocked = lambda m: m.reshape(depth, S5_BLOCKS, gpb, S5_GROUP, S5_STATE)
    to_state = lambda bb: _bf(jnp.einsum('lqghp,gk->lqghkp', blocked(bb), eye)
                              .reshape(depth, S5_BLOCKS, S5_BLOCK_IN, S5_BLOCK_STATE))
    from_state = lambda c: _bf(jnp.einsum('lqghp,gk->lqgpkh', blocked(c), eye)
                               .reshape(depth, S5_BLOCKS, S5_BLOCK_STATE, S5_BLOCK_IN))
    w["s5"] = (gn, _bf(w_in[:, :, c0:c1]), to_state(bb_re), to_state(bb_im), from_state(s5_c_re),
               from_state(s5_c_im), ab_re.reshape(depth, 1, S5_LANES), ab_im.reshape(depth, 1, S5_LANES), _row(s5_d),
               _bf(s5_w_glu), _row(s5_b_glu))
    conv = (jnp.swapaxes(m_conv_w, 1, 2).astype(F32), _row(m_conv_b),
            _row(jnp.pad(m_dt_bias, ((0, 0), (0, M_DT_PAD - M_HEADS)))))
    alog_row = _row(jnp.repeat(m_a_log, M_HEAD, axis=1))
    expand = every_layer((jnp.arange(M_DT_PAD)[:, None] == (jnp.arange(WIDTH) // M_HEAD)[None, :]).astype(BF16))
    tail = (_row(jnp.repeat(m_d, M_HEAD, axis=1)), _row(m_norm_w), expand)
    w["mamba_seq"] = (gn, w_m) + conv + (alog_row, m_a_log.reshape(depth, M_HEADS, 1).astype(F32)) + tail
    w["mamba_step"] = conv + (alog_row,) + tail
    w["merge"] = (gn, _bf(w_in[:, :, c3:]), _bf(w_branch), _bf(w_out), _row(g_post_mix))
    w["ffn"] = (_row(g_pre_ffn), _bf(w_up), jnp.swapaxes(f_conv_w, 1, 2).astype(F32), _row(f_conv_b), _bf(w_down),
                _row(g_post_ffn))
    return w


def _zero_states(n):
    return dict(shift=jnp.zeros((n, SUBLANES, R_COLS), F32),
                wkv=jnp.zeros((n, R_HEADS, R_HEAD, R_HEAD), F32),
                s5=jnp.zeros((n, S5_LANES), F32),
                ssd=jnp.zeros((n, M_HEADS, M_HEAD, M_STATE), F32),
                mconv=jnp.zeros((n, SUBLANES, M_CONV_CH), F32),
                fconv=jnp.zeros((n, SUBLANES, D_FF), F32))


def _sequence_layer(x, init, w, layer):
    bsz, t, d = x.shape
    flat = lambda a: a.reshape(bsz * t, a.shape[-1])
    o_r, last_r, wkv1 = _rwkv_seq(x, init["shift"], init["wkv"], layer, w["rwkv_seq"])
    o_s, s5r1, s5i1 = _s5(x, init["s5"], init["s5"], layer, w["s5"], True)
    o_m, last_m, ssd1 = _mamba_seq(x, init["mconv"], init["ssd"], layer, w["mamba_seq"])
    mixed = _merge(flat(x), flat(o_r), flat(o_s), flat(o_m), layer, w["merge"])
    x, last_f = _ffn(mixed.reshape(bsz, t, d), init["fconv"], layer, w["ffn"], True)
    states = (last_r[:, SUBLANES - 1], wkv1, s5r1.reshape(bsz, S5_GROUPS, S5_STATE),
              s5i1.reshape(bsz, S5_GROUPS, S5_STATE), ssd1, last_m[:, SUBLANES - (M_CONV - 1):],
              last_f[:, SUBLANES - (FFN_CONV - 1):])
    return x, states


def _step_layer(x, cache, chained, w, layer):
    shift_all, wkv_all, s5r_all, s5i_all, ssd_all, mconv_all, fconv_all = cache
    z_r = _norm_matmul(x, w["g_pre_mix"], w["w_r"], layer, "in_proj_rwkv")
    z_m = _norm_matmul(x, w["g_pre_mix"], w["w_m"], layer, "in_proj_mamba")
    o_r, wkv_new = _rwkv_step(z_r, shift_all, wkv_all, chained[0], layer, w["rwkv_step"])
    o_s, s5r1, s5i1 = _s5(x, s5r_all, s5i_all, layer, w["s5"], False)
    o_m, ssd_new = _mamba_step(z_m, mconv_all, ssd_all, chained[1], layer, w["mamba_step"])
    x = _merge(x, o_r, o_s, o_m, layer, w["merge"])
    x, gate = _ffn(x, fconv_all, layer, w["ffn"], False)
    return x, (wkv_new, ssd_new), (z_r, s5r1, s5i1, z_m[:, :M_CONV_CH], gate)


def kernel(x_prompt, x_sample, state_rwkv_shift, state_rwkv_wkv, state_s5_re, state_s5_im, state_ssd, state_ssd_conv, state_ffn_conv, g_pre_mix, g_post_mix, g_pre_ffn, g_post_ffn, w_in, r_mu, r_w0, r_w2, r_a0, r_a2, r_g2, r_kk, r_ka, r_rk, r_ln_w, r_ln_b, s5_lam_re, s5_lam_im, s5_log_step, s5_b_re, s5_b_im, s5_c_re, s5_c_im, s5_d, s5_w_glu, s5_b_glu, m_conv_w, m_conv_b, m_dt_bias, m_a_log, m_d, m_norm_w, w_branch, w_out, w_up, f_conv_w, f_conv_b, w_down):
    stacked = (g_pre_mix, g_post_mix, g_pre_ffn, g_post_ffn, w_in,
               r_mu, r_w0, r_w2, r_a0, r_a2, r_g2, r_kk, r_ka, r_rk, r_ln_w, r_ln_b,
               s5_lam_re, s5_lam_im, s5_log_step, s5_b_re, s5_b_im, s5_c_re, s5_c_im, s5_d,
               s5_w_glu, s5_b_glu,
               m_conv_w, m_conv_b, m_dt_bias, m_a_log, m_d, m_norm_w,
               w_branch, w_out, w_up, f_conv_w, f_conv_b, w_down)
    depth = w_in.shape[0]
    pb, pt, d = x_prompt.shape
    sb, s_t, _ = x_sample.shape
    w = _all_layer_weights(stacked)
    per_seq = lambda a: a.reshape(depth, sb, -1)
    cache = (state_rwkv_shift, state_rwkv_wkv, per_seq(state_s5_re), per_seq(state_s5_im), state_ssd,
             per_seq(state_ssd_conv), per_seq(state_ffn_conv))
    init = _zero_states(pb)
    xp = x_prompt
    xs = x_sample.reshape(sb * s_t, d)
    chained = (None, None)
    p_states, s_rows = [], []
    for l in range(depth):
        xp, sp = _sequence_layer(xp, init, w, l)
        xs, chained, rows = _step_layer(xs, cache, chained, w, l)
        p_states.append(sp)
        s_rows.append(rows)
    p_shift, p_wkv, p_s5_re, p_s5_im, p_ssd, p_ssd_conv, p_ffn_conv = (jnp.stack(v, 0) for v in zip(*p_states))
    z_r, s5r, s5i, raw_xbc, gate = (jnp.stack(v, 0) for v in zip(*s_rows))
    s_wkv, s_ssd = chained
    s_s5_re = s5r.reshape(depth, sb, S5_GROUPS, S5_STATE)
    s_s5_im = s5i.reshape(depth, sb, S5_GROUPS, S5_STATE)
    s_ssd_conv = jnp.concatenate([state_ssd_conv[:, :, 1:], raw_xbc[:, :, None]], axis=2)
    s_ffn_conv = jnp.concatenate([state_ffn_conv[:, :, 1:], gate[:, :, None]], axis=2)
    return (xp, xs.reshape(sb, s_t, d), p_shift, z_r, p_wkv, s_wkv, p_s5_re, s_s5_re, p_s5_im, s_s5_im,
            p_ssd, s_ssd, p_ssd_conv, s_ssd_conv, p_ffn_conv, s_ffn_conv)
```

```python
import functools

import jax
import jax.numpy as jnp
from jax import lax
from jax.experimental import pallas as pl
from jax.experimental.pallas import tpu as pltpu

F32 = jnp.float32
BF16 = jnp.bfloat16

D_MODEL = 1024
WIDTH = 512
R_HEADS, R_HEAD = 8, 64
R_COLS = 1792
R_LN_EPS = 64e-5
DECAY_SCALE = 0.6065306597126334
S5_GROUPS, S5_GROUP, S5_STATE = 32, 16, 64
S5_LANES = S5_GROUPS * S5_STATE
M_HEADS, M_HEAD, M_GROUPS, M_STATE = 8, 64, 2, 128
M_CONV, M_CONV_CH = 4, 1024
M_DT_PAD = 256
M_COLS_PAD = M_CONV_CH + WIDTH + M_DT_PAD
M_EPS = 1e-5
D_FF = 2816
FFN_CONV = 3
EPS = 1e-6
SUBLANES = 8
LANES = 128

RWKV_CHUNK = 64
RWKV_TILE = 256
SSD_CHUNK = 128
S5_CHUNK = 128
S5_ROW_PAD = 8
S5_SCAN_TILES = 8
SSD_TILE = 256
BRANCH_DTYPE = BF16
TOKEN_TILE = 512
FFN_COL_STARTS = (0, 1536, D_FF)
VMEM_LIMIT = 56 * 1024 * 1024


def _cp(sem):
    return pltpu.CompilerParams(dimension_semantics=sem, vmem_limit_bytes=VMEM_LIMIT)


def _layer_const(w, layer):
    shape = w.shape[1:]
    nd = len(shape)
    return pl.BlockSpec((None,) + shape, lambda *_: (layer,) + (0,) * nd, pipeline_mode=pl.Buffered(1))


def _full(shape):
    nd = len(shape)
    return pl.BlockSpec(shape, lambda *_: (0,) * nd)


def _bf(x):
    return x.astype(BF16)


def _dot(a, b):
    return jnp.dot(_bf(a), _bf(b), preferred_element_type=F32)


def _dot_nt(a, b):
    return lax.dot_general(_bf(a), _bf(b), (((1,), (1,)), ((), ())), preferred_element_type=F32)


def _dot_tn(a, b):
    return lax.dot_general(_bf(a), _bf(b), (((0,), (0,)), ((), ())), preferred_element_type=F32)


def _split(x, terms):
    out = []
    for _ in range(terms - 1):
        h = _bf(x)
        out.append(h)
        x = x - h.astype(F32)
    out.append(_bf(x))
    return out


def _dot_exact_lhs(m_bf16, x, terms=3):
    acc = None
    for h in _split(x, terms):
        p = jnp.dot(m_bf16, h, preferred_element_type=F32)
        acc = p if acc is None else acc + p
    return acc


def _dot_exact_rhs(x, m_bf16, terms=2):
    acc = None
    for h in _split(x, terms):
        p = jnp.dot(h, m_bf16, preferred_element_type=F32)
        acc = p if acc is None else acc + p
    return acc


def _softplus(x):
    return jnp.maximum(x, 0.0) + jnp.log1p(jnp.exp(-jnp.abs(x)))


def _rms(x, g, eps=EPS):
    return x * lax.rsqrt(jnp.mean(x * x, axis=-1, keepdims=True) + eps) * g


def _delayed(x, carry8, k):
    rx = pltpu.roll(x, k, 0)
    rc = pltpu.roll(carry8, k, 0)
    row = lax.broadcasted_iota(jnp.int32, (SUBLANES, x.shape[1]), 0)
    head = jnp.where(row < k, rc, rx[:SUBLANES])
    if x.shape[0] == SUBLANES:
        return head
    return jnp.concatenate([head, rx[SUBLANES:]], axis=0)


def _row_pairs(x, *parts):
    hi = _bf(x).astype(F32)
    part = {"hi": hi, "lo": x - hi}
    row = lax.broadcasted_iota(jnp.int32, x.shape, 0)
    out = jnp.zeros_like(x)
    for n, name in enumerate(parts):
        src = part[name] if n == 0 else pltpu.roll(part[name], 2 * n, 0)
        out = jnp.where((row >= 2 * n) & (row < 2 * n + 2), src, out)
    return _bf(out)


def _row_group(t, rows):
    start = t * rows
    return pl.multiple_of(start, SUBLANES) if rows % SUBLANES == 0 else start


def _rows(first, second):
    row = lax.broadcasted_iota(jnp.int32, (SUBLANES, first.shape[1]), 0)
    return jnp.where(row == 0, first, jnp.where(row == 1, second, 0.0))


def _norm_matmul_kernel(x_ref, g_ref, w_ref, o_ref):
    o_ref[...] = jnp.dot(_bf(_rms(x_ref[...], g_ref[...])), w_ref[...], preferred_element_type=F32)


def _norm_matmul(x, g, w, layer, name):
    n, d = x.shape
    c = w.shape[-1]
    return pl.pallas_call(
        _norm_matmul_kernel,
        grid=(1,),
        in_specs=[_full((n, d)), _layer_const(g, layer), _layer_const(w, layer)],
        out_specs=_full((n, c)),
        out_shape=jax.ShapeDtypeStruct((n, c), F32),
        compiler_params=_cp(("arbitrary",)),
        name=name,
    )(x, g, w)


def _rwkv_prep_math(z, prev, mu, w0, w2, a0, a2, g2, kkw, kaw, ones_bd):
    zm = z + (prev - z) * mu
    r = zm[:, 0:WIDTH]
    k = zm[:, WIDTH:2 * WIDTH]
    v = zm[:, 2 * WIDTH:3 * WIDTH]
    dw = zm[:, 1536:1600]
    da = zm[:, 1600:1664]
    dg = zm[:, 1664:1792]
    ld = -DECAY_SCALE * jax.nn.sigmoid(w0 + _dot(jnp.tanh(dw), w2))
    a = jax.nn.sigmoid(a0 + _dot(da, a2))
    g = _dot(jax.nn.sigmoid(dg), g2)
    kk = k * kkw
    ss = _dot_exact_rhs(kk * kk, ones_bd)
    kk = kk * lax.rsqrt(jnp.maximum(ss, 1e-24))
    k2 = k * (1.0 + (a - 1.0) * kaw)
    return r, ld, k2, v, kk, kk * a, g


def _rwkv_post(y, r, k, v, g, rk, lnw, lnb, ones_bd):
    inv = 1.0 / R_HEAD
    mean = _dot_exact_rhs(y, ones_bd) * inv
    d = y - mean
    var = _dot_exact_rhs(d * d, ones_bd) * inv
    yn = d * lax.rsqrt(var + R_LN_EPS) * lnw + lnb
    bonus = _dot_exact_rhs(r * k * rk, ones_bd) * v
    return (yn + bonus) * g


def _rwkv_chunks(r, ld, k, v, kk, b, s_scr):
    L = RWKV_CHUNK
    pair = 2 * R_HEAD
    n_pairs = R_HEADS // 2
    tm = r.shape[0]
    n_chunks = tm // L
    trow = lax.broadcasted_iota(jnp.int32, (tm, tm), 0)
    tcol = lax.broadcasted_iota(jnp.int32, (tm, tm), 1)
    same_chunk = (trow // L) == (tcol // L)
    cum = _dot_exact_lhs(_bf(((trow >= tcol) & same_chunk).astype(F32)), ld)
    wc = jnp.exp(cum)
    winv = jnp.exp(-cum)
    r_t = r * wc
    kk_t = kk * jnp.exp(cum - ld)
    k_h = k * winv
    b_h = b * winv

    row = lax.broadcasted_iota(jnp.int32, (pair, pair), 0)
    col = lax.broadcasted_iota(jnp.int32, (pair, pair), 1)
    same_head = (row // R_HEAD) == (col // R_HEAD)
    lrow = lax.broadcasted_iota(jnp.int32, (L, pair), 0)
    lcol = lax.broadcasted_iota(jnp.int32, (L, pair), 1) % R_HEAD
    strict = lrow > lcol
    lower = lrow >= lcol

    def bd(x):
        xb = _bf(x)
        return jnp.where(same_head, jnp.concatenate([xb, xb], axis=0), jnp.zeros((), BF16))

    units = [(c, p) for c in range(n_chunks) for p in range(n_pairs)]
    pre = {}
    for c, p in units:
        rows = slice(c * L, (c + 1) * L)
        lanes = slice(p * pair, (p + 1) * pair)
        wl = wc[(c + 1) * L - 1:(c + 1) * L, lanes]
        pre[c, p] = dict(kkt=kk_t[rows, lanes], rt=r_t[rows, lanes], kh=k_h[rows, lanes], bh=b_h[rows, lanes],
                         v=v[rows, lanes], vbd=bd(v[rows, lanes]), wl=wl)
    for u in units:
        d = pre[u]
        a = _dot_nt(jnp.concatenate([d["kkt"], d["rt"]], axis=0),
                    jnp.concatenate([bd(d["kh"]), bd(d["bh"])], axis=0))
        d["akk_k"] = jnp.where(strict, a[:L, :pair], 0.0)
        d["n"] = jnp.where(strict, a[:L, pair:], 0.0)
        d["ar_k"] = jnp.where(lower, a[L:, :pair], 0.0)
        d["ar_b"] = jnp.where(lower, a[L:, pair:], 0.0)
        d["q"] = -d["n"]
        d["m"] = d["n"]
    power = 2
    while power < L:
        for u in units:
            d = pre[u]
            d["m"] = _dot(d["m"], bd(d["m"]))
            d["q"] = d["q"] + d["m"] + _dot(d["q"], bd(d["m"]))
        power *= 2
    for u in units:
        d = pre[u]
        xy = _dot(jnp.concatenate([d["akk_k"], d["ar_k"]], axis=0), d["vbd"])
        x, d["y0"] = xy[:L], xy[L:]
        both = jnp.concatenate([d["kkt"], x], axis=1)
        both = both + _dot(d["q"], jnp.concatenate([bd(d["kkt"]), bd(x)], axis=1))
        d["g"], d["u0"] = both[:, :pair], both[:, pair:]
    for u in units:
        d = pre[u]
        t = _dot(d["ar_b"], jnp.concatenate([bd(d["g"]), bd(d["u0"])], axis=1))
        d["ry"] = d["rt"] - t[:, :pair]
        d["y0"] = d["y0"] - t[:, pair:]
        kw = d["kh"] * d["wl"]
        bw = d["bh"] * d["wl"]
        d["pm"] = jnp.where(same_head, _dot_tn(d["g"], bw), 0.0)
        d["c"] = jnp.where(same_head, _dot_tn(jnp.concatenate([d["v"], -d["u0"]], axis=0),
                                               jnp.concatenate([kw, bw], axis=0)), 0.0)
    ys = []
    for c in range(n_chunks):
        parts = []
        for p in range(n_pairs):
            d = pre[c, p]
            s = s_scr[p]
            parts.append(_dot_nt(d["ry"], s) + d["y0"])
            s_scr[p] = s * d["wl"] - _dot(s, d["pm"]) + d["c"]
        ys.append(jnp.concatenate(parts, axis=1))
    return jnp.concatenate(ys, axis=0)


def _rwkv_seq_kernel(x_ref, init_ref, s0_ref, gn_ref, wr_ref, mu_ref, w0_ref, w2_ref, a0_ref, a2_ref, g2_ref,
                     kkw_ref, kaw_ref, ones_ref, rk_ref, lnw_ref, lnb_ref, o_ref, last_ref, st_ref, carry, s_scr):
    j = pl.program_id(1)
    n_pairs = R_HEADS // 2

    @pl.when(j == 0)
    def _():
        carry[...] = init_ref[...]
        zero = jnp.zeros((R_HEAD, R_HEAD), F32)
        for p in range(n_pairs):
            top = jnp.concatenate([s0_ref[2 * p], zero], axis=1)
            bot = jnp.concatenate([zero, s0_ref[2 * p + 1]], axis=1)
            s_scr[p] = jnp.concatenate([top, bot], axis=0)

    z = jnp.dot(_bf(_rms(x_ref[...], gn_ref[...])), wr_ref[...], preferred_element_type=F32)
    prev = _delayed(z, carry[...], 1)
    tail = z[z.shape[0] - SUBLANES:, :]
    carry[...] = tail
    last_ref[...] = tail
    ones_bd = ones_ref[...]
    r, ld, k, v, kk, b, g = _rwkv_prep_math(z, prev, mu_ref[...], w0_ref[...], w2_ref[...], a0_ref[...],
                                            a2_ref[...], g2_ref[...], kkw_ref[...], kaw_ref[...], ones_bd)
    y = _rwkv_chunks(r, ld, k, v, kk, b, s_scr)
    o_ref[...] = _rwkv_post(y, r, k, v, g, rk_ref[...], lnw_ref[...], lnb_ref[...], ones_bd).astype(o_ref.dtype)

    @pl.when(j == pl.num_programs(1) - 1)
    def _():
        for p in range(n_pairs):
            s = s_scr[p]
            st_ref[2 * p] = s[:R_HEAD, :R_HEAD]
            st_ref[2 * p + 1] = s[R_HEAD:, R_HEAD:]


def _rwkv_seq(x, shift, s0, layer, weights):
    bsz, t, d = x.shape
    tm = min(RWKV_TILE, t)
    tile = lambda c: pl.BlockSpec((None, tm, c), lambda b, j: (b, j, 0))
    edge = pl.BlockSpec((None, SUBLANES, R_COLS), lambda b, j: (b, 0, 0))
    st = pl.BlockSpec((None, R_HEADS, R_HEAD, R_HEAD), lambda b, j: (b, 0, 0, 0))
    return pl.pallas_call(
        _rwkv_seq_kernel,
        grid=(bsz, t // tm),
        in_specs=[tile(d), edge, st] + [_layer_const(w, layer) for w in weights],
        out_specs=[tile(WIDTH), edge, st],
        out_shape=[jax.ShapeDtypeStruct((bsz, t, WIDTH), BRANCH_DTYPE),
                   jax.ShapeDtypeStruct((bsz, SUBLANES, R_COLS), F32),
                   jax.ShapeDtypeStruct((bsz, R_HEADS, R_HEAD, R_HEAD), F32)],
        scratch_shapes=[pltpu.VMEM((SUBLANES, R_COLS), F32),
                        pltpu.VMEM((R_HEADS // 2, 2 * R_HEAD, 2 * R_HEAD), F32)],
        compiler_params=_cp(("arbitrary", "arbitrary")),
        name="rwkv_seq",
    )(x, shift, s0, *weights)


def _rwkv_step_kernel(z_ref, prev_ref, s0_ref, *refs, chained):
    (mu_ref, w0_ref, w2_ref, a0_ref, a2_ref, g2_ref, kkw_ref, kaw_ref, ones_ref, rk_ref, lnw_ref, lnb_ref,
     o_ref, st_ref, y_scr) = refs[1:] if chained else refs
    bt = z_ref.shape[0]
    ones_bd = ones_ref[...]
    r, ld, k, v, kk, b, g = _rwkv_prep_math(z_ref[...], prev_ref[...], mu_ref[...], w0_ref[...], w2_ref[...],
                                            a0_ref[...], a2_ref[...], g2_ref[...], kkw_ref[...], kaw_ref[...],
                                            ones_bd)
    w = jnp.exp(ld)
    wr = w * r
    b_dot_r = _dot_exact_rhs(b * r, ones_bd)
    k_dot_r = _dot_exact_rhs(k * r, ones_bd)

    projs = []
    for i in range(bt):
        one = slice(i, i + 1)
        lhs = _rows(kk[one], wr[one])
        projs.append(jnp.concatenate(
            [_dot_nt(lhs[:, h * R_HEAD:(h + 1) * R_HEAD], s0_ref[i, h]) for h in range(R_HEADS)], axis=1))
    pieces = []
    for i in range(bt):
        one = slice(i, i + 1)
        sa = projs[i][0:1]
        y_scr[one, :] = projs[i][1:2] - sa * b_dot_r[one] + v[one] * k_dot_r[one]
        pieces.append((_row_pairs(_rows(v[one], -sa), "hi", "hi", "lo"), _row_pairs(_rows(k[one], b[one]), "hi", "lo", "hi")))
    for i in range(bt):
        one = slice(i, i + 1)
        left, right = pieces[i]
        for h in range(R_HEADS):
            cs = slice(h * R_HEAD, (h + 1) * R_HEAD)
            upd = lax.dot_general(left[:, cs], right[:, cs], (((0,), (0,)), ((), ())), preferred_element_type=F32)
            st_ref[i, h] = s0_ref[i, h] * w[one, cs] + upd
    o_ref[...] = _rwkv_post(y_scr[...], r, k, v, g, rk_ref[...], lnw_ref[...], lnb_ref[...], ones_bd)


def _chain_args(result):
    if result is None:
        return [], [], {}
    return [result], [pl.BlockSpec(memory_space=pl.ANY)], {3: 1}


def _rwkv_step(z, shift_all, s_all, s_new, layer, weights):
    n = z.shape[0]
    bt = SUBLANES
    tile = lambda c: pl.BlockSpec((bt, c), lambda i: (i, 0))
    prev = pl.BlockSpec((None, bt, R_COLS), lambda i: (layer, i, 0))
    st = pl.BlockSpec((None, bt, R_HEADS, R_HEAD, R_HEAD), lambda i: (layer, i, 0, 0, 0))
    chain, chain_specs, aliases = _chain_args(s_new)
    return pl.pallas_call(
        functools.partial(_rwkv_step_kernel, chained=bool(chain)),
        grid=(n // bt,),
        in_specs=[tile(R_COLS), prev, st] + chain_specs + [_layer_const(w, layer) for w in weights],
        out_specs=[tile(WIDTH), st],
        out_shape=[jax.ShapeDtypeStruct((n, WIDTH), F32), jax.ShapeDtypeStruct(s_all.shape, F32)],
        scratch_shapes=[pltpu.VMEM((bt, WIDTH), F32)],
        input_output_aliases=aliases,
        compiler_params=_cp(("arbitrary",)),
        name="rwkv_step",
    )(z, shift_all, s_all, *chain, *weights)


S5_BLOCKS = 4
S5_BLOCK_IN = WIDTH // S5_BLOCKS
S5_BLOCK_STATE = S5_LANES // S5_BLOCKS
S5_BLOCK_TILES = S5_BLOCK_STATE // LANES


def _s5_param_kernel(lr_ref, li_ref, ls_ref, br_ref, bi_ref, abr_ref, abi_ref, bbr_ref, bbi_ref):
    lr, li = lr_ref[...], li_ref[...]
    delta = jnp.exp(ls_ref[...])
    mag = jnp.exp(lr * delta)
    ab_re = mag * jnp.cos(li * delta)
    ab_im = mag * jnp.sin(li * delta)
    den = lr * lr + li * li
    nr, ni = ab_re - 1.0, ab_im
    cf_re = (nr * lr + ni * li) / den
    cf_im = (ni * lr - nr * li) / den
    abr_ref[...] = ab_re
    abi_ref[...] = ab_im
    br, bi = br_ref[...], bi_ref[...]
    bbr_ref[...] = cf_re[:, None, :] * br - cf_im[:, None, :] * bi
    bbi_ref[...] = cf_re[:, None, :] * bi + cf_im[:, None, :] * br


def _s5_params(lam_re, lam_im, log_step, b_re, b_im):
    depth, g, p, h = b_re.shape
    gp = jax.ShapeDtypeStruct((depth, g, p), F32)
    ghp = jax.ShapeDtypeStruct((depth, g, h, p), F32)
    args = (lam_re, lam_im, log_step.reshape(depth, g, 1), jnp.swapaxes(b_re, 2, 3), jnp.swapaxes(b_im, 2, 3))
    per_layer = lambda shape: pl.BlockSpec((None,) + shape, lambda l: (l,) + (0,) * len(shape))
    return pl.pallas_call(
        _s5_param_kernel,
        grid=(depth,),
        in_specs=[per_layer(a.shape[1:]) for a in args],
        out_specs=[per_layer((g, p)), per_layer((g, p)), per_layer((g, h, p)), per_layer((g, h, p))],
        out_shape=[gp, gp, ghp, ghp],
        compiler_params=_cp(("arbitrary",)),
        name="s5_params",
    )(*args)


def _s5_drive(ub, wbr_ref, wbi_ref, blk):
    cols = slice(blk * S5_BLOCK_IN, (blk + 1) * S5_BLOCK_IN)
    return (jnp.dot(ub[:, cols], wbr_ref[blk], preferred_element_type=F32),
            jnp.dot(ub[:, cols], wbi_ref[blk], preferred_element_type=F32))


def _s5_readout(u, xr_blocks, xi_blocks, wcr_ref, wci_ref, d, wg, bg):
    y = jnp.concatenate([_dot(xr_blocks[q], wcr_ref[q]) - _dot(xi_blocks[q], wci_ref[q])
                         for q in range(S5_BLOCKS)], axis=1) + d * u
    y = jax.nn.gelu(y)
    return y * jax.nn.sigmoid(_dot(y, wg) + bg)


def _s5_seq_kernel(x_ref, s0r_ref, s0i_ref, gn_ref, ws_ref, wbr_ref, wbi_ref, wcr_ref, wci_ref, ar_ref, ai_ref,
                   d_ref, wg_ref, bg_ref, o_ref, str_ref, sti_ref, xr_scr, xi_scr, cr_scr, ci_scr, seq_scr, tm_scr,
                   *, bt, tc):
    @pl.when(pl.program_id(0) == 0)
    def _():
        cr_scr[...] = s0r_ref[...]
        ci_scr[...] = s0i_ref[...]

    pitch = tc + S5_ROW_PAD
    in_tiles = WIDTH // LANES
    n_tiles = S5_LANES // LANES
    x = x_ref[...].reshape(bt * tc, D_MODEL)
    u_seq = jnp.dot(_bf(_rms(x, gn_ref[...])), ws_ref[...], preferred_element_type=F32)
    for c in range(in_tiles):
        for s in range(bt):
            seq_scr[c, s * pitch:s * pitch + tc, :] = u_seq[s * tc:(s + 1) * tc, c * LANES:(c + 1) * LANES]

    def to_time_major(t, carry):
        dst = pl.ds(_row_group(t, bt), bt)
        for c in range(in_tiles):
            tm_scr[c, dst, :] = seq_scr[c, pl.ds(t, bt, stride=pitch), :]
        return carry

    lax.fori_loop(0, tc, to_time_major, 0, unroll=2)
    u = jnp.concatenate([tm_scr[c] for c in range(in_tiles)], axis=1)
    ub = _bf(u)
    for q in range(S5_BLOCKS):
        bu_re, bu_im = _s5_drive(ub, wbr_ref, wbi_ref, q)
        for c in range(S5_BLOCK_TILES):
            xr_scr[q * S5_BLOCK_TILES + c] = bu_re[:, c * LANES:(c + 1) * LANES]
            xi_scr[q * S5_BLOCK_TILES + c] = bu_im[:, c * LANES:(c + 1) * LANES]

    for g0 in range(0, n_tiles, S5_SCAN_TILES):
        tiles = range(g0, g0 + S5_SCAN_TILES)
        ar = [jnp.broadcast_to(ar_ref[:, c * LANES:(c + 1) * LANES], (bt, LANES)) for c in tiles]
        ai = [jnp.broadcast_to(ai_ref[:, c * LANES:(c + 1) * LANES], (bt, LANES)) for c in tiles]

        def body(t, carry, tiles=tiles, ar=ar, ai=ai):
            rows = pl.ds(_row_group(t, bt), bt)
            out = []
            for n, c in enumerate(tiles):
                xr, xi = carry[2 * n], carry[2 * n + 1]
                nr = ar[n] * xr - ai[n] * xi + xr_scr[c, rows, :]
                ni = ar[n] * xi + ai[n] * xr + xi_scr[c, rows, :]
                xr_scr[c, rows, :] = nr
                xi_scr[c, rows, :] = ni
                out += [nr, ni]
            return tuple(out)

        init = []
        for c in tiles:
            init += [cr_scr[:, c * LANES:(c + 1) * LANES], ci_scr[:, c * LANES:(c + 1) * LANES]]
        fin = lax.fori_loop(0, tc, body, tuple(init), unroll=2)
        for n, c in enumerate(tiles):
            cr_scr[:, c * LANES:(c + 1) * LANES] = fin[2 * n]
            ci_scr[:, c * LANES:(c + 1) * LANES] = fin[2 * n + 1]
    str_ref[...] = cr_scr[...]
    sti_ref[...] = ci_scr[...]

    blocks = lambda scr: [jnp.concatenate([scr[q * S5_BLOCK_TILES + c] for c in range(S5_BLOCK_TILES)], axis=1)
                          for q in range(S5_BLOCKS)]
    y = _s5_readout(u, blocks(xr_scr), blocks(xi_scr), wcr_ref, wci_ref, d_ref[...], wg_ref[...], bg_ref[...])
    for c in range(in_tiles):
        tm_scr[c] = y[:, c * LANES:(c + 1) * LANES]

    def to_sequence_major(t, carry):
        src = pl.ds(_row_group(t, bt), bt)
        for c in range(in_tiles):
            seq_scr[c, pl.ds(t, bt, stride=pitch), :] = tm_scr[c, src, :]
        return carry

    lax.fori_loop(0, tc, to_sequence_major, 0, unroll=2)
    for s in range(bt):
        o_ref[s] = jnp.concatenate([seq_scr[c, s * pitch:s * pitch + tc, :] for c in range(in_tiles)],
                                   axis=1).astype(o_ref.dtype)


def _s5_step_kernel(x_ref, s0r_ref, s0i_ref, gn_ref, ws_ref, wbr_ref, wbi_ref, wcr_ref, wci_ref, ar_ref, ai_ref,
                    d_ref, wg_ref, bg_ref, o_ref, str_ref, sti_ref):
    u = jnp.dot(_bf(_rms(x_ref[...], gn_ref[...])), ws_ref[...], preferred_element_type=F32)
    ub = _bf(u)
    xr_blocks, xi_blocks = [], []
    for q in range(S5_BLOCKS):
        lanes = slice(q * S5_BLOCK_STATE, (q + 1) * S5_BLOCK_STATE)
        ar, ai = ar_ref[:, lanes], ai_ref[:, lanes]
        sr, si = s0r_ref[:, lanes], s0i_ref[:, lanes]
        bu_re, bu_im = _s5_drive(ub, wbr_ref, wbi_ref, q)
        xr = ar * sr - ai * si + bu_re
        xi = ar * si + ai * sr + bu_im
        str_ref[:, lanes] = xr
        sti_ref[:, lanes] = xi
        xr_blocks.append(xr)
        xi_blocks.append(xi)
    o_ref[...] = _s5_readout(u, xr_blocks, xi_blocks, wcr_ref, wci_ref, d_ref[...], wg_ref[...], bg_ref[...])


def _s5(x, s0r, s0i, layer, weights, seq):
    wspecs = [_layer_const(w, layer) for w in weights]
    if seq:
        bsz, t, d = x.shape
        tc = min(S5_CHUNK, t)
        st = _full((bsz, S5_LANES))
        return pl.pallas_call(
            functools.partial(_s5_seq_kernel, bt=bsz, tc=tc),
            grid=(t // tc,),
            in_specs=[pl.BlockSpec((bsz, tc, d), lambda j: (0, j, 0)), st, st] + wspecs,
            out_specs=[pl.BlockSpec((bsz, tc, WIDTH), lambda j: (0, j, 0)), st, st],
            out_shape=[jax.ShapeDtypeStruct((bsz, t, WIDTH), BRANCH_DTYPE)]
            + [jax.ShapeDtypeStruct((bsz, S5_LANES), F32)] * 2,
            scratch_shapes=[pltpu.VMEM((S5_LANES // LANES, bsz * tc, LANES), F32)] * 2
            + [pltpu.VMEM((bsz, S5_LANES), F32)] * 2
            + [pltpu.VMEM((WIDTH // LANES, bsz * (tc + S5_ROW_PAD), LANES), F32),
               pltpu.VMEM((WIDTH // LANES, bsz * tc, LANES), F32)],
            compiler_params=_cp(("arbitrary",)),
            name="s5_seq",
        )(x, s0r, s0i, *weights)
    n, d = x.shape
    st = _full((n, S5_LANES))
    st_in = pl.BlockSpec((None, n, S5_LANES), lambda i: (layer, 0, 0))
    return pl.pallas_call(
        _s5_step_kernel,
        grid=(1,),
        in_specs=[_full((n, d)), st_in, st_in] + wspecs,
        out_specs=[_full((n, WIDTH)), st, st],
        out_shape=[jax.ShapeDtypeStruct((n, WIDTH), F32)] + [jax.ShapeDtypeStruct((n, S5_LANES), F32)] * 2,
        compiler_params=_cp(("arbitrary",)),
        name="s5_step",
    )(x, s0r, s0i, *weights)


def _mamba_prep_math(xbc_taps, dt_raw, cw, cb, dtb):
    acc = xbc_taps[M_CONV - 1] * cw[0:1]
    for kk in range(1, M_CONV):
        acc = acc + xbc_taps[M_CONV - 1 - kk] * cw[kk:kk + 1]
    acc = acc + cb
    return acc * jax.nn.sigmoid(acc), _softplus(dt_raw + dtb)


def _ssd_post(y, z, nw):
    y = y * (z * jax.nn.sigmoid(z))
    half = WIDTH // M_GROUPS
    parts = []
    for gi in range(M_GROUPS):
        yg = y[:, gi * half:(gi + 1) * half]
        parts.append(yg * lax.rsqrt(jnp.mean(yg * yg, axis=-1, keepdims=True) + M_EPS))
    return jnp.concatenate(parts, axis=1) * nw


def _ssd_chunk_math(xbc, dt, alog_row, acol, dsk, expand, h_scr):
    q = xbc.shape[0]
    pair = 2 * M_HEAD
    xs = xbc[:, :WIDTH]
    row = lax.broadcasted_iota(jnp.int32, (q, q), 0)
    col = lax.broadcasted_iota(jnp.int32, (q, q), 1)
    lower = row >= col
    dt_full = _dot_exact_rhs(dt, expand, terms=3)
    cum = _dot_exact_lhs(_bf(lower.astype(F32)), dt_full * -jnp.exp(alog_row))
    da_t = jnp.transpose(dt[:, :LANES])[:SUBLANES] * -jnp.exp(acol)
    cum_t = _dot_exact_rhs(da_t, _bf((row <= col).astype(F32)), terms=3)
    ecum = jnp.exp(cum)
    cum_last = cum[q - 1:q, :]
    xd = xs * dt_full
    xdec = xd * jnp.exp(cum_last - cum)
    lane = lax.broadcasted_iota(jnp.int32, (q, pair), 1)
    prow = lax.broadcasted_iota(jnp.int32, (pair, M_STATE), 0)
    heads_per_group = M_HEADS // M_GROUPS
    ys = []
    for gi in range(M_GROUPS):
        bg = xbc[:, WIDTH + gi * M_STATE:WIDTH + (gi + 1) * M_STATE]
        cg = xbc[:, WIDTH + (M_GROUPS + gi) * M_STATE:WIDTH + (M_GROUPS + gi + 1) * M_STATE]
        cb = _dot_nt(cg, bg)
        for pp in range(heads_per_group // 2):
            p = gi * (heads_per_group // 2) + pp
            lanes = slice(p * pair, (p + 1) * pair)
            ms = []
            for h in (2 * p, 2 * p + 1):
                ch = cum[:, h * M_HEAD:(h + 1) * M_HEAD]
                seg = jnp.concatenate([ch] * (q // M_HEAD), axis=1) - cum_t[h:h + 1, :]
                ms.append(jnp.where(lower, jnp.exp(seg), 0.0) * cb)
            xd_pair = xd[:, lanes]
            stacked = jnp.concatenate([jnp.where(lane < M_HEAD, xd_pair, 0.0),
                                       jnp.where(lane >= M_HEAD, xd_pair, 0.0)], axis=0)
            hs = h_scr[p]
            ys.append(_dot(jnp.concatenate(ms, axis=1), stacked) + _dot_nt(cg, hs) * ecum[:, lanes])
            keep = jnp.where(prow < M_HEAD, jnp.exp(cum_t[2 * p:2 * p + 1, q - 1:q]),
                             jnp.exp(cum_t[2 * p + 1:2 * p + 2, q - 1:q]))
            h_scr[p] = hs * keep + _dot_tn(xdec[:, lanes], bg)
    return jnp.concatenate(ys, axis=1) + dsk * xs


def _mamba_seq_kernel(x_ref, init_ref, h0_ref, gn_ref, wm_ref, cw_ref, cb_ref, dtb_ref, alog_ref, acol_ref, dsk_ref,
                      nw_ref, expand_ref, o_ref, last_ref, ht_ref, carry, h_scr):
    j = pl.program_id(1)
    n_pairs = M_HEADS // 2

    @pl.when(j == 0)
    def _():
        carry[...] = init_ref[...]
        for p in range(n_pairs):
            h_scr[p] = jnp.concatenate([h0_ref[2 * p], h0_ref[2 * p + 1]], axis=0)

    zm = jnp.dot(_bf(_rms(x_ref[...], gn_ref[...])), wm_ref[...], preferred_element_type=F32)
    raw = zm[:, :M_CONV_CH]
    z = zm[:, M_CONV_CH:M_CONV_CH + WIDTH]
    c8 = carry[...]
    taps = [raw] + [_delayed(raw, c8, kk) for kk in range(1, M_CONV)]
    tail = raw[raw.shape[0] - SUBLANES:, :]
    carry[...] = tail
    last_ref[...] = tail
    xbc, dt = _mamba_prep_math(taps, zm[:, M_CONV_CH + WIDTH:], cw_ref[...], cb_ref[...], dtb_ref[...])
    q = min(SSD_CHUNK, xbc.shape[0])
    y = jnp.concatenate(
        [_ssd_chunk_math(xbc[r0:r0 + q], dt[r0:r0 + q], alog_ref[...], acol_ref[...], dsk_ref[...], expand_ref[...],
                         h_scr) for r0 in range(0, xbc.shape[0], q)], axis=0)
    o_ref[...] = _ssd_post(y, z, nw_ref[...]).astype(o_ref.dtype)

    @pl.when(j == pl.num_programs(1) - 1)
    def _():
        for p in range(n_pairs):
            hs = h_scr[p]
            ht_ref[2 * p] = hs[:M_HEAD]
            ht_ref[2 * p + 1] = hs[M_HEAD:]


def _mamba_seq(x, conv_state, h0, layer, weights):
    bsz, t, d = x.shape
    q = min(SSD_TILE, t)
    tile = lambda c: pl.BlockSpec((None, q, c), lambda b, j: (b, j, 0))
    edge = pl.BlockSpec((None, SUBLANES, M_CONV_CH), lambda b, j: (b, 0, 0))
    st = pl.BlockSpec((None, M_HEADS, M_HEAD, M_STATE), lambda b, j: (b, 0, 0, 0))
    return pl.pallas_call(
        _mamba_seq_kernel,
        grid=(bsz, t // q),
        in_specs=[tile(d), edge, st] + [_layer_const(w, layer) for w in weights],
        out_specs=[tile(WIDTH), edge, st],
        out_shape=[jax.ShapeDtypeStruct((bsz, t, WIDTH), BRANCH_DTYPE),
                   jax.ShapeDtypeStruct((bsz, SUBLANES, M_CONV_CH), F32),
                   jax.ShapeDtypeStruct((bsz, M_HEADS, M_HEAD, M_STATE), F32)],
        scratch_shapes=[pltpu.VMEM((SUBLANES, M_CONV_CH), F32),
                        pltpu.VMEM((M_HEADS // 2, 2 * M_HEAD, M_STATE), F32)],
        compiler_params=_cp(("arbitrary", "arbitrary")),
        name="mamba_seq",
    )(x, conv_state, h0, *weights)


def _mamba_step_kernel(zm_ref, past_ref, h0_ref, *refs, chained):
    (cw_ref, cb_ref, dtb_ref, alog_ref, dsk_ref, nw_ref, expand_ref,
     o_ref, ht_ref, y_scr) = refs[1:] if chained else refs
    bt = zm_ref.shape[0]
    zm = zm_ref[...]
    raw = zm[:, :M_CONV_CH]
    z = zm[:, M_CONV_CH:M_CONV_CH + WIDTH]
    past = past_ref[...]
    taps = [raw] + [past[:, (M_CONV - 1 - kk) * M_CONV_CH:(M_CONV - kk) * M_CONV_CH] for kk in range(1, M_CONV)]
    xbc, dt = _mamba_prep_math(taps, zm[:, M_CONV_CH + WIDTH:], cw_ref[...], cb_ref[...], dtb_ref[...])
    xs = xbc[:, :WIDTH]
    dt_full = _dot_exact_rhs(dt, expand_ref[...], terms=3)
    keep = jnp.exp(dt_full * -jnp.exp(alog_ref[...]))
    xd = xs * dt_full
    heads_per_group = M_HEADS // M_GROUPS
    zero_x = jnp.zeros((1, WIDTH), F32)
    zero_g = jnp.zeros((1, M_GROUPS * M_STATE), F32)
    new = {}
    for i in range(bt):
        one = slice(i, i + 1)
        xrow = _rows(xd[one], zero_x)
        brow = _rows(xbc[one, WIDTH:WIDTH + M_GROUPS * M_STATE], zero_g)
        for h in range(M_HEADS):
            gi = h // heads_per_group
            cs_ = slice(h * M_HEAD, (h + 1) * M_HEAD)
            kp = keep[one, cs_]
            hn = (h0_ref[i, h] * jnp.concatenate([kp, kp], axis=1)
                  + _dot_tn(xrow[:, cs_], brow[:, gi * M_STATE:(gi + 1) * M_STATE]))
            ht_ref[i, h] = hn
            new[i, h] = hn
    for i in range(bt):
        one = slice(i, i + 1)
        crow = _rows(xbc[one, WIDTH + M_GROUPS * M_STATE:], zero_g)
        outs = [_dot_nt(crow[:, (h // heads_per_group) * M_STATE:(h // heads_per_group + 1) * M_STATE], new[i, h])
                for h in range(M_HEADS)]
        y_scr[one, :] = jnp.concatenate(outs, axis=1)[0:1]
    y = y_scr[...] + dsk_ref[...] * xs
    o_ref[...] = _ssd_post(y, z, nw_ref[...])


def _mamba_step(zm, past_all, h_all, h_new, layer, weights):
    n = zm.shape[0]
    bt = SUBLANES
    tile = lambda c: pl.BlockSpec((bt, c), lambda i: (i, 0))
    past = pl.BlockSpec((None, bt, (M_CONV - 1) * M_CONV_CH), lambda i: (layer, i, 0))
    st = pl.BlockSpec((None, bt, M_HEADS, M_HEAD, M_STATE), lambda i: (layer, i, 0, 0, 0))
    chain, chain_specs, aliases = _chain_args(h_new)
    return pl.pallas_call(
        functools.partial(_mamba_step_kernel, chained=bool(chain)),
        grid=(n // bt,),
        in_specs=[tile(M_COLS_PAD), past, st] + chain_specs + [_layer_const(w, layer) for w in weights],
        out_specs=[tile(WIDTH), st],
        out_shape=[jax.ShapeDtypeStruct((n, WIDTH), F32), jax.ShapeDtypeStruct(h_all.shape, F32)],
        scratch_shapes=[pltpu.VMEM((bt, WIDTH), F32)],
        input_output_aliases=aliases,
        compiler_params=_cp(("arbitrary",)),
        name="mamba_step",
    )(zm, past_all, h_all, *chain, *weights)


def _merge_kernel(x_ref, or_ref, os_ref, om_ref, gn_ref, wg_ref, wb_ref, wo_ref, gp_ref, o_ref):
    x = x_ref[...]
    hb = _bf(_rms(x, gn_ref[...]))
    mixed = None
    for kk, ref in enumerate((or_ref, os_ref, om_ref)):
        gate = jax.nn.sigmoid(jnp.dot(hb, wg_ref[:, kk * D_MODEL:(kk + 1) * D_MODEL], preferred_element_type=F32))
        term = gate * jnp.dot(_bf(ref[...]), wb_ref[kk], preferred_element_type=F32)
        mixed = term if mixed is None else mixed + term
    out = jnp.dot(_bf(mixed), wo_ref[...], preferred_element_type=F32)
    o_ref[...] = x + _rms(out, gp_ref[...])


def _merge(x, o_r, o_s, o_m, layer, weights):
    n = x.shape[0]
    tm = min(TOKEN_TILE, n)
    tile = lambda c: pl.BlockSpec((tm, c), lambda i: (i, 0))
    return pl.pallas_call(
        _merge_kernel,
        grid=(n // tm,),
        in_specs=[tile(D_MODEL)] + [tile(WIDTH)] * 3 + [_layer_const(w, layer) for w in weights],
        out_specs=tile(D_MODEL),
        out_shape=jax.ShapeDtypeStruct((n, D_MODEL), F32),
        compiler_params=_cp(("arbitrary",)),
        name="merge",
    )(x, o_r, o_s, o_m, *weights)


def _ffn_chunk(hb, gate_taps_of, cols, wu_ref, cw_ref, cb_ref, wd_ref):
    gate = jnp.dot(hb, wu_ref[:, cols], preferred_element_type=F32)
    val = jnp.dot(hb, wu_ref[:, D_FF + cols.start:D_FF + cols.stop], preferred_element_type=F32)
    taps = gate_taps_of(gate)
    acc = taps[FFN_CONV - 1] * cw_ref[0:1, cols]
    for kk in range(1, FFN_CONV):
        acc = acc + taps[FFN_CONV - 1 - kk] * cw_ref[kk:kk + 1, cols]
    acc = acc + cb_ref[:, cols]
    return jnp.dot(_bf(jax.nn.gelu(acc) * val), wd_ref[cols, :], preferred_element_type=F32), gate


def _ffn_seq_kernel(x_ref, init_ref, gn_ref, wu_ref, cw_ref, cb_ref, wd_ref, gp_ref, o_ref, last_ref, carry):
    @pl.when(pl.program_id(1) == 0)
    def _():
        carry[...] = init_ref[...]

    x = x_ref[...]
    hb = _bf(_rms(x, gn_ref[...]))
    f = None
    for c0, c1 in zip(FFN_COL_STARTS[:-1], FFN_COL_STARTS[1:]):
        cols = slice(c0, c1)
        c8 = carry[:, cols]
        part, gate = _ffn_chunk(hb, lambda g: [g] + [_delayed(g, c8, kk) for kk in range(1, FFN_CONV)], cols,
                                wu_ref, cw_ref, cb_ref, wd_ref)
        tail = gate[gate.shape[0] - SUBLANES:, :]
        carry[:, cols] = tail
        last_ref[:, cols] = tail
        f = part if f is None else f + part
    o_ref[...] = x + _rms(f, gp_ref[...])


def _ffn_step_kernel(x_ref, past_ref, gn_ref, wu_ref, cw_ref, cb_ref, wd_ref, gp_ref, o_ref, gate_ref):
    x = x_ref[...]
    hb = _bf(_rms(x, gn_ref[...]))
    f = None
    for c0, c1 in zip(FFN_COL_STARTS[:-1], FFN_COL_STARTS[1:]):
        cols = slice(c0, c1)
        past = [past_ref[:, (FFN_CONV - 1 - kk) * D_FF + c0:(FFN_CONV - 1 - kk) * D_FF + c1]
                for kk in range(1, FFN_CONV)]
        part, gate = _ffn_chunk(hb, lambda g: [g] + past, cols, wu_ref, cw_ref, cb_ref, wd_ref)
        gate_ref[:, cols] = gate
        f = part if f is None else f + part
    o_ref[...] = x + _rms(f, gp_ref[...])


def _ffn(x, conv_state, layer, weights, seq):
    wspecs = [_layer_const(w, layer) for w in weights]
    if seq:
        bsz, t, d = x.shape
        tm = min(TOKEN_TILE, t)
        tile = pl.BlockSpec((None, tm, d), lambda b, j: (b, j, 0))
        edge = pl.BlockSpec((None, SUBLANES, D_FF), lambda b, j: (b, 0, 0))
        return pl.pallas_call(
            _ffn_seq_kernel,
            grid=(bsz, t // tm),
            in_specs=[tile, edge] + wspecs,
            out_specs=[tile, edge],
            out_shape=[jax.ShapeDtypeStruct((bsz, t, d), F32),
                       jax.ShapeDtypeStruct((bsz, SUBLANES, D_FF), F32)],
            scratch_shapes=[pltpu.VMEM((SUBLANES, D_FF), F32)],
            compiler_params=_cp(("arbitrary", "arbitrary")),
            name="ffn_seq",
        )(x, conv_state, *weights)
    n, d = x.shape
    return pl.pallas_call(
        _ffn_step_kernel,
        grid=(1,),
        in_specs=[_full((n, d)), pl.BlockSpec((None,) + conv_state.shape[1:], lambda i: (layer, 0, 0))] + wspecs,
        out_specs=[_full((n, d)), _full((n, D_FF))],
        out_shape=[jax.ShapeDtypeStruct((n, d), F32), jax.ShapeDtypeStruct((n, D_FF), F32)],
        compiler_params=_cp(("arbitrary",)),
        name="ffn_step",
    )(x, conv_state, *weights)


def _row(v):
    return v.reshape(v.shape[0], 1, -1).astype(F32)


def _all_layer_weights(params):
    (g_pre_mix, g_post_mix, g_pre_ffn, g_post_ffn, w_in,
     r_mu, r_w0, r_w2, r_a0, r_a2, r_g2, r_kk, r_ka, r_rk, r_ln_w, r_ln_b,
     s5_lam_re, s5_lam_im, s5_log_step, s5_b_re, s5_b_im, s5_c_re, s5_c_im, s5_d, s5_w_glu, s5_b_glu,
     m_conv_w, m_conv_b, m_dt_bias, m_a_log, m_d, m_norm_w,
     w_branch, w_out, w_up, f_conv_w, f_conv_b, w_down) = params
    depth = w_in.shape[0]
    c0 = R_COLS
    c1 = c0 + WIDTH
    c2 = c1 + WIDTH + M_CONV_CH
    c3 = c2 + M_HEADS
    every_layer = lambda m: jnp.broadcast_to(m[None], (depth,) + m.shape)
    head = jnp.arange(WIDTH) // R_HEAD
    ones_bd = every_layer((head[:, None] == head[None, :]).astype(BF16))
    gn = _row(g_pre_mix)
    w = {"g_pre_mix": gn}
    w_r = _bf(w_in[:, :, :c0])
    w_m = _bf(jnp.concatenate(
        [w_in[:, :, c1 + WIDTH:c2], w_in[:, :, c1:c1 + WIDTH], w_in[:, :, c2:c3],
         jnp.zeros((depth, D_MODEL, M_DT_PAD - M_HEADS), F32)], axis=2))
    w["w_r"], w["w_m"] = w_r, w_m
    rwkv_mix = (_row(r_mu), _row(r_w0), _bf(r_w2), _row(r_a0), _bf(r_a2), _bf(r_g2), _row(r_kk), _row(r_ka),
                ones_bd, _row(r_rk), _row(r_ln_w), _row(r_ln_b))
    w["rwkv_seq"] = (gn, w_r) + rwkv_mix
    w["rwkv_step"] = rwkv_mix
    ab_re, ab_im, bb_re, bb_im = _s5_params(s5_lam_re, s5_lam_im, s5_log_step, s5_b_re, s5_b_im)
    gpb = S5_GROUPS // S5_BLOCKS
    eye = jnp.eye(gpb, dtype=F32)
    blocked = lambda m: m.reshape(depth, S5_BLOCKS, gpb, S5_GROUP, S5_STATE)
    to_state = lambda bb: _bf(jnp.einsum('lqghp,gk->lqghkp', blocked(bb), eye)
                              .reshape(depth, S5_BLOCKS, S5_BLOCK_IN, S5_BLOCK_STATE))
    from_state = lambda c: _bf(jnp.einsum('lqghp,gk->lqgpkh', blocked(c), eye)
                               .reshape(depth, S5_BLOCKS, S5_BLOCK_STATE, S5_BLOCK_IN))
    w["s5"] = (gn, _bf(w_in[:, :, c0:c1]), to_state(bb_re), to_state(bb_im), from_state(s5_c_re),
               from_state(s5_c_im), ab_re.reshape(depth, 1, S5_LANES), ab_im.reshape(depth, 1, S5_LANES), _row(s5_d),
               _bf(s5_w_glu), _row(s5_b_glu))
    conv = (jnp.swapaxes(m_conv_w, 1, 2).astype(F32), _row(m_conv_b),
            _row(jnp.pad(m_dt_bias, ((0, 0), (0, M_DT_PAD - M_HEADS)))))
    alog_row = _row(jnp.repeat(m_a_log, M_HEAD, axis=1))
    expand = every_layer((jnp.arange(M_DT_PAD)[:, None] == (jnp.arange(WIDTH) // M_HEAD)[None, :]).astype(BF16))
    tail = (_row(jnp.repeat(m_d, M_HEAD, axis=1)), _row(m_norm_w), expand)
    w["mamba_seq"] = (gn, w_m) + conv + (alog_row, m_a_log.reshape(depth, M_HEADS, 1).astype(F32)) + tail
    w["mamba_step"] = conv + (alog_row,) + tail
    w["merge"] = (gn, _bf(w_in[:, :, c3:]), _bf(w_branch), _bf(w_out), _row(g_post_mix))
    w["ffn"] = (_row(g_pre_ffn), _bf(w_up), jnp.swapaxes(f_conv_w, 1, 2).astype(F32), _row(f_conv_b), _bf(w_down),
                _row(g_post_ffn))
    return w


def _zero_states(n):
    return dict(shift=jnp.zeros((n, SUBLANES, R_COLS), F32),
                wkv=jnp.zeros((n, R_HEADS, R_HEAD, R_HEAD), F32),
                s5=jnp.zeros((n, S5_LANES), F32),
                ssd=jnp.zeros((n, M_HEADS, M_HEAD, M_STATE), F32),
                mconv=jnp.zeros((n, SUBLANES, M_CONV_CH), F32),
                fconv=jnp.zeros((n, SUBLANES, D_FF), F32))


def _sequence_layer(x, init, w, layer):
    bsz, t, d = x.shape
    flat = lambda a: a.reshape(bsz * t, a.shape[-1])
    o_r, last_r, wkv1 = _rwkv_seq(x, init["shift"], init["wkv"], layer, w["rwkv_seq"])
    o_s, s5r1, s5i1 = _s5(x, init["s5"], init["s5"], layer, w["s5"], True)
    o_m, last_m, ssd1 = _mamba_seq(x, init["mconv"], init["ssd"], layer, w["mamba_seq"])
    mixed = _merge(flat(x), flat(o_r), flat(o_s), flat(o_m), layer, w["merge"])
    x, last_f = _ffn(mixed.reshape(bsz, t, d), init["fconv"], layer, w["ffn"], True)
    states = (last_r[:, SUBLANES - 1], wkv1, s5r1.reshape(bsz, S5_GROUPS, S5_STATE),
              s5i1.reshape(bsz, S5_GROUPS, S5_STATE), ssd1, last_m[:, SUBLANES - (M_CONV - 1):],
              last_f[:, SUBLANES - (FFN_CONV - 1):])
    return x, states


def _step_layer(x, cache, chained, w, layer):
    shift_all, wkv_all, s5r_all, s5i_all, ssd_all, mconv_all, fconv_all = cache
    z_r = _norm_matmul(x, w["g_pre_mix"], w["w_r"], layer, "in_proj_rwkv")
    z_m = _norm_matmul(x, w["g_pre_mix"], w["w_m"], layer, "in_proj_mamba")
    o_r, wkv_new = _rwkv_step(z_r, shift_all, wkv_all, chained[0], layer, w["rwkv_step"])
    o_s, s5r1, s5i1 = _s5(x, s5r_all, s5i_all, layer, w["s5"], False)
    o_m, ssd_new = _mamba_step(z_m, mconv_all, ssd_all, chained[1], layer, w["mamba_step"])
    x = _merge(x, o_r, o_s, o_m, layer, w["merge"])
    x, gate = _ffn(x, fconv_all, layer, w["ffn"], False)
    return x, (wkv_new, ssd_new), (z_r, s5r1, s5i1, z_m[:, :M_CONV_CH], gate)


def kernel(x_prompt, x_sample, state_rwkv_shift, state_rwkv_wkv, state_s5_re, state_s5_im, state_ssd, state_ssd_conv, state_ffn_conv, g_pre_mix, g_post_mix, g_pre_ffn, g_post_ffn, w_in, r_mu, r_w0, r_w2, r_a0, r_a2, r_g2, r_kk, r_ka, r_rk, r_ln_w, r_ln_b, s5_lam_re, s5_lam_im, s5_log_step, s5_b_re, s5_b_im, s5_c_re, s5_c_im, s5_d, s5_w_glu, s5_b_glu, m_conv_w, m_conv_b, m_dt_bias, m_a_log, m_d, m_norm_w, w_branch, w_out, w_up, f_conv_w, f_conv_b, w_down):
    stacked = (g_pre_mix, g_post_mix, g_pre_ffn, g_post_ffn, w_in,
               r_mu, r_w0, r_w2, r_a0, r_a2, r_g2, r_kk, r_ka, r_rk, r_ln_w, r_ln_b,
               s5_lam_re, s5_lam_im, s5_log_step, s5_b_re, s5_b_im, s5_c_re, s5_c_im, s5_d,
               s5_w_glu, s5_b_glu,
               m_conv_w, m_conv_b, m_dt_bias, m_a_log, m_d, m_norm_w,
               w_branch, w_out, w_up, f_conv_w, f_conv_b, w_down)
    depth = w_in.shape[0]
    pb, pt, d = x_prompt.shape
    sb, s_t, _ = x_sample.shape
    w = _all_layer_weights(stacked)
    per_seq = lambda a: a.reshape(depth, sb, -1)
    cache = (state_rwkv_shift, state_rwkv_wkv, per_seq(state_s5_re), per_seq(state_s5_im), state_ssd,
             per_seq(state_ssd_conv), per_seq(state_ffn_conv))
    init = _zero_states(pb)
    xp = x_prompt
    xs = x_sample.reshape(sb * s_t, d)
    chained = (None, None)
    p_states, s_rows = [], []
    for l in range(depth):
        xp, sp = _sequence_layer(xp, init, w, l)
        xs, chained, rows = _step_layer(xs, cache, chained, w, l)
        p_states.append(sp)
        s_rows.append(rows)
    p_shift, p_wkv, p_s5_re, p_s5_im, p_ssd, p_ssd_conv, p_ffn_conv = (jnp.stack(v, 0) for v in zip(*p_states))
    z_r, s5r, s5i, raw_xbc, gate = (jnp.stack(v, 0) for v in zip(*s_rows))
    s_wkv, s_ssd = chained
    s_s5_re = s5r.reshape(depth, sb, S5_GROUPS, S5_STATE)
    s_s5_im = s5i.reshape(depth, sb, S5_GROUPS, S5_STATE)
    s_ssd_conv = jnp.concatenate([state_ssd_conv[:, :, 1:], raw_xbc[:, :, None]], axis=2)
    s_ffn_conv = jnp.concatenate([state_ffn_conv[:, :, 1:], gate[:, :, None]], axis=2)
    return (xp, xs.reshape(sb, s_t, d), p_shift, z_r, p_wkv, s_wkv, p_s5_re, s_s5_re, p_s5_im, s_s5_im,
            p_ssd, s_ssd, p_ssd_conv, s_ssd_conv, p_ffn_conv, s_ffn_conv)
```

```python
import functools

import jax
import jax.numpy as jnp
from jax import lax
from jax.experimental import pallas as pl
from jax.experimental.pallas import tpu as pltpu

F32 = jnp.float32
BF16 = jnp.bfloat16

D_MODEL = 1024
WIDTH = 512
R_HEADS, R_HEAD = 8, 64
R_COLS = 1792
R_LN_EPS = 64e-5
DECAY_SCALE = 0.6065306597126334
S5_GROUPS, S5_GROUP, S5_STATE = 32, 16, 64
S5_LANES = S5_GROUPS * S5_STATE
M_HEADS, M_HEAD, M_GROUPS, M_STATE = 8, 64, 2, 128
M_CONV, M_CONV_CH = 4, 1024
M_DT_PAD = 256
M_COLS_PAD = M_CONV_CH + WIDTH + M_DT_PAD
M_EPS = 1e-5
D_FF = 2816
FFN_CONV = 3
EPS = 1e-6
SUBLANES = 8
LANES = 128

RWKV_CHUNK = 64
RWKV_TILE = 256
SSD_CHUNK = 128
S5_CHUNK = 128
S5_ROW_PAD = 8
S5_SCAN_TILES = 8
SSD_TILE = 256
BRANCH_DTYPE = BF16
TOKEN_TILE = 512
FFN_COL_STARTS = (0, 1536, D_FF)
VMEM_LIMIT = 56 * 1024 * 1024


def _cp(sem):
    return pltpu.CompilerParams(dimension_semantics=sem, vmem_limit_bytes=VMEM_LIMIT)


def _layer_const(w, layer):
    shape = w.shape[1:]
    nd = len(shape)
    return pl.BlockSpec((None,) + shape, lambda *_: (layer,) + (0,) * nd, pipeline_mode=pl.Buffered(1))


def _full(shape):
    nd = len(shape)
    return pl.BlockSpec(shape, lambda *_: (0,) * nd)


def _bf(x):
    return x.astype(BF16)


def _dot(a, b):
    return jnp.dot(_bf(a), _bf(b), preferred_element_type=F32)


def _dot_nt(a, b):
    return lax.dot_general(_bf(a), _bf(b), (((1,), (1,)), ((), ())), preferred_element_type=F32)


def _dot_tn(a, b):
    return lax.dot_general(_bf(a), _bf(b), (((0,), (0,)), ((), ())), preferred_element_type=F32)


def _split(x, terms):
    out = []
    for _ in range(terms - 1):
        h = _bf(x)
        out.append(h)
        x = x - h.astype(F32)
    out.append(_bf(x))
    return out


def _dot_exact_lhs(m_bf16, x, terms=3):
    acc = None
    for h in _split(x, terms):
        p = jnp.dot(m_bf16, h, preferred_element_type=F32)
        acc = p if acc is None else acc + p
    return acc


def _dot_exact_rhs(x, m_bf16, terms=2):
    acc = None
    for h in _split(x, terms):
        p = jnp.dot(h, m_bf16, preferred_element_type=F32)
        acc = p if acc is None else acc + p
    return acc


def _softplus(x):
    return jnp.maximum(x, 0.0) + jnp.log1p(jnp.exp(-jnp.abs(x)))


def _rms(x, g, eps=EPS):
    return x * lax.rsqrt(jnp.mean(x * x, axis=-1, keepdims=True) + eps) * g


def _delayed(x, carry8, k):
    rx = pltpu.roll(x, k, 0)
    rc = pltpu.roll(carry8, k, 0)
    row = lax.broadcasted_iota(jnp.int32, (SUBLANES, x.shape[1]), 0)
    head = jnp.where(row < k, rc, rx[:SUBLANES])
    if x.shape[0] == SUBLANES:
        return head
    return jnp.concatenate([head, rx[SUBLANES:]], axis=0)


def _row_pairs(x, *parts):
    hi = _bf(x).astype(F32)
    part = {"hi": hi, "lo": x - hi}
    row = lax.broadcasted_iota(jnp.int32, x.shape, 0)
    out = jnp.zeros_like(x)
    for n, name in enumerate(parts):
        src = part[name] if n == 0 else pltpu.roll(part[name], 2 * n, 0)
        out = jnp.where((row >= 2 * n) & (row < 2 * n + 2), src, out)
    return _bf(out)


def _row_group(t, rows):
    start = t * rows
    return pl.multiple_of(start, SUBLANES) if rows % SUBLANES == 0 else start


def _rows(first, second):
    row = lax.broadcasted_iota(jnp.int32, (SUBLANES, first.shape[1]), 0)
    return jnp.where(row == 0, first, jnp.where(row == 1, second, 0.0))


def _norm_matmul_kernel(x_ref, g_ref, w_ref, o_ref):
    o_ref[...] = jnp.dot(_bf(_rms(x_ref[...], g_ref[...])), w_ref[...], preferred_element_type=F32)


def _norm_matmul(x, g, w, layer, name):
    n, d = x.shape
    c = w.shape[-1]
    return pl.pallas_call(
        _norm_matmul_kernel,
        grid=(1,),
        in_specs=[_full((n, d)), _layer_const(g, layer), _layer_const(w, layer)],
        out_specs=_full((n, c)),
        out_shape=jax.ShapeDtypeStruct((n, c), F32),
        compiler_params=_cp(("arbitrary",)),
        name=name,
    )(x, g, w)


def _rwkv_prep_math(z, prev, mu, w0, w2, a0, a2, g2, kkw, kaw, ones_bd):
    zm = z + (prev - z) * mu
    r = zm[:, 0:WIDTH]
    k = zm[:, WIDTH:2 * WIDTH]
    v = zm[:, 2 * WIDTH:3 * WIDTH]
    dw = zm[:, 1536:1600]
    da = zm[:, 1600:1664]
    dg = zm[:, 1664:1792]
    ld = -DECAY_SCALE * jax.nn.sigmoid(w0 + _dot(jnp.tanh(dw), w2))
    a = jax.nn.sigmoid(a0 + _dot(da, a2))
    g = _dot(jax.nn.sigmoid(dg), g2)
    kk = k * kkw
    ss = _dot_exact_rhs(kk * kk, ones_bd)
    kk = kk * lax.rsqrt(jnp.maximum(ss, 1e-24))
    k2 = k * (1.0 + (a - 1.0) * kaw)
    return r, ld, k2, v, kk, kk * a, g


def _rwkv_post(y, r, k, v, g, rk, lnw, lnb, ones_bd):
    inv = 1.0 / R_HEAD
    mean = _dot_exact_rhs(y, ones_bd) * inv
    d = y - mean
    var = _dot_exact_rhs(d * d, ones_bd) * inv
    yn = d * lax.rsqrt(var + R_LN_EPS) * lnw + lnb
    bonus = _dot_exact_rhs(r * k * rk, ones_bd) * v
    return (yn + bonus) * g


def _rwkv_chunks(r, ld, k, v, kk, b, s_scr):
    L = RWKV_CHUNK
    pair = 2 * R_HEAD
    n_pairs = R_HEADS // 2
    tm = r.shape[0]
    n_chunks = tm // L
    trow = lax.broadcasted_iota(jnp.int32, (tm, tm), 0)
    tcol = lax.broadcasted_iota(jnp.int32, (tm, tm), 1)
    same_chunk = (trow // L) == (tcol // L)
    cum = _dot_exact_lhs(_bf(((trow >= tcol) & same_chunk).astype(F32)), ld)
    wc = jnp.exp(cum)
    winv = jnp.exp(-cum)
    r_t = r * wc
    kk_t = kk * jnp.exp(cum - ld)
    k_h = k * winv
    b_h = b * winv

    row = lax.broadcasted_iota(jnp.int32, (pair, pair), 0)
    col = lax.broadcasted_iota(jnp.int32, (pair, pair), 1)
    same_head = (row // R_HEAD) == (col // R_HEAD)
    lrow = lax.broadcasted_iota(jnp.int32, (L, pair), 0)
    lcol = lax.broadcasted_iota(jnp.int32, (L, pair), 1) % R_HEAD
    strict = lrow > lcol
    lower = lrow >= lcol

    def bd(x):
        xb = _bf(x)
        return jnp.where(same_head, jnp.concatenate([xb, xb], axis=0), jnp.zeros((), BF16))

    units = [(c, p) for c in range(n_chunks) for p in range(n_pairs)]
    pre = {}
    for c, p in units:
        rows = slice(c * L, (c + 1) * L)
        lanes = slice(p * pair, (p + 1) * pair)
        wl = wc[(c + 1) * L - 1:(c + 1) * L, lanes]
        pre[c, p] = dict(kkt=kk_t[rows, lanes], rt=r_t[rows, lanes], kh=k_h[rows, lanes], bh=b_h[rows, lanes],
                         v=v[rows, lanes], vbd=bd(v[rows, lanes]), wl=wl)
    for u in units:
        d = pre[u]
        a = _dot_nt(jnp.concatenate([d["kkt"], d["rt"]], axis=0),
                    jnp.concatenate([bd(d["kh"]), bd(d["bh"])], axis=0))
        d["akk_k"] = jnp.where(strict, a[:L, :pair], 0.0)
        d["n"] = jnp.where(strict, a[:L, pair:], 0.0)
        d["ar_k"] = jnp.where(lower, a[L:, :pair], 0.0)
        d["ar_b"] = jnp.where(lower, a[L:, pair:], 0.0)
        d["q"] = -d["n"]
        d["m"] = d["n"]
    power = 2
    while power < L:
        for u in units:
            d = pre[u]
            d["m"] = _dot(d["m"], bd(d["m"]))
            d["q"] = d["q"] + d["m"] + _dot(d["q"], bd(d["m"]))
        power *= 2
    for u in units:
        d = pre[u]
        xy = _dot(jnp.concatenate([d["akk_k"], d["ar_k"]], axis=0), d["vbd"])
        x, d["y0"] = xy[:L], xy[L:]
        both = jnp.concatenate([d["kkt"], x], axis=1)
        both = both + _dot(d["q"], jnp.concatenate([bd(d["kkt"]), bd(x)], axis=1))
        d["g"], d["u0"] = both[:, :pair], both[:, pair:]
    for u in units:
        d = pre[u]
        t = _dot(d["ar_b"], jnp.concatenate([bd(d["g"]), bd(d["u0"])], axis=1))
        d["ry"] = d["rt"] - t[:, :pair]
        d["y0"] = d["y0"] - t[:, pair:]
        kw = d["kh"] * d["wl"]
        bw = d["bh"] * d["wl"]
        d["pm"] = jnp.where(same_head, _dot_tn(d["g"], bw), 0.0)
        d["c"] = jnp.where(same_head, _dot_tn(jnp.concatenate([d["v"], -d["u0"]], axis=0),
                                               jnp.concatenate([kw, bw], axis=0)), 0.0)
    ys = []
    for c in range(n_chunks):
        parts = []
        for p in range(n_pairs):
            d = pre[c, p]
            s = s_scr[p]
            parts.append(_dot_nt(d["ry"], s) + d["y0"])
            s_scr[p] = s * d["wl"] - _dot(s, d["pm"]) + d["c"]
        ys.append(jnp.concatenate(parts, axis=1))
    return jnp.concatenate(ys, axis=0)


def _rwkv_seq_kernel(x_ref, init_ref, s0_ref, gn_ref, wr_ref, mu_ref, w0_ref, w2_ref, a0_ref, a2_ref, g2_ref,
                     kkw_ref, kaw_ref, ones_ref, rk_ref, lnw_ref, lnb_ref, o_ref, last_ref, st_ref, carry, s_scr):
    j = pl.program_id(1)
    n_pairs = R_HEADS // 2

    @pl.when(j == 0)
    def _():
        carry[...] = init_ref[...]
        zero = jnp.zeros((R_HEAD, R_HEAD), F32)
        for p in range(n_pairs):
            top = jnp.concatenate([s0_ref[2 * p], zero], axis=1)
            bot = jnp.concatenate([zero, s0_ref[2 * p + 1]], axis=1)
            s_scr[p] = jnp.concatenate([top, bot], axis=0)

    z = jnp.dot(_bf(_rms(x_ref[...], gn_ref[...])), wr_ref[...], preferred_element_type=F32)
    prev = _delayed(z, carry[...], 1)
    tail = z[z.shape[0] - SUBLANES:, :]
    carry[...] = tail
    last_ref[...] = tail
    ones_bd = ones_ref[...]
    r, ld, k, v, kk, b, g = _rwkv_prep_math(z, prev, mu_ref[...], w0_ref[...], w2_ref[...], a0_ref[...],
                                            a2_ref[...], g2_ref[...], kkw_ref[...], kaw_ref[...], ones_bd)
    y = _rwkv_chunks(r, ld, k, v, kk, b, s_scr)
    o_ref[...] = _rwkv_post(y, r, k, v, g, rk_ref[...], lnw_ref[...], lnb_ref[...], ones_bd).astype(o_ref.dtype)

    @pl.when(j == pl.num_programs(1) - 1)
    def _():
        for p in range(n_pairs):
            s = s_scr[p]
            st_ref[2 * p] = s[:R_HEAD, :R_HEAD]
            st_ref[2 * p + 1] = s[R_HEAD:, R_HEAD:]


def _rwkv_seq(x, shift, s0, layer, weights):
    bsz, t, d = x.shape
    tm = min(RWKV_TILE, t)
    tile = lambda c: pl.BlockSpec((None, tm, c), lambda b, j: (b, j, 0))
    edge = pl.BlockSpec((None, SUBLANES, R_COLS), lambda b, j: (b, 0, 0))
    st = pl.BlockSpec((None, R_HEADS, R_HEAD, R_HEAD), lambda b, j: (b, 0, 0, 0))
    return pl.pallas_call(
        _rwkv_seq_kernel,
        grid=(bsz, t // tm),
        in_specs=[tile(d), edge, st] + [_layer_const(w, layer) for w in weights],
        out_specs=[tile(WIDTH), edge, st],
        out_shape=[jax.ShapeDtypeStruct((bsz, t, WIDTH), BRANCH_DTYPE),
                   jax.ShapeDtypeStruct((bsz, SUBLANES, R_COLS), F32),
                   jax.ShapeDtypeStruct((bsz, R_HEADS, R_HEAD, R_HEAD), F32)],
        scratch_shapes=[pltpu.VMEM((SUBLANES, R_COLS), F32),
                        pltpu.VMEM((R_HEADS // 2, 2 * R_HEAD, 2 * R_HEAD), F32)],
        compiler_params=_cp(("arbitrary", "arbitrary")),
        name="rwkv_seq",
    )(x, shift, s0, *weights)


def _rwkv_step_kernel(z_ref, prev_ref, s0_ref, mu_ref, w0_ref, w2_ref, a0_ref, a2_ref, g2_ref, kkw_ref, kaw_ref,
                      ones_ref, rk_ref, lnw_ref, lnb_ref, o_ref, st_ref, y_scr):
    bt = z_ref.shape[0]
    ones_bd = ones_ref[...]
    r, ld, k, v, kk, b, g = _rwkv_prep_math(z_ref[...], prev_ref[...], mu_ref[...], w0_ref[...], w2_ref[...],
                                            a0_ref[...], a2_ref[...], g2_ref[...], kkw_ref[...], kaw_ref[...],
                                            ones_bd)
    w = jnp.exp(ld)
    wr = w * r
    b_dot_r = _dot_exact_rhs(b * r, ones_bd)
    k_dot_r = _dot_exact_rhs(k * r, ones_bd)

    projs = []
    for i in range(bt):
        one = slice(i, i + 1)
        lhs = _rows(kk[one], wr[one])
        projs.append(jnp.concatenate(
            [_dot_nt(lhs[:, h * R_HEAD:(h + 1) * R_HEAD], s0_ref[i, h]) for h in range(R_HEADS)], axis=1))
    pieces = []
    for i in range(bt):
        one = slice(i, i + 1)
        sa = projs[i][0:1]
        y_scr[one, :] = projs[i][1:2] - sa * b_dot_r[one] + v[one] * k_dot_r[one]
        pieces.append((_row_pairs(_rows(v[one], -sa), "hi", "hi", "lo"), _row_pairs(_rows(k[one], b[one]), "hi", "lo", "hi")))
    for i in range(bt):
        one = slice(i, i + 1)
        left, right = pieces[i]
        for h in range(R_HEADS):
            cs = slice(h * R_HEAD, (h + 1) * R_HEAD)
            upd = lax.dot_general(left[:, cs], right[:, cs], (((0,), (0,)), ((), ())), preferred_element_type=F32)
            st_ref[i, h] = s0_ref[i, h] * w[one, cs] + upd
    o_ref[...] = _rwkv_post(y_scr[...], r, k, v, g, rk_ref[...], lnw_ref[...], lnb_ref[...], ones_bd)


def _rwkv_step(z, shift_all, s_all, layer, weights):
    n = z.shape[0]
    bt = SUBLANES
    tile = lambda c: pl.BlockSpec((bt, c), lambda i: (i, 0))
    prev = pl.BlockSpec((None, bt, R_COLS), lambda i: (layer, i, 0))
    st_in = pl.BlockSpec((None, bt, R_HEADS, R_HEAD, R_HEAD), lambda i: (layer, i, 0, 0, 0))
    st_out = pl.BlockSpec((bt, R_HEADS, R_HEAD, R_HEAD), lambda i: (i, 0, 0, 0))
    return pl.pallas_call(
        _rwkv_step_kernel,
        grid=(n // bt,),
        in_specs=[tile(R_COLS), prev, st_in] + [_layer_const(w, layer) for w in weights],
        out_specs=[tile(WIDTH), st_out],
        out_shape=[jax.ShapeDtypeStruct((n, WIDTH), F32), jax.ShapeDtypeStruct(s_all.shape[1:], F32)],
        scratch_shapes=[pltpu.VMEM((bt, WIDTH), F32)],
        compiler_params=_cp(("arbitrary",)),
        name="rwkv_step",
    )(z, shift_all, s_all, *weights)


S5_BLOCKS = 4
S5_BLOCK_IN = WIDTH // S5_BLOCKS
S5_BLOCK_STATE = S5_LANES // S5_BLOCKS
S5_BLOCK_TILES = S5_BLOCK_STATE // LANES


def _s5_param_kernel(lr_ref, li_ref, ls_ref, br_ref, bi_ref, abr_ref, abi_ref, bbr_ref, bbi_ref):
    lr, li = lr_ref[...], li_ref[...]
    delta = jnp.exp(ls_ref[...])
    mag = jnp.exp(lr * delta)
    ab_re = mag * jnp.cos(li * delta)
    ab_im = mag * jnp.sin(li * delta)
    den = lr * lr + li * li
    nr, ni = ab_re - 1.0, ab_im
    cf_re = (nr * lr + ni * li) / den
    cf_im = (ni * lr - nr * li) / den
    abr_ref[...] = ab_re
    abi_ref[...] = ab_im
    br, bi = br_ref[...], bi_ref[...]
    bbr_ref[...] = cf_re[:, None, :] * br - cf_im[:, None, :] * bi
    bbi_ref[...] = cf_re[:, None, :] * bi + cf_im[:, None, :] * br


def _s5_params(lam_re, lam_im, log_step, b_re, b_im):
    depth, g, p, h = b_re.shape
    gp = jax.ShapeDtypeStruct((depth, g, p), F32)
    ghp = jax.ShapeDtypeStruct((depth, g, h, p), F32)
    args = (lam_re, lam_im, log_step.reshape(depth, g, 1), jnp.swapaxes(b_re, 2, 3), jnp.swapaxes(b_im, 2, 3))
    per_layer = lambda shape: pl.BlockSpec((None,) + shape, lambda l: (l,) + (0,) * len(shape))
    return pl.pallas_call(
        _s5_param_kernel,
        grid=(depth,),
        in_specs=[per_layer(a.shape[1:]) for a in args],
        out_specs=[per_layer((g, p)), per_layer((g, p)), per_layer((g, h, p)), per_layer((g, h, p))],
        out_shape=[gp, gp, ghp, ghp],
        compiler_params=_cp(("arbitrary",)),
        name="s5_params",
    )(*args)


def _s5_drive(ub, wbr_ref, wbi_ref, blk):
    cols = slice(blk * S5_BLOCK_IN, (blk + 1) * S5_BLOCK_IN)
    return (jnp.dot(ub[:, cols], wbr_ref[blk], preferred_element_type=F32),
            jnp.dot(ub[:, cols], wbi_ref[blk], preferred_element_type=F32))


def _s5_readout(u, xr_blocks, xi_blocks, wcr_ref, wci_ref, d, wg, bg):
    y = jnp.concatenate([_dot(xr_blocks[q], wcr_ref[q]) - _dot(xi_blocks[q], wci_ref[q])
                         for q in range(S5_BLOCKS)], axis=1) + d * u
    y = jax.nn.gelu(y)
    return y * jax.nn.sigmoid(_dot(y, wg) + bg)


def _s5_seq_kernel(x_ref, s0r_ref, s0i_ref, gn_ref, ws_ref, wbr_ref, wbi_ref, wcr_ref, wci_ref, ar_ref, ai_ref,
                   d_ref, wg_ref, bg_ref, o_ref, str_ref, sti_ref, xr_scr, xi_scr, cr_scr, ci_scr, seq_scr, tm_scr,
                   *, bt, tc):
    @pl.when(pl.program_id(0) == 0)
    def _():
        cr_scr[...] = s0r_ref[...]
        ci_scr[...] = s0i_ref[...]

    pitch = tc + S5_ROW_PAD
    in_tiles = WIDTH // LANES
    n_tiles = S5_LANES // LANES
    x = x_ref[...].reshape(bt * tc, D_MODEL)
    u_seq = jnp.dot(_bf(_rms(x, gn_ref[...])), ws_ref[...], preferred_element_type=F32)
    for c in range(in_tiles):
        for s in range(bt):
            seq_scr[c, s * pitch:s * pitch + tc, :] = u_seq[s * tc:(s + 1) * tc, c * LANES:(c + 1) * LANES]

    def to_time_major(t, carry):
        dst = pl.ds(_row_group(t, bt), bt)
        for c in range(in_tiles):
            tm_scr[c, dst, :] = seq_scr[c, pl.ds(t, bt, stride=pitch), :]
        return carry

    lax.fori_loop(0, tc, to_time_major, 0, unroll=2)
    u = jnp.concatenate([tm_scr[c] for c in range(in_tiles)], axis=1)
    ub = _bf(u)
    for q in range(S5_BLOCKS):
        bu_re, bu_im = _s5_drive(ub, wbr_ref, wbi_ref, q)
        for c in range(S5_BLOCK_TILES):
            xr_scr[q * S5_BLOCK_TILES + c] = bu_re[:, c * LANES:(c + 1) * LANES]
            xi_scr[q * S5_BLOCK_TILES + c] = bu_im[:, c * LANES:(c + 1) * LANES]

    for g0 in range(0, n_tiles, S5_SCAN_TILES):
        tiles = range(g0, g0 + S5_SCAN_TILES)
        ar = [jnp.broadcast_to(ar_ref[:, c * LANES:(c + 1) * LANES], (bt, LANES)) for c in tiles]
        ai = [jnp.broadcast_to(ai_ref[:, c * LANES:(c + 1) * LANES], (bt, LANES)) for c in tiles]

        def body(t, carry, tiles=tiles, ar=ar, ai=ai):
            rows = pl.ds(_row_group(t, bt), bt)
            out = []
            for n, c in enumerate(tiles):
                xr, xi = carry[2 * n], carry[2 * n + 1]
                nr = ar[n] * xr - ai[n] * xi + xr_scr[c, rows, :]
                ni = ar[n] * xi + ai[n] * xr + xi_scr[c, rows, :]
                xr_scr[c, rows, :] = nr
                xi_scr[c, rows, :] = ni
                out += [nr, ni]
            return tuple(out)

        init = []
        for c in tiles:
            init += [cr_scr[:, c * LANES:(c + 1) * LANES], ci_scr[:, c * LANES:(c + 1) * LANES]]
        fin = lax.fori_loop(0, tc, body, tuple(init), unroll=2)
        for n, c in enumerate(tiles):
            cr_scr[:, c * LANES:(c + 1) * LANES] = fin[2 * n]
            ci_scr[:, c * LANES:(c + 1) * LANES] = fin[2 * n + 1]
    str_ref[...] = cr_scr[...]
    sti_ref[...] = ci_scr[...]

    blocks = lambda scr: [jnp.concatenate([scr[q * S5_BLOCK_TILES + c] for c in range(S5_BLOCK_TILES)], axis=1)
                          for q in range(S5_BLOCKS)]
    y = _s5_readout(u, blocks(xr_scr), blocks(xi_scr), wcr_ref, wci_ref, d_ref[...], wg_ref[...], bg_ref[...])
    for c in range(in_tiles):
        tm_scr[c] = y[:, c * LANES:(c + 1) * LANES]

    def to_sequence_major(t, carry):
        src = pl.ds(_row_group(t, bt), bt)
        for c in range(in_tiles):
            seq_scr[c, pl.ds(t, bt, stride=pitch), :] = tm_scr[c, src, :]
        return carry

    lax.fori_loop(0, tc, to_sequence_major, 0, unroll=2)
    for s in range(bt):
        o_ref[s] = jnp.concatenate([seq_scr[c, s * pitch:s * pitch + tc, :] for c in range(in_tiles)],
                                   axis=1).astype(o_ref.dtype)


def _s5_step_kernel(x_ref, s0r_ref, s0i_ref, gn_ref, ws_ref, wbr_ref, wbi_ref, wcr_ref, wci_ref, ar_ref, ai_ref,
                    d_ref, wg_ref, bg_ref, o_ref, str_ref, sti_ref):
    u = jnp.dot(_bf(_rms(x_ref[...], gn_ref[...])), ws_ref[...], preferred_element_type=F32)
    ub = _bf(u)
    xr_blocks, xi_blocks = [], []
    for q in range(S5_BLOCKS):
        lanes = slice(q * S5_BLOCK_STATE, (q + 1) * S5_BLOCK_STATE)
        ar, ai = ar_ref[:, lanes], ai_ref[:, lanes]
        sr, si = s0r_ref[:, lanes], s0i_ref[:, lanes]
        bu_re, bu_im = _s5_drive(ub, wbr_ref, wbi_ref, q)
        xr = ar * sr - ai * si + bu_re
        xi = ar * si + ai * sr + bu_im
        str_ref[:, lanes] = xr
        sti_ref[:, lanes] = xi
        xr_blocks.append(xr)
        xi_blocks.append(xi)
    o_ref[...] = _s5_readout(u, xr_blocks, xi_blocks, wcr_ref, wci_ref, d_ref[...], wg_ref[...], bg_ref[...])


def _s5(x, s0r, s0i, layer, weights, seq):
    wspecs = [_layer_const(w, layer) for w in weights]
    if seq:
        bsz, t, d = x.shape
        tc = min(S5_CHUNK, t)
        st = _full((bsz, S5_LANES))
        return pl.pallas_call(
            functools.partial(_s5_seq_kernel, bt=bsz, tc=tc),
            grid=(t // tc,),
            in_specs=[pl.BlockSpec((bsz, tc, d), lambda j: (0, j, 0)), st, st] + wspecs,
            out_specs=[pl.BlockSpec((bsz, tc, WIDTH), lambda j: (0, j, 0)), st, st],
            out_shape=[jax.ShapeDtypeStruct((bsz, t, WIDTH), BRANCH_DTYPE)]
            + [jax.ShapeDtypeStruct((bsz, S5_LANES), F32)] * 2,
            scratch_shapes=[pltpu.VMEM((S5_LANES // LANES, bsz * tc, LANES), F32)] * 2
            + [pltpu.VMEM((bsz, S5_LANES), F32)] * 2
            + [pltpu.VMEM((WIDTH // LANES, bsz * (tc + S5_ROW_PAD), LANES), F32),
               pltpu.VMEM((WIDTH // LANES, bsz * tc, LANES), F32)],
            compiler_params=_cp(("arbitrary",)),
            name="s5_seq",
        )(x, s0r, s0i, *weights)
    n, d = x.shape
    st = _full((n, S5_LANES))
    st_in = pl.BlockSpec((None, n, S5_LANES), lambda i: (layer, 0, 0))
    return pl.pallas_call(
        _s5_step_kernel,
        grid=(1,),
        in_specs=[_full((n, d)), st_in, st_in] + wspecs,
        out_specs=[_full((n, WIDTH)), st, st],
        out_shape=[jax.ShapeDtypeStruct((n, WIDTH), F32)] + [jax.ShapeDtypeStruct((n, S5_LANES), F32)] * 2,
        compiler_params=_cp(("arbitrary",)),
        name="s5_step",
    )(x, s0r, s0i, *weights)


def _mamba_prep_math(xbc_taps, dt_raw, cw, cb, dtb):
    acc = xbc_taps[M_CONV - 1] * cw[0:1]
    for kk in range(1, M_CONV):
        acc = acc + xbc_taps[M_CONV - 1 - kk] * cw[kk:kk + 1]
    acc = acc + cb
    return acc * jax.nn.sigmoid(acc), _softplus(dt_raw + dtb)


def _ssd_post(y, z, nw):
    y = y * (z * jax.nn.sigmoid(z))
    half = WIDTH // M_GROUPS
    parts = []
    for gi in range(M_GROUPS):
        yg = y[:, gi * half:(gi + 1) * half]
        parts.append(yg * lax.rsqrt(jnp.mean(yg * yg, axis=-1, keepdims=True) + M_EPS))
    return jnp.concatenate(parts, axis=1) * nw


def _ssd_chunk_math(xbc, dt, alog_row, acol, dsk, expand, h_scr):
    q = xbc.shape[0]
    pair = 2 * M_HEAD
    xs = xbc[:, :WIDTH]
    row = lax.broadcasted_iota(jnp.int32, (q, q), 0)
    col = lax.broadcasted_iota(jnp.int32, (q, q), 1)
    lower = row >= col
    dt_full = _dot_exact_rhs(dt, expand, terms=3)
    cum = _dot_exact_lhs(_bf(lower.astype(F32)), dt_full * -jnp.exp(alog_row))
    da_t = jnp.transpose(dt[:, :LANES])[:SUBLANES] * -jnp.exp(acol)
    cum_t = _dot_exact_rhs(da_t, _bf((row <= col).astype(F32)), terms=3)
    ecum = jnp.exp(cum)
    cum_last = cum[q - 1:q, :]
    xd = xs * dt_full
    xdec = xd * jnp.exp(cum_last - cum)
    lane = lax.broadcasted_iota(jnp.int32, (q, pair), 1)
    prow = lax.broadcasted_iota(jnp.int32, (pair, M_STATE), 0)
    heads_per_group = M_HEADS // M_GROUPS
    ys = []
    for gi in range(M_GROUPS):
        bg = xbc[:, WIDTH + gi * M_STATE:WIDTH + (gi + 1) * M_STATE]
        cg = xbc[:, WIDTH + (M_GROUPS + gi) * M_STATE:WIDTH + (M_GROUPS + gi + 1) * M_STATE]
        cb = _dot_nt(cg, bg)
        for pp in range(heads_per_group // 2):
            p = gi * (heads_per_group // 2) + pp
            lanes = slice(p * pair, (p + 1) * pair)
            ms = []
            for h in (2 * p, 2 * p + 1):
                ch = cum[:, h * M_HEAD:(h + 1) * M_HEAD]
                seg = jnp.concatenate([ch] * (q // M_HEAD), axis=1) - cum_t[h:h + 1, :]
                ms.append(jnp.where(lower, jnp.exp(seg), 0.0) * cb)
            xd_pair = xd[:, lanes]
            stacked = jnp.concatenate([jnp.where(lane < M_HEAD, xd_pair, 0.0),
                                       jnp.where(lane >= M_HEAD, xd_pair, 0.0)], axis=0)
            hs = h_scr[p]
            ys.append(_dot(jnp.concatenate(ms, axis=1), stacked) + _dot_nt(cg, hs) * ecum[:, lanes])
            keep = jnp.where(prow < M_HEAD, jnp.exp(cum_t[2 * p:2 * p + 1, q - 1:q]),
                             jnp.exp(cum_t[2 * p + 1:2 * p + 2, q - 1:q]))
            h_scr[p] = hs * keep + _dot_tn(xdec[:, lanes], bg)
    return jnp.concatenate(ys, axis=1) + dsk * xs


def _mamba_seq_kernel(x_ref, init_ref, h0_ref, gn_ref, wm_ref, cw_ref, cb_ref, dtb_ref, alog_ref, acol_ref, dsk_ref,
                      nw_ref, expand_ref, o_ref, last_ref, ht_ref, carry, h_scr):
    j = pl.program_id(1)
    n_pairs = M_HEADS // 2

    @pl.when(j == 0)
    def _():
        carry[...] = init_ref[...]
        for p in range(n_pairs):
            h_scr[p] = jnp.concatenate([h0_ref[2 * p], h0_ref[2 * p + 1]], axis=0)

    zm = jnp.dot(_bf(_rms(x_ref[...], gn_ref[...])), wm_ref[...], preferred_element_type=F32)
    raw = zm[:, :M_CONV_CH]
    z = zm[:, M_CONV_CH:M_CONV_CH + WIDTH]
    c8 = carry[...]
    taps = [raw] + [_delayed(raw, c8, kk) for kk in range(1, M_CONV)]
    tail = raw[raw.shape[0] - SUBLANES:, :]
    carry[...] = tail
    last_ref[...] = tail
    xbc, dt = _mamba_prep_math(taps, zm[:, M_CONV_CH + WIDTH:], cw_ref[...], cb_ref[...], dtb_ref[...])
    q = min(SSD_CHUNK, xbc.shape[0])
    y = jnp.concatenate(
        [_ssd_chunk_math(xbc[r0:r0 + q], dt[r0:r0 + q], alog_ref[...], acol_ref[...], dsk_ref[...], expand_ref[...],
                         h_scr) for r0 in range(0, xbc.shape[0], q)], axis=0)
    o_ref[...] = _ssd_post(y, z, nw_ref[...]).astype(o_ref.dtype)

    @pl.when(j == pl.num_programs(1) - 1)
    def _():
        for p in range(n_pairs):
            hs = h_scr[p]
            ht_ref[2 * p] = hs[:M_HEAD]
            ht_ref[2 * p + 1] = hs[M_HEAD:]


def _mamba_seq(x, conv_state, h0, layer, weights):
    bsz, t, d = x.shape
    q = min(SSD_TILE, t)
    tile = lambda c: pl.BlockSpec((None, q, c), lambda b, j: (b, j, 0))
    edge = pl.BlockSpec((None, SUBLANES, M_CONV_CH), lambda b, j: (b, 0, 0))
    st = pl.BlockSpec((None, M_HEADS, M_HEAD, M_STATE), lambda b, j: (b, 0, 0, 0))
    return pl.pallas_call(
        _mamba_seq_kernel,
        grid=(bsz, t // q),
        in_specs=[tile(d), edge, st] + [_layer_const(w, layer) for w in weights],
        out_specs=[tile(WIDTH), edge, st],
        out_shape=[jax.ShapeDtypeStruct((bsz, t, WIDTH), BRANCH_DTYPE),
                   jax.ShapeDtypeStruct((bsz, SUBLANES, M_CONV_CH), F32),
                   jax.ShapeDtypeStruct((bsz, M_HEADS, M_HEAD, M_STATE), F32)],
        scratch_shapes=[pltpu.VMEM((SUBLANES, M_CONV_CH), F32),
                        pltpu.VMEM((M_HEADS // 2, 2 * M_HEAD, M_STATE), F32)],
        compiler_params=_cp(("arbitrary", "arbitrary")),
        name="mamba_seq",
    )(x, conv_state, h0, *weights)


def _mamba_step_kernel(zm_ref, past_ref, h0_ref, cw_ref, cb_ref, dtb_ref, alog_ref, dsk_ref, nw_ref, expand_ref,
                       o_ref, ht_ref, y_scr):
    bt = zm_ref.shape[0]
    zm = zm_ref[...]
    raw = zm[:, :M_CONV_CH]
    z = zm[:, M_CONV_CH:M_CONV_CH + WIDTH]
    past = past_ref[...]
    taps = [raw] + [past[:, (M_CONV - 1 - kk) * M_CONV_CH:(M_CONV - kk) * M_CONV_CH] for kk in range(1, M_CONV)]
    xbc, dt = _mamba_prep_math(taps, zm[:, M_CONV_CH + WIDTH:], cw_ref[...], cb_ref[...], dtb_ref[...])
    xs = xbc[:, :WIDTH]
    dt_full = _dot_exact_rhs(dt, expand_ref[...], terms=3)
    keep = jnp.exp(dt_full * -jnp.exp(alog_ref[...]))
    xd = xs * dt_full
    heads_per_group = M_HEADS // M_GROUPS
    zero_x = jnp.zeros((1, WIDTH), F32)
    zero_g = jnp.zeros((1, M_GROUPS * M_STATE), F32)
    new = {}
    for i in range(bt):
        one = slice(i, i + 1)
        xrow = _rows(xd[one], zero_x)
        brow = _rows(xbc[one, WIDTH:WIDTH + M_GROUPS * M_STATE], zero_g)
        for h in range(M_HEADS):
            gi = h // heads_per_group
            cs_ = slice(h * M_HEAD, (h + 1) * M_HEAD)
            kp = keep[one, cs_]
            hn = (h0_ref[i, h] * jnp.concatenate([kp, kp], axis=1)
                  + _dot_tn(xrow[:, cs_], brow[:, gi * M_STATE:(gi + 1) * M_STATE]))
            ht_ref[i, h] = hn
            new[i, h] = hn
    for i in range(bt):
        one = slice(i, i + 1)
        crow = _rows(xbc[one, WIDTH + M_GROUPS * M_STATE:], zero_g)
        outs = [_dot_nt(crow[:, (h // heads_per_group) * M_STATE:(h // heads_per_group + 1) * M_STATE], new[i, h])
                for h in range(M_HEADS)]
        y_scr[one, :] = jnp.concatenate(outs, axis=1)[0:1]
    y = y_scr[...] + dsk_ref[...] * xs
    o_ref[...] = _ssd_post(y, z, nw_ref[...])


def _mamba_step(zm, past_all, h_all, layer, weights):
    n = zm.shape[0]
    bt = SUBLANES
    tile = lambda c: pl.BlockSpec((bt, c), lambda i: (i, 0))
    past = pl.BlockSpec((None, bt, (M_CONV - 1) * M_CONV_CH), lambda i: (layer, i, 0))
    st_in = pl.BlockSpec((None, bt, M_HEADS, M_HEAD, M_STATE), lambda i: (layer, i, 0, 0, 0))
    st_out = pl.BlockSpec((bt, M_HEADS, M_HEAD, M_STATE), lambda i: (i, 0, 0, 0))
    return pl.pallas_call(
        _mamba_step_kernel,
        grid=(n // bt,),
        in_specs=[tile(M_COLS_PAD), past, st_in] + [_layer_const(w, layer) for w in weights],
        out_specs=[tile(WIDTH), st_out],
        out_shape=[jax.ShapeDtypeStruct((n, WIDTH), F32), jax.ShapeDtypeStruct(h_all.shape[1:], F32)],
        scratch_shapes=[pltpu.VMEM((bt, WIDTH), F32)],
        compiler_params=_cp(("arbitrary",)),
        name="mamba_step",
    )(zm, past_all, h_all, *weights)


def _merge_kernel(x_ref, or_ref, os_ref, om_ref, gn_ref, wg_ref, wb_ref, wo_ref, gp_ref, o_ref):
    x = x_ref[...]
    hb = _bf(_rms(x, gn_ref[...]))
    mixed = None
    for kk, ref in enumerate((or_ref, os_ref, om_ref)):
        gate = jax.nn.sigmoid(jnp.dot(hb, wg_ref[:, kk * D_MODEL:(kk + 1) * D_MODEL], preferred_element_type=F32))
        term = gate * jnp.dot(_bf(ref[...]), wb_ref[kk], preferred_element_type=F32)
        mixed = term if mixed is None else mixed + term
    out = jnp.dot(_bf(mixed), wo_ref[...], preferred_element_type=F32)
    o_ref[...] = x + _rms(out, gp_ref[...])


def _merge(x, o_r, o_s, o_m, layer, weights):
    n = x.shape[0]
    tm = min(TOKEN_TILE, n)
    tile = lambda c: pl.BlockSpec((tm, c), lambda i: (i, 0))
    return pl.pallas_call(
        _merge_kernel,
        grid=(n // tm,),
        in_specs=[tile(D_MODEL)] + [tile(WIDTH)] * 3 + [_layer_const(w, layer) for w in weights],
        out_specs=tile(D_MODEL),
        out_shape=jax.ShapeDtypeStruct((n, D_MODEL), F32),
        compiler_params=_cp(("arbitrary",)),
        name="merge",
    )(x, o_r, o_s, o_m, *weights)


def _ffn_chunk(hb, gate_taps_of, cols, wu_ref, cw_ref, cb_ref, wd_ref):
    gate = jnp.dot(hb, wu_ref[:, cols], preferred_element_type=F32)
    val = jnp.dot(hb, wu_ref[:, D_FF + cols.start:D_FF + cols.stop], preferred_element_type=F32)
    taps = gate_taps_of(gate)
    acc = taps[FFN_CONV - 1] * cw_ref[0:1, cols]
    for kk in range(1, FFN_CONV):
        acc = acc + taps[FFN_CONV - 1 - kk] * cw_ref[kk:kk + 1, cols]
    acc = acc + cb_ref[:, cols]
    return jnp.dot(_bf(jax.nn.gelu(acc) * val), wd_ref[cols, :], preferred_element_type=F32), gate


def _ffn_seq_kernel(x_ref, init_ref, gn_ref, wu_ref, cw_ref, cb_ref, wd_ref, gp_ref, o_ref, last_ref, carry):
    @pl.when(pl.program_id(1) == 0)
    def _():
        carry[...] = init_ref[...]

    x = x_ref[...]
    hb = _bf(_rms(x, gn_ref[...]))
    f = None
    for c0, c1 in zip(FFN_COL_STARTS[:-1], FFN_COL_STARTS[1:]):
        cols = slice(c0, c1)
        c8 = carry[:, cols]
        part, gate = _ffn_chunk(hb, lambda g: [g] + [_delayed(g, c8, kk) for kk in range(1, FFN_CONV)], cols,
                                wu_ref, cw_ref, cb_ref, wd_ref)
        tail = gate[gate.shape[0] - SUBLANES:, :]
        carry[:, cols] = tail
        last_ref[:, cols] = tail
        f = part if f is None else f + part
    o_ref[...] = x + _rms(f, gp_ref[...])


def _ffn_step_kernel(x_ref, past_ref, gn_ref, wu_ref, cw_ref, cb_ref, wd_ref, gp_ref, o_ref, gate_ref):
    x = x_ref[...]
    hb = _bf(_rms(x, gn_ref[...]))
    f = None
    for c0, c1 in zip(FFN_COL_STARTS[:-1], FFN_COL_STARTS[1:]):
        cols = slice(c0, c1)
        past = [past_ref[:, (FFN_CONV - 1 - kk) * D_FF + c0:(FFN_CONV - 1 - kk) * D_FF + c1]
                for kk in range(1, FFN_CONV)]
        part, gate = _ffn_chunk(hb, lambda g: [g] + past, cols, wu_ref, cw_ref, cb_ref, wd_ref)
        gate_ref[:, cols] = gate
        f = part if f is None else f + part
    o_ref[...] = x + _rms(f, gp_ref[...])


def _ffn(x, conv_state, layer, weights, seq):
    wspecs = [_layer_const(w, layer) for w in weights]
    if seq:
        bsz, t, d = x.shape
        tm = min(TOKEN_TILE, t)
        tile = pl.BlockSpec((None, tm, d), lambda b, j: (b, j, 0))
        edge = pl.BlockSpec((None, SUBLANES, D_FF), lambda b, j: (b, 0, 0))
        return pl.pallas_call(
            _ffn_seq_kernel,
            grid=(bsz, t // tm),
            in_specs=[tile, edge] + wspecs,
            out_specs=[tile, edge],
            out_shape=[jax.ShapeDtypeStruct((bsz, t, d), F32),
                       jax.ShapeDtypeStruct((bsz, SUBLANES, D_FF), F32)],
            scratch_shapes=[pltpu.VMEM((SUBLANES, D_FF), F32)],
            compiler_params=_cp(("arbitrary", "arbitrary")),
            name="ffn_seq",
        )(x, conv_state, *weights)
    n, d = x.shape
    return pl.pallas_call(
        _ffn_step_kernel,
        grid=(1,),
        in_specs=[_full((n, d)), pl.BlockSpec((None,) + conv_state.shape[1:], lambda i: (layer, 0, 0))] + wspecs,
        out_specs=[_full((n, d)), _full((n, D_FF))],
        out_shape=[jax.ShapeDtypeStruct((n, d), F32), jax.ShapeDtypeStruct((n, D_FF), F32)],
        compiler_params=_cp(("arbitrary",)),
        name="ffn_step",
    )(x, conv_state, *weights)


def _row(v):
    return v.reshape(v.shape[0], 1, -1).astype(F32)


def _all_layer_weights(params):
    (g_pre_mix, g_post_mix, g_pre_ffn, g_post_ffn, w_in,
     r_mu, r_w0, r_w2, r_a0, r_a2, r_g2, r_kk, r_ka, r_rk, r_ln_w, r_ln_b,
     s5_lam_re, s5_lam_im, s5_log_step, s5_b_re, s5_b_im, s5_c_re, s5_c_im, s5_d, s5_w_glu, s5_b_glu,
     m_conv_w, m_conv_b, m_dt_bias, m_a_log, m_d, m_norm_w,
     w_branch, w_out, w_up, f_conv_w, f_conv_b, w_down) = params
    depth = w_in.shape[0]
    c0 = R_COLS
    c1 = c0 + WIDTH
    c2 = c1 + WIDTH + M_CONV_CH
    c3 = c2 + M_HEADS
    every_layer = lambda m: jnp.broadcast_to(m[None], (depth,) + m.shape)
    head = jnp.arange(WIDTH) // R_HEAD
    ones_bd = every_layer((head[:, None] == head[None, :]).astype(BF16))
    gn = _row(g_pre_mix)
    w = {"g_pre_mix": gn}
    w_r = _bf(w_in[:, :, :c0])
    w_m = _bf(jnp.concatenate(
        [w_in[:, :, c1 + WIDTH:c2], w_in[:, :, c1:c1 + WIDTH], w_in[:, :, c2:c3],
         jnp.zeros((depth, D_MODEL, M_DT_PAD - M_HEADS), F32)], axis=2))
    w["w_r"], w["w_m"] = w_r, w_m
    rwkv_mix = (_row(r_mu), _row(r_w0), _bf(r_w2), _row(r_a0), _bf(r_a2), _bf(r_g2), _row(r_kk), _row(r_ka),
                ones_bd, _row(r_rk), _row(r_ln_w), _row(r_ln_b))
    w["rwkv_seq"] = (gn, w_r) + rwkv_mix
    w["rwkv_step"] = rwkv_mix
    ab_re, ab_im, bb_re, bb_im = _s5_params(s5_lam_re, s5_lam_im, s5_log_step, s5_b_re, s5_b_im)
    gpb = S5_GROUPS // S5_BLOCKS
    eye = jnp.eye(gpb, dtype=F32)
    blocked = lambda m: m.reshape(depth, S5_BLOCKS, gpb, S5_GROUP, S5_STATE)
    to_state = lambda bb: _bf(jnp.einsum('lqghp,gk->lqghkp', blocked(bb), eye)
                              .reshape(depth, S5_BLOCKS, S5_BLOCK_IN, S5_BLOCK_STATE))
    from_state = lambda c: _bf(jnp.einsum('lqghp,gk->lqgpkh', blocked(c), eye)
                               .reshape(depth, S5_BLOCKS, S5_BLOCK_STATE, S5_BLOCK_IN))
    w["s5"] = (gn, _bf(w_in[:, :, c0:c1]), to_state(bb_re), to_state(bb_im), from_state(s5_c_re),
               from_state(s5_c_im), ab_re.reshape(depth, 1, S5_LANES), ab_im.reshape(depth, 1, S5_LANES), _row(s5_d),
               _bf(s5_w_glu), _row(s5_b_glu))
    conv = (jnp.swapaxes(m_conv_w, 1, 2).astype(F32), _row(m_conv_b),
            _row(jnp.pad(m_dt_bias, ((0, 0), (0, M_DT_PAD - M_HEADS)))))
    alog_row = _row(jnp.repeat(m_a_log, M_HEAD, axis=1))
    expand = every_layer((jnp.arange(M_DT_PAD)[:, None] == (jnp.arange(WIDTH) // M_HEAD)[None, :]).astype(BF16))
    tail = (_row(jnp.repeat(m_d, M_HEAD, axis=1)), _row(m_norm_w), expand)
    w["mamba_seq"] = (gn, w_m) + conv + (alog_row, m_a_log.reshape(depth, M_HEADS, 1).astype(F32)) + tail
    w["mamba_step"] = conv + (alog_row,) + tail
    w["merge"] = (gn, _bf(w_in[:, :, c3:]), _bf(w_branch), _bf(w_out), _row(g_post_mix))
    w["ffn"] = (_row(g_pre_ffn), _bf(w_up), jnp.swapaxes(f_conv_w, 1, 2).astype(F32), _row(f_conv_b), _bf(w_down),
                _row(g_post_ffn))
    return w


def _zero_states(n):
    return dict(shift=jnp.zeros((n, SUBLANES, R_COLS), F32),
                wkv=jnp.zeros((n, R_HEADS, R_HEAD, R_HEAD), F32),
                s5=jnp.zeros((n, S5_LANES), F32),
                ssd=jnp.zeros((n, M_HEADS, M_HEAD, M_STATE), F32),
                mconv=jnp.zeros((n, SUBLANES, M_CONV_CH), F32),
                fconv=jnp.zeros((n, SUBLANES, D_FF), F32))


def _sequence_layer(x, init, w, layer):
    bsz, t, d = x.shape
    flat = lambda a: a.reshape(bsz * t, a.shape[-1])
    o_r, last_r, wkv1 = _rwkv_seq(x, init["shift"], init["wkv"], layer, w["rwkv_seq"])
    o_s, s5r1, s5i1 = _s5(x, init["s5"], init["s5"], layer, w["s5"], True)
    o_m, last_m, ssd1 = _mamba_seq(x, init["mconv"], init["ssd"], layer, w["mamba_seq"])
    mixed = _merge(flat(x), flat(o_r), flat(o_s), flat(o_m), layer, w["merge"])
    x, last_f = _ffn(mixed.reshape(bsz, t, d), init["fconv"], layer, w["ffn"], True)
    states = (last_r[:, SUBLANES - 1], wkv1, s5r1.reshape(bsz, S5_GROUPS, S5_STATE),
              s5i1.reshape(bsz, S5_GROUPS, S5_STATE), ssd1, last_m[:, SUBLANES - (M_CONV - 1):],
              last_f[:, SUBLANES - (FFN_CONV - 1):])
    return x, states


def _step_layer(x, cache, w, layer):
    shift_all, wkv_all, s5r_all, s5i_all, ssd_all, mconv_all, fconv_all = cache
    z_r = _norm_matmul(x, w["g_pre_mix"], w["w_r"], layer, "in_proj_rwkv")
    z_m = _norm_matmul(x, w["g_pre_mix"], w["w_m"], layer, "in_proj_mamba")
    o_r, wkv1 = _rwkv_step(z_r, shift_all, wkv_all, layer, w["rwkv_step"])
    o_s, s5r1, s5i1 = _s5(x, s5r_all, s5i_all, layer, w["s5"], False)
    o_m, ssd1 = _mamba_step(z_m, mconv_all, ssd_all, layer, w["mamba_step"])
    x = _merge(x, o_r, o_s, o_m, layer, w["merge"])
    x, gate = _ffn(x, fconv_all, layer, w["ffn"], False)
    return x, (z_r, wkv1, s5r1, s5i1, ssd1, z_m[:, :M_CONV_CH], gate)


def kernel(x_prompt, x_sample, state_rwkv_shift, state_rwkv_wkv, state_s5_re, state_s5_im, state_ssd, state_ssd_conv, state_ffn_conv, g_pre_mix, g_post_mix, g_pre_ffn, g_post_ffn, w_in, r_mu, r_w0, r_w2, r_a0, r_a2, r_g2, r_kk, r_ka, r_rk, r_ln_w, r_ln_b, s5_lam_re, s5_lam_im, s5_log_step, s5_b_re, s5_b_im, s5_c_re, s5_c_im, s5_d, s5_w_glu, s5_b_glu, m_conv_w, m_conv_b, m_dt_bias, m_a_log, m_d, m_norm_w, w_branch, w_out, w_up, f_conv_w, f_conv_b, w_down):
    stacked = (g_pre_mix, g_post_mix, g_pre_ffn, g_post_ffn, w_in,
               r_mu, r_w0, r_w2, r_a0, r_a2, r_g2, r_kk, r_ka, r_rk, r_ln_w, r_ln_b,
               s5_lam_re, s5_lam_im, s5_log_step, s5_b_re, s5_b_im, s5_c_re, s5_c_im, s5_d,
               s5_w_glu, s5_b_glu,
               m_conv_w, m_conv_b, m_dt_bias, m_a_log, m_d, m_norm_w,
               w_branch, w_out, w_up, f_conv_w, f_conv_b, w_down)
    depth = w_in.shape[0]
    pb, pt, d = x_prompt.shape
    sb, s_t, _ = x_sample.shape
    w = _all_layer_weights(stacked)
    per_seq = lambda a: a.reshape(depth, sb, -1)
    cache = (state_rwkv_shift, state_rwkv_wkv, per_seq(state_s5_re), per_seq(state_s5_im), state_ssd,
             per_seq(state_ssd_conv), per_seq(state_ffn_conv))
    init = _zero_states(pb)
    xp = x_prompt
    xs = x_sample.reshape(sb * s_t, d)
    p_states, s_states = [], []
    for l in range(depth):
        xp, sp = _sequence_layer(xp, init, w, l)
        xs, ss = _step_layer(xs, cache, w, l)
        p_states.append(sp)
        s_states.append(ss)
    p_shift, p_wkv, p_s5_re, p_s5_im, p_ssd, p_ssd_conv, p_ffn_conv = (jnp.stack(v, 0) for v in zip(*p_states))
    z_r, s_wkv, s5r, s5i, s_ssd, raw_xbc, gate = (jnp.stack(v, 0) for v in zip(*s_states))
    s_s5_re = s5r.reshape(depth, sb, S5_GROUPS, S5_STATE)
    s_s5_im = s5i.reshape(depth, sb, S5_GROUPS, S5_STATE)
    s_ssd_conv = jnp.concatenate([state_ssd_conv[:, :, 1:], raw_xbc[:, :, None]], axis=2)
    s_ffn_conv = jnp.concatenate([state_ffn_conv[:, :, 1:], gate[:, :, None]], axis=2)
    return (xp, xs.reshape(sb, s_t, d), p_shift, z_r, p_wkv, s_wkv, p_s5_re, s_s5_re, p_s5_im, s_s5_im,
            p_ssd, s_ssd, p_ssd_conv, s_ssd_conv, p_ffn_conv, s_ffn_conv)
```

```python
import functools

import jax
import jax.numpy as jnp
from jax import lax
from jax.experimental import pallas as pl
from jax.experimental.pallas import tpu as pltpu

F32 = jnp.float32
BF16 = jnp.bfloat16

D_MODEL = 1024
WIDTH = 512
R_HEADS, R_HEAD = 8, 64
R_COLS = 1792
R_LN_EPS = 64e-5
DECAY_SCALE = 0.6065306597126334
S5_GROUPS, S5_GROUP, S5_STATE = 32, 16, 64
S5_LANES = S5_GROUPS * S5_STATE
M_HEADS, M_HEAD, M_GROUPS, M_STATE = 8, 64, 2, 128
M_CONV, M_CONV_CH = 4, 1024
M_DT_PAD = 256
M_COLS_PAD = M_CONV_CH + WIDTH + M_DT_PAD
M_EPS = 1e-5
D_FF = 2816
FFN_CONV = 3
EPS = 1e-6
SUBLANES = 8
LANES = 128

RWKV_CHUNK = 64
RWKV_TILE = 256
SSD_CHUNK = 128
S5_CHUNK = 128
S5_ROW_PAD = 8
S5_SCAN_TILES = 8
SSD_TILE = 256
BRANCH_DTYPE = BF16
TOKEN_TILE = 512
FFN_COL_STARTS = (0, 1536, D_FF)
VMEM_LIMIT = 56 * 1024 * 1024


def _cp(sem):
    return pltpu.CompilerParams(dimension_semantics=sem, vmem_limit_bytes=VMEM_LIMIT)


def _layer_const(w, layer):
    shape = w.shape[1:]
    nd = len(shape)
    return pl.BlockSpec((None,) + shape, lambda *_: (layer,) + (0,) * nd, pipeline_mode=pl.Buffered(1))


def _full(shape):
    nd = len(shape)
    return pl.BlockSpec(shape, lambda *_: (0,) * nd)


def _bf(x):
    return x.astype(BF16)


def _dot(a, b):
    return jnp.dot(_bf(a), _bf(b), preferred_element_type=F32)


def _dot_nt(a, b):
    return lax.dot_general(_bf(a), _bf(b), (((1,), (1,)), ((), ())), preferred_element_type=F32)


def _dot_tn(a, b):
    return lax.dot_general(_bf(a), _bf(b), (((0,), (0,)), ((), ())), preferred_element_type=F32)


def _split(x, terms):
    out = []
    for _ in range(terms - 1):
        h = _bf(x)
        out.append(h)
        x = x - h.astype(F32)
    out.append(_bf(x))
    return out


def _dot_exact_lhs(m_bf16, x, terms=3):
    acc = None
    for h in _split(x, terms):
        p = jnp.dot(m_bf16, h, preferred_element_type=F32)
        acc = p if acc is None else acc + p
    return acc


def _dot_exact_rhs(x, m_bf16, terms=2):
    acc = None
    for h in _split(x, terms):
        p = jnp.dot(h, m_bf16, preferred_element_type=F32)
        acc = p if acc is None else acc + p
    return acc


def _softplus(x):
    return jnp.maximum(x, 0.0) + jnp.log1p(jnp.exp(-jnp.abs(x)))


def _rms(x, g, eps=EPS):
    return x * lax.rsqrt(jnp.mean(x * x, axis=-1, keepdims=True) + eps) * g


def _delayed(x, carry8, k):
    rx = pltpu.roll(x, k, 0)
    rc = pltpu.roll(carry8, k, 0)
    row = lax.broadcasted_iota(jnp.int32, (SUBLANES, x.shape[1]), 0)
    head = jnp.where(row < k, rc, rx[:SUBLANES])
    if x.shape[0] == SUBLANES:
        return head
    return jnp.concatenate([head, rx[SUBLANES:]], axis=0)


def _row_pairs(x, *parts):
    hi = _bf(x).astype(F32)
    part = {"hi": hi, "lo": x - hi}
    row = lax.broadcasted_iota(jnp.int32, x.shape, 0)
    out = jnp.zeros_like(x)
    for n, name in enumerate(parts):
        src = part[name] if n == 0 else pltpu.roll(part[name], 2 * n, 0)
        out = jnp.where((row >= 2 * n) & (row < 2 * n + 2), src, out)
    return _bf(out)


def _row_group(t, rows):
    start = t * rows
    return pl.multiple_of(start, SUBLANES) if rows % SUBLANES == 0 else start


def _rows(first, second):
    row = lax.broadcasted_iota(jnp.int32, (SUBLANES, first.shape[1]), 0)
    return jnp.where(row == 0, first, jnp.where(row == 1, second, 0.0))


def _norm_matmul_kernel(x_ref, g_ref, w_ref, o_ref):
    o_ref[...] = jnp.dot(_bf(_rms(x_ref[...], g_ref[...])), w_ref[...], preferred_element_type=F32)


def _norm_matmul(x, g, w, layer, name):
    n, d = x.shape
    c = w.shape[-1]
    return pl.pallas_call(
        _norm_matmul_kernel,
        grid=(1,),
        in_specs=[_full((n, d)), _layer_const(g, layer), _layer_const(w, layer)],
        out_specs=_full((n, c)),
        out_shape=jax.ShapeDtypeStruct((n, c), F32),
        compiler_params=_cp(("arbitrary",)),
        name=name,
    )(x, g, w)


def _rwkv_prep_math(z, prev, mu, w0, w2, a0, a2, g2, kkw, kaw, ones_bd):
    zm = z + (prev - z) * mu
    r = zm[:, 0:WIDTH]
    k = zm[:, WIDTH:2 * WIDTH]
    v = zm[:, 2 * WIDTH:3 * WIDTH]
    dw = zm[:, 1536:1600]
    da = zm[:, 1600:1664]
    dg = zm[:, 1664:1792]
    ld = -DECAY_SCALE * jax.nn.sigmoid(w0 + _dot(jnp.tanh(dw), w2))
    a = jax.nn.sigmoid(a0 + _dot(da, a2))
    g = _dot(jax.nn.sigmoid(dg), g2)
    kk = k * kkw
    ss = _dot_exact_rhs(kk * kk, ones_bd)
    kk = kk * lax.rsqrt(jnp.maximum(ss, 1e-24))
    k2 = k * (1.0 + (a - 1.0) * kaw)
    return r, ld, k2, v, kk, kk * a, g


def _rwkv_post(y, r, k, v, g, rk, lnw, lnb, ones_bd):
    inv = 1.0 / R_HEAD
    mean = _dot_exact_rhs(y, ones_bd) * inv
    d = y - mean
    var = _dot_exact_rhs(d * d, ones_bd) * inv
    yn = d * lax.rsqrt(var + R_LN_EPS) * lnw + lnb
    bonus = _dot_exact_rhs(r * k * rk, ones_bd) * v
    return (yn + bonus) * g


def _rwkv_chunks(r, ld, k, v, kk, b, s_scr):
    L = RWKV_CHUNK
    pair = 2 * R_HEAD
    n_pairs = R_HEADS // 2
    tm = r.shape[0]
    n_chunks = tm // L
    trow = lax.broadcasted_iota(jnp.int32, (tm, tm), 0)
    tcol = lax.broadcasted_iota(jnp.int32, (tm, tm), 1)
    same_chunk = (trow // L) == (tcol // L)
    cum = _dot_exact_lhs(_bf(((trow >= tcol) & same_chunk).astype(F32)), ld)
    wc = jnp.exp(cum)
    winv = jnp.exp(-cum)
    r_t = r * wc
    kk_t = kk * jnp.exp(cum - ld)
    k_h = k * winv
    b_h = b * winv

    row = lax.broadcasted_iota(jnp.int32, (pair, pair), 0)
    col = lax.broadcasted_iota(jnp.int32, (pair, pair), 1)
    same_head = (row // R_HEAD) == (col // R_HEAD)
    lrow = lax.broadcasted_iota(jnp.int32, (L, pair), 0)
    lcol = lax.broadcasted_iota(jnp.int32, (L, pair), 1) % R_HEAD
    strict = lrow > lcol
    lower = lrow >= lcol

    def bd(x):
        xb = _bf(x)
        return jnp.where(same_head, jnp.concatenate([xb, xb], axis=0), jnp.zeros((), BF16))

    units = [(c, p) for c in range(n_chunks) for p in range(n_pairs)]
    pre = {}
    for c, p in units:
        rows = slice(c * L, (c + 1) * L)
        lanes = slice(p * pair, (p + 1) * pair)
        wl = wc[(c + 1) * L - 1:(c + 1) * L, lanes]
        pre[c, p] = dict(kkt=kk_t[rows, lanes], rt=r_t[rows, lanes], kh=k_h[rows, lanes], bh=b_h[rows, lanes],
                         v=v[rows, lanes], vbd=bd(v[rows, lanes]), wl=wl)
    for u in units:
        d = pre[u]
        a = _dot_nt(jnp.concatenate([d["kkt"], d["rt"]], axis=0),
                    jnp.concatenate([bd(d["kh"]), bd(d["bh"])], axis=0))
        d["akk_k"] = jnp.where(strict, a[:L, :pair], 0.0)
        d["n"] = jnp.where(strict, a[:L, pair:], 0.0)
        d["ar_k"] = jnp.where(lower, a[L:, :pair], 0.0)
        d["ar_b"] = jnp.where(lower, a[L:, pair:], 0.0)
        d["q"] = -d["n"]
        d["m"] = d["n"]
    power = 2
    while power < L:
        for u in units:
            d = pre[u]
            d["m"] = _dot(d["m"], bd(d["m"]))
            d["q"] = d["q"] + d["m"] + _dot(d["q"], bd(d["m"]))
        power *= 2
    for u in units:
        d = pre[u]
        xy = _dot(jnp.concatenate([d["akk_k"], d["ar_k"]], axis=0), d["vbd"])
        x, d["y0"] = xy[:L], xy[L:]
        both = jnp.concatenate([d["kkt"], x], axis=1)
        both = both + _dot(d["q"], jnp.concatenate([bd(d["kkt"]), bd(x)], axis=1))
        d["g"], d["u0"] = both[:, :pair], both[:, pair:]
    for u in units:
        d = pre[u]
        t = _dot(d["ar_b"], jnp.concatenate([bd(d["g"]), bd(d["u0"])], axis=1))
        d["ry"] = d["rt"] - t[:, :pair]
        d["y0"] = d["y0"] - t[:, pair:]
        kw = d["kh"] * d["wl"]
        bw = d["bh"] * d["wl"]
        d["pm"] = jnp.where(same_head, _dot_tn(d["g"], bw), 0.0)
        d["c"] = jnp.where(same_head, _dot_tn(jnp.concatenate([d["v"], -d["u0"]], axis=0),
                                               jnp.concatenate([kw, bw], axis=0)), 0.0)
    ys = []
    for c in range(n_chunks):
        parts = []
        for p in range(n_pairs):
            d = pre[c, p]
            s = s_scr[p]
            parts.append(_dot_nt(d["ry"], s) + d["y0"])
            s_scr[p] = s * d["wl"] - _dot(s, d["pm"]) + d["c"]
        ys.append(jnp.concatenate(parts, axis=1))
    return jnp.concatenate(ys, axis=0)


def _rwkv_seq_kernel(x_ref, init_ref, s0_ref, gn_ref, wr_ref, mu_ref, w0_ref, w2_ref, a0_ref, a2_ref, g2_ref,
                     kkw_ref, kaw_ref, ones_ref, rk_ref, lnw_ref, lnb_ref, o_ref, last_ref, st_ref, carry, s_scr):
    j = pl.program_id(1)
    n_pairs = R_HEADS // 2

    @pl.when(j == 0)
    def _():
        carry[...] = init_ref[...]
        zero = jnp.zeros((R_HEAD, R_HEAD), F32)
        for p in range(n_pairs):
            top = jnp.concatenate([s0_ref[2 * p], zero], axis=1)
            bot = jnp.concatenate([zero, s0_ref[2 * p + 1]], axis=1)
            s_scr[p] = jnp.concatenate([top, bot], axis=0)

    z = jnp.dot(_bf(_rms(x_ref[...], gn_ref[...])), wr_ref[...], preferred_element_type=F32)
    prev = _delayed(z, carry[...], 1)
    tail = z[z.shape[0] - SUBLANES:, :]
    carry[...] = tail
    last_ref[...] = tail
    ones_bd = ones_ref[...]
    r, ld, k, v, kk, b, g = _rwkv_prep_math(z, prev, mu_ref[...], w0_ref[...], w2_ref[...], a0_ref[...],
                                            a2_ref[...], g2_ref[...], kkw_ref[...], kaw_ref[...], ones_bd)
    y = _rwkv_chunks(r, ld, k, v, kk, b, s_scr)
    o_ref[...] = _rwkv_post(y, r, k, v, g, rk_ref[...], lnw_ref[...], lnb_ref[...], ones_bd).astype(o_ref.dtype)

    @pl.when(j == pl.num_programs(1) - 1)
    def _():
        for p in range(n_pairs):
            s = s_scr[p]
            st_ref[2 * p] = s[:R_HEAD, :R_HEAD]
            st_ref[2 * p + 1] = s[R_HEAD:, R_HEAD:]


def _rwkv_seq(x, shift, s0, layer, weights):
    bsz, t, d = x.shape
    tm = min(RWKV_TILE, t)
    tile = lambda c: pl.BlockSpec((None, tm, c), lambda b, j: (b, j, 0))
    edge = pl.BlockSpec((None, SUBLANES, R_COLS), lambda b, j: (b, 0, 0))
    st = pl.BlockSpec((None, R_HEADS, R_HEAD, R_HEAD), lambda b, j: (b, 0, 0, 0))
    return pl.pallas_call(
        _rwkv_seq_kernel,
        grid=(bsz, t // tm),
        in_specs=[tile(d), edge, st] + [_layer_const(w, layer) for w in weights],
        out_specs=[tile(WIDTH), edge, st],
        out_shape=[jax.ShapeDtypeStruct((bsz, t, WIDTH), BRANCH_DTYPE),
                   jax.ShapeDtypeStruct((bsz, SUBLANES, R_COLS), F32),
                   jax.ShapeDtypeStruct((bsz, R_HEADS, R_HEAD, R_HEAD), F32)],
        scratch_shapes=[pltpu.VMEM((SUBLANES, R_COLS), F32),
                        pltpu.VMEM((R_HEADS // 2, 2 * R_HEAD, 2 * R_HEAD), F32)],
        compiler_params=_cp(("arbitrary", "arbitrary")),
        name="rwkv_seq",
    )(x, shift, s0, *weights)


def _rwkv_step_kernel(z_ref, prev_ref, s0_ref, mu_ref, w0_ref, w2_ref, a0_ref, a2_ref, g2_ref, kkw_ref, kaw_ref,
                      ones_ref, rk_ref, lnw_ref, lnb_ref, o_ref, st_ref, y_scr):
    bt = z_ref.shape[0]
    ones_bd = ones_ref[...]
    r, ld, k, v, kk, b, g = _rwkv_prep_math(z_ref[...], prev_ref[...], mu_ref[...], w0_ref[...], w2_ref[...],
                                            a0_ref[...], a2_ref[...], g2_ref[...], kkw_ref[...], kaw_ref[...],
                                            ones_bd)
    w = jnp.exp(ld)
    wr = w * r
    b_dot_r = _dot_exact_rhs(b * r, ones_bd)
    k_dot_r = _dot_exact_rhs(k * r, ones_bd)

    projs = []
    for i in range(bt):
        one = slice(i, i + 1)
        lhs = _rows(kk[one], wr[one])
        projs.append(jnp.concatenate(
            [_dot_nt(lhs[:, h * R_HEAD:(h + 1) * R_HEAD], s0_ref[i, h]) for h in range(R_HEADS)], axis=1))
    pieces = []
    for i in range(bt):
        one = slice(i, i + 1)
        sa = projs[i][0:1]
        y_scr[one, :] = projs[i][1:2] - sa * b_dot_r[one] + v[one] * k_dot_r[one]
        pieces.append((_row_pairs(_rows(v[one], -sa), "hi", "hi", "lo"), _row_pairs(_rows(k[one], b[one]), "hi", "lo", "hi")))
    for i in range(bt):
        one = slice(i, i + 1)
        left, right = pieces[i]
        for h in range(R_HEADS):
            cs = slice(h * R_HEAD, (h + 1) * R_HEAD)
            upd = lax.dot_general(left[:, cs], right[:, cs], (((0,), (0,)), ((), ())), preferred_element_type=F32)
            st_ref[i, h] = s0_ref[i, h] * w[one, cs] + upd
    o_ref[...] = _rwkv_post(y_scr[...], r, k, v, g, rk_ref[...], lnw_ref[...], lnb_ref[...], ones_bd)


def _rwkv_step(z, shift_all, s_all, layer, weights):
    n = z.shape[0]
    bt = SUBLANES
    tile = lambda c: pl.BlockSpec((bt, c), lambda i: (i, 0))
    prev = pl.BlockSpec((None, bt, R_COLS), lambda i: (layer, i, 0))
    st_out = pl.BlockSpec((bt, R_HEADS, R_HEAD, R_HEAD), lambda i: (i, 0, 0, 0))
    st_in = st_out
    return pl.pallas_call(
        _rwkv_step_kernel,
        grid=(n // bt,),
        in_specs=[tile(R_COLS), prev, st_in] + [_layer_const(w, layer) for w in weights],
        out_specs=[tile(WIDTH), st_out],
        out_shape=[jax.ShapeDtypeStruct((n, WIDTH), F32), jax.ShapeDtypeStruct(s_all.shape[1:], F32)],
        scratch_shapes=[pltpu.VMEM((bt, WIDTH), F32)],
        compiler_params=_cp(("arbitrary",)),
        name="rwkv_step",
    )(z, shift_all, s_all[layer], *weights)


S5_BLOCKS = 4
S5_BLOCK_IN = WIDTH // S5_BLOCKS
S5_BLOCK_STATE = S5_LANES // S5_BLOCKS
S5_BLOCK_TILES = S5_BLOCK_STATE // LANES


def _s5_param_kernel(lr_ref, li_ref, ls_ref, br_ref, bi_ref, abr_ref, abi_ref, bbr_ref, bbi_ref):
    lr, li = lr_ref[...], li_ref[...]
    delta = jnp.exp(ls_ref[...])
    mag = jnp.exp(lr * delta)
    ab_re = mag * jnp.cos(li * delta)
    ab_im = mag * jnp.sin(li * delta)
    den = lr * lr + li * li
    nr, ni = ab_re - 1.0, ab_im
    cf_re = (nr * lr + ni * li) / den
    cf_im = (ni * lr - nr * li) / den
    abr_ref[...] = ab_re
    abi_ref[...] = ab_im
    br, bi = br_ref[...], bi_ref[...]
    bbr_ref[...] = cf_re[:, None, :] * br - cf_im[:, None, :] * bi
    bbi_ref[...] = cf_re[:, None, :] * bi + cf_im[:, None, :] * br


def _s5_params(lam_re, lam_im, log_step, b_re, b_im):
    depth, g, p, h = b_re.shape
    gp = jax.ShapeDtypeStruct((depth, g, p), F32)
    ghp = jax.ShapeDtypeStruct((depth, g, h, p), F32)
    args = (lam_re, lam_im, log_step.reshape(depth, g, 1), jnp.swapaxes(b_re, 2, 3), jnp.swapaxes(b_im, 2, 3))
    per_layer = lambda shape: pl.BlockSpec((None,) + shape, lambda l: (l,) + (0,) * len(shape))
    return pl.pallas_call(
        _s5_param_kernel,
        grid=(depth,),
        in_specs=[per_layer(a.shape[1:]) for a in args],
        out_specs=[per_layer((g, p)), per_layer((g, p)), per_layer((g, h, p)), per_layer((g, h, p))],
        out_shape=[gp, gp, ghp, ghp],
        compiler_params=_cp(("arbitrary",)),
        name="s5_params",
    )(*args)


def _s5_drive(ub, wbr_ref, wbi_ref, blk):
    cols = slice(blk * S5_BLOCK_IN, (blk + 1) * S5_BLOCK_IN)
    return (jnp.dot(ub[:, cols], wbr_ref[blk], preferred_element_type=F32),
            jnp.dot(ub[:, cols], wbi_ref[blk], preferred_element_type=F32))


def _s5_readout(u, xr_blocks, xi_blocks, wcr_ref, wci_ref, d, wg, bg):
    y = jnp.concatenate([_dot(xr_blocks[q], wcr_ref[q]) - _dot(xi_blocks[q], wci_ref[q])
                         for q in range(S5_BLOCKS)], axis=1) + d * u
    y = jax.nn.gelu(y)
    return y * jax.nn.sigmoid(_dot(y, wg) + bg)


def _s5_seq_kernel(x_ref, s0r_ref, s0i_ref, gn_ref, ws_ref, wbr_ref, wbi_ref, wcr_ref, wci_ref, ar_ref, ai_ref,
                   d_ref, wg_ref, bg_ref, o_ref, str_ref, sti_ref, xr_scr, xi_scr, cr_scr, ci_scr, seq_scr, tm_scr,
                   *, bt, tc):
    @pl.when(pl.program_id(0) == 0)
    def _():
        cr_scr[...] = s0r_ref[...]
        ci_scr[...] = s0i_ref[...]

    pitch = tc + S5_ROW_PAD
    in_tiles = WIDTH // LANES
    n_tiles = S5_LANES // LANES
    x = x_ref[...].reshape(bt * tc, D_MODEL)
    u_seq = jnp.dot(_bf(_rms(x, gn_ref[...])), ws_ref[...], preferred_element_type=F32)
    for c in range(in_tiles):
        for s in range(bt):
            seq_scr[c, s * pitch:s * pitch + tc, :] = u_seq[s * tc:(s + 1) * tc, c * LANES:(c + 1) * LANES]

    def to_time_major(t, carry):
        dst = pl.ds(_row_group(t, bt), bt)
        for c in range(in_tiles):
            tm_scr[c, dst, :] = seq_scr[c, pl.ds(t, bt, stride=pitch), :]
        return carry

    lax.fori_loop(0, tc, to_time_major, 0, unroll=2)
    u = jnp.concatenate([tm_scr[c] for c in range(in_tiles)], axis=1)
    ub = _bf(u)
    for q in range(S5_BLOCKS):
        bu_re, bu_im = _s5_drive(ub, wbr_ref, wbi_ref, q)
        for c in range(S5_BLOCK_TILES):
            xr_scr[q * S5_BLOCK_TILES + c] = bu_re[:, c * LANES:(c + 1) * LANES]
            xi_scr[q * S5_BLOCK_TILES + c] = bu_im[:, c * LANES:(c + 1) * LANES]

    for g0 in range(0, n_tiles, S5_SCAN_TILES):
        tiles = range(g0, g0 + S5_SCAN_TILES)
        ar = [jnp.broadcast_to(ar_ref[:, c * LANES:(c + 1) * LANES], (bt, LANES)) for c in tiles]
        ai = [jnp.broadcast_to(ai_ref[:, c * LANES:(c + 1) * LANES], (bt, LANES)) for c in tiles]

        def body(t, carry, tiles=tiles, ar=ar, ai=ai):
            rows = pl.ds(_row_group(t, bt), bt)
            out = []
            for n, c in enumerate(tiles):
                xr, xi = carry[2 * n], carry[2 * n + 1]
                nr = ar[n] * xr - ai[n] * xi + xr_scr[c, rows, :]
                ni = ar[n] * xi + ai[n] * xr + xi_scr[c, rows, :]
                xr_scr[c, rows, :] = nr
                xi_scr[c, rows, :] = ni
                out += [nr, ni]
            return tuple(out)

        init = []
        for c in tiles:
            init += [cr_scr[:, c * LANES:(c + 1) * LANES], ci_scr[:, c * LANES:(c + 1) * LANES]]
        fin = lax.fori_loop(0, tc, body, tuple(init), unroll=2)
        for n, c in enumerate(tiles):
            cr_scr[:, c * LANES:(c + 1) * LANES] = fin[2 * n]
            ci_scr[:, c * LANES:(c + 1) * LANES] = fin[2 * n + 1]
    str_ref[...] = cr_scr[...]
    sti_ref[...] = ci_scr[...]

    blocks = lambda scr: [jnp.concatenate([scr[q * S5_BLOCK_TILES + c] for c in range(S5_BLOCK_TILES)], axis=1)
                          for q in range(S5_BLOCKS)]
    y = _s5_readout(u, blocks(xr_scr), blocks(xi_scr), wcr_ref, wci_ref, d_ref[...], wg_ref[...], bg_ref[...])
    for c in range(in_tiles):
        tm_scr[c] = y[:, c * LANES:(c + 1) * LANES]

    def to_sequence_major(t, carry):
        src = pl.ds(_row_group(t, bt), bt)
        for c in range(in_tiles):
            seq_scr[c, pl.ds(t, bt, stride=pitch), :] = tm_scr[c, src, :]
        return carry

    lax.fori_loop(0, tc, to_sequence_major, 0, unroll=2)
    for s in range(bt):
        o_ref[s] = jnp.concatenate([seq_scr[c, s * pitch:s * pitch + tc, :] for c in range(in_tiles)],
                                   axis=1).astype(o_ref.dtype)


def _s5_step_kernel(x_ref, s0r_ref, s0i_ref, gn_ref, ws_ref, wbr_ref, wbi_ref, wcr_ref, wci_ref, ar_ref, ai_ref,
                    d_ref, wg_ref, bg_ref, o_ref, str_ref, sti_ref):
    u = jnp.dot(_bf(_rms(x_ref[...], gn_ref[...])), ws_ref[...], preferred_element_type=F32)
    ub = _bf(u)
    xr_blocks, xi_blocks = [], []
    for q in range(S5_BLOCKS):
        lanes = slice(q * S5_BLOCK_STATE, (q + 1) * S5_BLOCK_STATE)
        ar, ai = ar_ref[:, lanes], ai_ref[:, lanes]
        sr, si = s0r_ref[:, lanes], s0i_ref[:, lanes]
        bu_re, bu_im = _s5_drive(ub, wbr_ref, wbi_ref, q)
        xr = ar * sr - ai * si + bu_re
        xi = ar * si + ai * sr + bu_im
        str_ref[:, lanes] = xr
        sti_ref[:, lanes] = xi
        xr_blocks.append(xr)
        xi_blocks.append(xi)
    o_ref[...] = _s5_readout(u, xr_blocks, xi_blocks, wcr_ref, wci_ref, d_ref[...], wg_ref[...], bg_ref[...])


def _s5(x, s0r, s0i, layer, weights, seq):
    wspecs = [_layer_const(w, layer) for w in weights]
    if seq:
        bsz, t, d = x.shape
        tc = min(S5_CHUNK, t)
        st = _full((bsz, S5_LANES))
        return pl.pallas_call(
            functools.partial(_s5_seq_kernel, bt=bsz, tc=tc),
            grid=(t // tc,),
            in_specs=[pl.BlockSpec((bsz, tc, d), lambda j: (0, j, 0)), st, st] + wspecs,
            out_specs=[pl.BlockSpec((bsz, tc, WIDTH), lambda j: (0, j, 0)), st, st],
            out_shape=[jax.ShapeDtypeStruct((bsz, t, WIDTH), BRANCH_DTYPE)]
            + [jax.ShapeDtypeStruct((bsz, S5_LANES), F32)] * 2,
            scratch_shapes=[pltpu.VMEM((S5_LANES // LANES, bsz * tc, LANES), F32)] * 2
            + [pltpu.VMEM((bsz, S5_LANES), F32)] * 2
            + [pltpu.VMEM((WIDTH // LANES, bsz * (tc + S5_ROW_PAD), LANES), F32),
               pltpu.VMEM((WIDTH // LANES, bsz * tc, LANES), F32)],
            compiler_params=_cp(("arbitrary",)),
            name="s5_seq",
        )(x, s0r, s0i, *weights)
    n, d = x.shape
    st = _full((n, S5_LANES))
    st_in = pl.BlockSpec((None, n, S5_LANES), lambda i: (layer, 0, 0))
    return pl.pallas_call(
        _s5_step_kernel,
        grid=(1,),
        in_specs=[_full((n, d)), st_in, st_in] + wspecs,
        out_specs=[_full((n, WIDTH)), st, st],
        out_shape=[jax.ShapeDtypeStruct((n, WIDTH), F32)] + [jax.ShapeDtypeStruct((n, S5_LANES), F32)] * 2,
        compiler_params=_cp(("arbitrary",)),
        name="s5_step",
    )(x, s0r, s0i, *weights)


def _mamba_prep_math(xbc_taps, dt_raw, cw, cb, dtb):
    acc = xbc_taps[M_CONV - 1] * cw[0:1]
    for kk in range(1, M_CONV):
        acc = acc + xbc_taps[M_CONV - 1 - kk] * cw[kk:kk + 1]
    acc = acc + cb
    return acc * jax.nn.sigmoid(acc), _softplus(dt_raw + dtb)


def _ssd_post(y, z, nw):
    y = y * (z * jax.nn.sigmoid(z))
    half = WIDTH // M_GROUPS
    parts = []
    for gi in range(M_GROUPS):
        yg = y[:, gi * half:(gi + 1) * half]
        parts.append(yg * lax.rsqrt(jnp.mean(yg * yg, axis=-1, keepdims=True) + M_EPS))
    return jnp.concatenate(parts, axis=1) * nw


def _ssd_chunk_math(xbc, dt, alog_row, acol, dsk, expand, h_scr):
    q = xbc.shape[0]
    pair = 2 * M_HEAD
    xs = xbc[:, :WIDTH]
    row = lax.broadcasted_iota(jnp.int32, (q, q), 0)
    col = lax.broadcasted_iota(jnp.int32, (q, q), 1)
    lower = row >= col
    dt_full = _dot_exact_rhs(dt, expand, terms=3)
    cum = _dot_exact_lhs(_bf(lower.astype(F32)), dt_full * -jnp.exp(alog_row))
    da_t = jnp.transpose(dt[:, :LANES])[:SUBLANES] * -jnp.exp(acol)
    cum_t = _dot_exact_rhs(da_t, _bf((row <= col).astype(F32)), terms=3)
    ecum = jnp.exp(cum)
    cum_last = cum[q - 1:q, :]
    xd = xs * dt_full
    xdec = xd * jnp.exp(cum_last - cum)
    lane = lax.broadcasted_iota(jnp.int32, (q, pair), 1)
    prow = lax.broadcasted_iota(jnp.int32, (pair, M_STATE), 0)
    heads_per_group = M_HEADS // M_GROUPS
    ys = []
    for gi in range(M_GROUPS):
        bg = xbc[:, WIDTH + gi * M_STATE:WIDTH + (gi + 1) * M_STATE]
        cg = xbc[:, WIDTH + (M_GROUPS + gi) * M_STATE:WIDTH + (M_GROUPS + gi + 1) * M_STATE]
        cb = _dot_nt(cg, bg)
        for pp in range(heads_per_group // 2):
            p = gi * (heads_per_group // 2) + pp
            lanes = slice(p * pair, (p + 1) * pair)
            ms = []
            for h in (2 * p, 2 * p + 1):
                ch = cum[:, h * M_HEAD:(h + 1) * M_HEAD]
                seg = jnp.concatenate([ch] * (q // M_HEAD), axis=1) - cum_t[h:h + 1, :]
                ms.append(jnp.where(lower, jnp.exp(seg), 0.0) * cb)
            xd_pair = xd[:, lanes]
            stacked = jnp.concatenate([jnp.where(lane < M_HEAD, xd_pair, 0.0),
                                       jnp.where(lane >= M_HEAD, xd_pair, 0.0)], axis=0)
            hs = h_scr[p]
            ys.append(_dot(jnp.concatenate(ms, axis=1), stacked) + _dot_nt(cg, hs) * ecum[:, lanes])
            keep = jnp.where(prow < M_HEAD, jnp.exp(cum_t[2 * p:2 * p + 1, q - 1:q]),
                             jnp.exp(cum_t[2 * p + 1:2 * p + 2, q - 1:q]))
            h_scr[p] = hs * keep + _dot_tn(xdec[:, lanes], bg)
    return jnp.concatenate(ys, axis=1) + dsk * xs


def _mamba_seq_kernel(x_ref, init_ref, h0_ref, gn_ref, wm_ref, cw_ref, cb_ref, dtb_ref, alog_ref, acol_ref, dsk_ref,
                      nw_ref, expand_ref, o_ref, last_ref, ht_ref, carry, h_scr):
    j = pl.program_id(1)
    n_pairs = M_HEADS // 2

    @pl.when(j == 0)
    def _():
        carry[...] = init_ref[...]
        for p in range(n_pairs):
            h_scr[p] = jnp.concatenate([h0_ref[2 * p], h0_ref[2 * p + 1]], axis=0)

    zm = jnp.dot(_bf(_rms(x_ref[...], gn_ref[...])), wm_ref[...], preferred_element_type=F32)
    raw = zm[:, :M_CONV_CH]
    z = zm[:, M_CONV_CH:M_CONV_CH + WIDTH]
    c8 = carry[...]
    taps = [raw] + [_delayed(raw, c8, kk) for kk in range(1, M_CONV)]
    tail = raw[raw.shape[0] - SUBLANES:, :]
    carry[...] = tail
    last_ref[...] = tail
    xbc, dt = _mamba_prep_math(taps, zm[:, M_CONV_CH + WIDTH:], cw_ref[...], cb_ref[...], dtb_ref[...])
    q = min(SSD_CHUNK, xbc.shape[0])
    y = jnp.concatenate(
        [_ssd_chunk_math(xbc[r0:r0 + q], dt[r0:r0 + q], alog_ref[...], acol_ref[...], dsk_ref[...], expand_ref[...],
                         h_scr) for r0 in range(0, xbc.shape[0], q)], axis=0)
    o_ref[...] = _ssd_post(y, z, nw_ref[...]).astype(o_ref.dtype)

    @pl.when(j == pl.num_programs(1) - 1)
    def _():
        for p in range(n_pairs):
            hs = h_scr[p]
            ht_ref[2 * p] = hs[:M_HEAD]
            ht_ref[2 * p + 1] = hs[M_HEAD:]


def _mamba_seq(x, conv_state, h0, layer, weights):
    bsz, t, d = x.shape
    q = min(SSD_TILE, t)
    tile = lambda c: pl.BlockSpec((None, q, c), lambda b, j: (b, j, 0))
    edge = pl.BlockSpec((None, SUBLANES, M_CONV_CH), lambda b, j: (b, 0, 0))
    st = pl.BlockSpec((None, M_HEADS, M_HEAD, M_STATE), lambda b, j: (b, 0, 0, 0))
    return pl.pallas_call(
        _mamba_seq_kernel,
        grid=(bsz, t // q),
        in_specs=[tile(d), edge, st] + [_layer_const(w, layer) for w in weights],
        out_specs=[tile(WIDTH), edge, st],
        out_shape=[jax.ShapeDtypeStruct((bsz, t, WIDTH), BRANCH_DTYPE),
                   jax.ShapeDtypeStruct((bsz, SUBLANES, M_CONV_CH), F32),
                   jax.ShapeDtypeStruct((bsz, M_HEADS, M_HEAD, M_STATE), F32)],
        scratch_shapes=[pltpu.VMEM((SUBLANES, M_CONV_CH), F32),
                        pltpu.VMEM((M_HEADS // 2, 2 * M_HEAD, M_STATE), F32)],
        compiler_params=_cp(("arbitrary", "arbitrary")),
        name="mamba_seq",
    )(x, conv_state, h0, *weights)


def _mamba_step_kernel(zm_ref, past_ref, h0_ref, cw_ref, cb_ref, dtb_ref, alog_ref, dsk_ref, nw_ref, expand_ref,
                       o_ref, ht_ref, y_scr):
    bt = zm_ref.shape[0]
    zm = zm_ref[...]
    raw = zm[:, :M_CONV_CH]
    z = zm[:, M_CONV_CH:M_CONV_CH + WIDTH]
    past = past_ref[...]
    taps = [raw] + [past[:, (M_CONV - 1 - kk) * M_CONV_CH:(M_CONV - kk) * M_CONV_CH] for kk in range(1, M_CONV)]
    xbc, dt = _mamba_prep_math(taps, zm[:, M_CONV_CH + WIDTH:], cw_ref[...], cb_ref[...], dtb_ref[...])
    xs = xbc[:, :WIDTH]
    dt_full = _dot_exact_rhs(dt, expand_ref[...], terms=3)
    keep = jnp.exp(dt_full * -jnp.exp(alog_ref[...]))
    xd = xs * dt_full
    heads_per_group = M_HEADS // M_GROUPS
    zero_x = jnp.zeros((1, WIDTH), F32)
    zero_g = jnp.zeros((1, M_GROUPS * M_STATE), F32)
    new = {}
    for i in range(bt):
        one = slice(i, i + 1)
        xrow = _rows(xd[one], zero_x)
        brow = _rows(xbc[one, WIDTH:WIDTH + M_GROUPS * M_STATE], zero_g)
        for h in range(M_HEADS):
            gi = h // heads_per_group
            cs_ = slice(h * M_HEAD, (h + 1) * M_HEAD)
            kp = keep[one, cs_]
            hn = (h0_ref[i, h] * jnp.concatenate([kp, kp], axis=1)
                  + _dot_tn(xrow[:, cs_], brow[:, gi * M_STATE:(gi + 1) * M_STATE]))
            ht_ref[i, h] = hn
            new[i, h] = hn
    for i in range(bt):
        one = slice(i, i + 1)
        crow = _rows(xbc[one, WIDTH + M_GROUPS * M_STATE:], zero_g)
        outs = [_dot_nt(crow[:, (h // heads_per_group) * M_STATE:(h // heads_per_group + 1) * M_STATE], new[i, h])
                for h in range(M_HEADS)]
        y_scr[one, :] = jnp.concatenate(outs, axis=1)[0:1]
    y = y_scr[...] + dsk_ref[...] * xs
    o_ref[...] = _ssd_post(y, z, nw_ref[...])


def _mamba_step(zm, past_all, h_all, layer, weights):
    n = zm.shape[0]
    bt = SUBLANES
    tile = lambda c: pl.BlockSpec((bt, c), lambda i: (i, 0))
    past = pl.BlockSpec((None, bt, (M_CONV - 1) * M_CONV_CH), lambda i: (layer, i, 0))
    st_out = pl.BlockSpec((bt, M_HEADS, M_HEAD, M_STATE), lambda i: (i, 0, 0, 0))
    st_in = st_out
    return pl.pallas_call(
        _mamba_step_kernel,
        grid=(n // bt,),
        in_specs=[tile(M_COLS_PAD), past, st_in] + [_layer_const(w, layer) for w in weights],
        out_specs=[tile(WIDTH), st_out],
        out_shape=[jax.ShapeDtypeStruct((n, WIDTH), F32), jax.ShapeDtypeStruct(h_all.shape[1:], F32)],
        scratch_shapes=[pltpu.VMEM((bt, WIDTH), F32)],
        compiler_params=_cp(("arbitrary",)),
        name="mamba_step",
    )(zm, past_all, h_all[layer], *weights)


def _merge_kernel(x_ref, or_ref, os_ref, om_ref, gn_ref, wg_ref, wb_ref, wo_ref, gp_ref, o_ref):
    x = x_ref[...]
    hb = _bf(_rms(x, gn_ref[...]))
    mixed = None
    for kk, ref in enumerate((or_ref, os_ref, om_ref)):
        gate = jax.nn.sigmoid(jnp.dot(hb, wg_ref[:, kk * D_MODEL:(kk + 1) * D_MODEL], preferred_element_type=F32))
        term = gate * jnp.dot(_bf(ref[...]), wb_ref[kk], preferred_element_type=F32)
        mixed = term if mixed is None else mixed + term
    out = jnp.dot(_bf(mixed), wo_ref[...], preferred_element_type=F32)
    o_ref[...] = x + _rms(out, gp_ref[...])


def _merge(x, o_r, o_s, o_m, layer, weights):
    n = x.shape[0]
    tm = min(TOKEN_TILE, n)
    tile = lambda c: pl.BlockSpec((tm, c), lambda i: (i, 0))
    return pl.pallas_call(
        _merge_kernel,
        grid=(n // tm,),
        in_specs=[tile(D_MODEL)] + [tile(WIDTH)] * 3 + [_layer_const(w, layer) for w in weights],
        out_specs=tile(D_MODEL),
        out_shape=jax.ShapeDtypeStruct((n, D_MODEL), F32),
        compiler_params=_cp(("arbitrary",)),
        name="merge",
    )(x, o_r, o_s, o_m, *weights)


def _ffn_chunk(hb, gate_taps_of, cols, wu_ref, cw_ref, cb_ref, wd_ref):
    gate = jnp.dot(hb, wu_ref[:, cols], preferred_element_type=F32)
    val = jnp.dot(hb, wu_ref[:, D_FF + cols.start:D_FF + cols.stop], preferred_element_type=F32)
    taps = gate_taps_of(gate)
    acc = taps[FFN_CONV - 1] * cw_ref[0:1, cols]
    for kk in range(1, FFN_CONV):
        acc = acc + taps[FFN_CONV - 1 - kk] * cw_ref[kk:kk + 1, cols]
    acc = acc + cb_ref[:, cols]
    return jnp.dot(_bf(jax.nn.gelu(acc) * val), wd_ref[cols, :], preferred_element_type=F32), gate


def _ffn_seq_kernel(x_ref, init_ref, gn_ref, wu_ref, cw_ref, cb_ref, wd_ref, gp_ref, o_ref, last_ref, carry):
    @pl.when(pl.program_id(1) == 0)
    def _():
        carry[...] = init_ref[...]

    x = x_ref[...]
    hb = _bf(_rms(x, gn_ref[...]))
    f = None
    for c0, c1 in zip(FFN_COL_STARTS[:-1], FFN_COL_STARTS[1:]):
        cols = slice(c0, c1)
        c8 = carry[:, cols]
        part, gate = _ffn_chunk(hb, lambda g: [g] + [_delayed(g, c8, kk) for kk in range(1, FFN_CONV)], cols,
                                wu_ref, cw_ref, cb_ref, wd_ref)
        tail = gate[gate.shape[0] - SUBLANES:, :]
        carry[:, cols] = tail
        last_ref[:, cols] = tail
        f = part if f is None else f + part
    o_ref[...] = x + _rms(f, gp_ref[...])


def _ffn_step_kernel(x_ref, past_ref, gn_ref, wu_ref, cw_ref, cb_ref, wd_ref, gp_ref, o_ref, gate_ref):
    x = x_ref[...]
    hb = _bf(_rms(x, gn_ref[...]))
    f = None
    for c0, c1 in zip(FFN_COL_STARTS[:-1], FFN_COL_STARTS[1:]):
        cols = slice(c0, c1)
        past = [past_ref[:, (FFN_CONV - 1 - kk) * D_FF + c0:(FFN_CONV - 1 - kk) * D_FF + c1]
                for kk in range(1, FFN_CONV)]
        part, gate = _ffn_chunk(hb, lambda g: [g] + past, cols, wu_ref, cw_ref, cb_ref, wd_ref)
        gate_ref[:, cols] = gate
        f = part if f is None else f + part
    o_ref[...] = x + _rms(f, gp_ref[...])


def _ffn(x, conv_state, layer, weights, seq):
    wspecs = [_layer_const(w, layer) for w in weights]
    if seq:
        bsz, t, d = x.shape
        tm = min(TOKEN_TILE, t)
        tile = pl.BlockSpec((None, tm, d), lambda b, j: (b, j, 0))
        edge = pl.BlockSpec((None, SUBLANES, D_FF), lambda b, j: (b, 0, 0))
        return pl.pallas_call(
            _ffn_seq_kernel,
            grid=(bsz, t // tm),
            in_specs=[tile, edge] + wspecs,
            out_specs=[tile, edge],
            out_shape=[jax.ShapeDtypeStruct((bsz, t, d), F32),
                       jax.ShapeDtypeStruct((bsz, SUBLANES, D_FF), F32)],
            scratch_shapes=[pltpu.VMEM((SUBLANES, D_FF), F32)],
            compiler_params=_cp(("arbitrary", "arbitrary")),
            name="ffn_seq",
        )(x, conv_state, *weights)
    n, d = x.shape
    return pl.pallas_call(
        _ffn_step_kernel,
        grid=(1,),
        in_specs=[_full((n, d)), pl.BlockSpec((None,) + conv_state.shape[1:], lambda i: (layer, 0, 0))] + wspecs,
        out_specs=[_full((n, d)), _full((n, D_FF))],
        out_shape=[jax.ShapeDtypeStruct((n, d), F32), jax.ShapeDtypeStruct((n, D_FF), F32)],
        compiler_params=_cp(("arbitrary",)),
        name="ffn_step",
    )(x, conv_state, *weights)


def _row(v):
    return v.reshape(v.shape[0], 1, -1).astype(F32)


def _all_layer_weights(params):
    (g_pre_mix, g_post_mix, g_pre_ffn, g_post_ffn, w_in,
     r_mu, r_w0, r_w2, r_a0, r_a2, r_g2, r_kk, r_ka, r_rk, r_ln_w, r_ln_b,
     s5_lam_re, s5_lam_im, s5_log_step, s5_b_re, s5_b_im, s5_c_re, s5_c_im, s5_d, s5_w_glu, s5_b_glu,
     m_conv_w, m_conv_b, m_dt_bias, m_a_log, m_d, m_norm_w,
     w_branch, w_out, w_up, f_conv_w, f_conv_b, w_down) = params
    depth = w_in.shape[0]
    c0 = R_COLS
    c1 = c0 + WIDTH
    c2 = c1 + WIDTH + M_CONV_CH
    c3 = c2 + M_HEADS
    every_layer = lambda m: jnp.broadcast_to(m[None], (depth,) + m.shape)
    head = jnp.arange(WIDTH) // R_HEAD
    ones_bd = every_layer((head[:, None] == head[None, :]).astype(BF16))
    gn = _row(g_pre_mix)
    w = {"g_pre_mix": gn}
    w_r = _bf(w_in[:, :, :c0])
    w_m = _bf(jnp.concatenate(
        [w_in[:, :, c1 + WIDTH:c2], w_in[:, :, c1:c1 + WIDTH], w_in[:, :, c2:c3],
         jnp.zeros((depth, D_MODEL, M_DT_PAD - M_HEADS), F32)], axis=2))
    w["w_r"], w["w_m"] = w_r, w_m
    rwkv_mix = (_row(r_mu), _row(r_w0), _bf(r_w2), _row(r_a0), _bf(r_a2), _bf(r_g2), _row(r_kk), _row(r_ka),
                ones_bd, _row(r_rk), _row(r_ln_w), _row(r_ln_b))
    w["rwkv_seq"] = (gn, w_r) + rwkv_mix
    w["rwkv_step"] = rwkv_mix
    ab_re, ab_im, bb_re, bb_im = _s5_params(s5_lam_re, s5_lam_im, s5_log_step, s5_b_re, s5_b_im)
    gpb = S5_GROUPS // S5_BLOCKS
    eye = jnp.eye(gpb, dtype=F32)
    blocked = lambda m: m.reshape(depth, S5_BLOCKS, gpb, S5_GROUP, S5_STATE)
    to_state = lambda bb: _bf(jnp.einsum('lqghp,gk->lqghkp', blocked(bb), eye)
                              .reshape(depth, S5_BLOCKS, S5_BLOCK_IN, S5_BLOCK_STATE))
    from_state = lambda c: _bf(jnp.einsum('lqghp,gk->lqgpkh', blocked(c), eye)
                               .reshape(depth, S5_BLOCKS, S5_BLOCK_STATE, S5_BLOCK_IN))
    w["s5"] = (gn, _bf(w_in[:, :, c0:c1]), to_state(bb_re), to_state(bb_im), from_state(s5_c_re),
               from_state(s5_c_im), ab_re.reshape(depth, 1, S5_LANES), ab_im.reshape(depth, 1, S5_LANES), _row(s5_d),
               _bf(s5_w_glu), _row(s5_b_glu))
    conv = (jnp.swapaxes(m_conv_w, 1, 2).astype(F32), _row(m_conv_b),
            _row(jnp.pad(m_dt_bias, ((0, 0), (0, M_DT_PAD - M_HEADS)))))
    alog_row = _row(jnp.repeat(m_a_log, M_HEAD, axis=1))
    expand = every_layer((jnp.arange(M_DT_PAD)[:, None] == (jnp.arange(WIDTH) // M_HEAD)[None, :]).astype(BF16))
    tail = (_row(jnp.repeat(m_d, M_HEAD, axis=1)), _row(m_norm_w), expand)
    w["mamba_seq"] = (gn, w_m) + conv + (alog_row, m_a_log.reshape(depth, M_HEADS, 1).astype(F32)) + tail
    w["mamba_step"] = conv + (alog_row,) + tail
    w["merge"] = (gn, _bf(w_in[:, :, c3:]), _bf(w_branch), _bf(w_out), _row(g_post_mix))
    w["ffn"] = (_row(g_pre_ffn), _bf(w_up), jnp.swapaxes(f_conv_w, 1, 2).astype(F32), _row(f_conv_b), _bf(w_down),
                _row(g_post_ffn))
    return w


def _zero_states(n):
    return dict(shift=jnp.zeros((n, SUBLANES, R_COLS), F32),
                wkv=jnp.zeros((n, R_HEADS, R_HEAD, R_HEAD), F32),
                s5=jnp.zeros((n, S5_LANES), F32),
                ssd=jnp.zeros((n, M_HEADS, M_HEAD, M_STATE), F32),
                mconv=jnp.zeros((n, SUBLANES, M_CONV_CH), F32),
                fconv=jnp.zeros((n, SUBLANES, D_FF), F32))


def _sequence_layer(x, init, w, layer):
    bsz, t, d = x.shape
    flat = lambda a: a.reshape(bsz * t, a.shape[-1])
    o_r, last_r, wkv1 = _rwkv_seq(x, init["shift"], init["wkv"], layer, w["rwkv_seq"])
    o_s, s5r1, s5i1 = _s5(x, init["s5"], init["s5"], layer, w["s5"], True)
    o_m, last_m, ssd1 = _mamba_seq(x, init["mconv"], init["ssd"], layer, w["mamba_seq"])
    mixed = _merge(flat(x), flat(o_r), flat(o_s), flat(o_m), layer, w["merge"])
    x, last_f = _ffn(mixed.reshape(bsz, t, d), init["fconv"], layer, w["ffn"], True)
    states = (last_r[:, SUBLANES - 1], wkv1, s5r1.reshape(bsz, S5_GROUPS, S5_STATE),
              s5i1.reshape(bsz, S5_GROUPS, S5_STATE), ssd1, last_m[:, SUBLANES - (M_CONV - 1):],
              last_f[:, SUBLANES - (FFN_CONV - 1):])
    return x, states


def _step_layer(x, cache, w, layer):
    shift_all, wkv_all, s5r_all, s5i_all, ssd_all, mconv_all, fconv_all = cache
    z_r = _norm_matmul(x, w["g_pre_mix"], w["w_r"], layer, "in_proj_rwkv")
    z_m = _norm_matmul(x, w["g_pre_mix"], w["w_m"], layer, "in_proj_mamba")
    o_r, wkv1 = _rwkv_step(z_r, shift_all, wkv_all, layer, w["rwkv_step"])
    o_s, s5r1, s5i1 = _s5(x, s5r_all, s5i_all, layer, w["s5"], False)
    o_m, ssd1 = _mamba_step(z_m, mconv_all, ssd_all, layer, w["mamba_step"])
    x = _merge(x, o_r, o_s, o_m, layer, w["merge"])
    x, gate = _ffn(x, fconv_all, layer, w["ffn"], False)
    return x, (z_r, wkv1, s5r1, s5i1, ssd1, z_m[:, :M_CONV_CH], gate)


def kernel(x_prompt, x_sample, state_rwkv_shift, state_rwkv_wkv, state_s5_re, state_s5_im, state_ssd, state_ssd_conv, state_ffn_conv, g_pre_mix, g_post_mix, g_pre_ffn, g_post_ffn, w_in, r_mu, r_w0, r_w2, r_a0, r_a2, r_g2, r_kk, r_ka, r_rk, r_ln_w, r_ln_b, s5_lam_re, s5_lam_im, s5_log_step, s5_b_re, s5_b_im, s5_c_re, s5_c_im, s5_d, s5_w_glu, s5_b_glu, m_conv_w, m_conv_b, m_dt_bias, m_a_log, m_d, m_norm_w, w_branch, w_out, w_up, f_conv_w, f_conv_b, w_down):
    stacked = (g_pre_mix, g_post_mix, g_pre_ffn, g_post_ffn, w_in,
               r_mu, r_w0, r_w2, r_a0, r_a2, r_g2, r_kk, r_ka, r_rk, r_ln_w, r_ln_b,
               s5_lam_re, s5_lam_im, s5_log_step, s5_b_re, s5_b_im, s5_c_re, s5_c_im, s5_d,
               s5_w_glu, s5_b_glu,
               m_conv_w, m_conv_b, m_dt_bias, m_a_log, m_d, m_norm_w,
               w_branch, w_out, w_up, f_conv_w, f_conv_b, w_down)
    depth = w_in.shape[0]
    pb, pt, d = x_prompt.shape
    sb, s_t, _ = x_sample.shape
    w = _all_layer_weights(stacked)
    per_seq = lambda a: a.reshape(depth, sb, -1)
    cache = (state_rwkv_shift, state_rwkv_wkv, per_seq(state_s5_re), per_seq(state_s5_im), state_ssd,
             per_seq(state_ssd_conv), per_seq(state_ffn_conv))
    init = _zero_states(pb)
    xp = x_prompt
    xs = x_sample.reshape(sb * s_t, d)
    p_states, s_states = [], []
    for l in range(depth):
        xp, sp = _sequence_layer(xp, init, w, l)
        xs, ss = _step_layer(xs, cache, w, l)
        p_states.append(sp)
        s_states.append(ss)
    p_shift, p_wkv, p_s5_re, p_s5_im, p_ssd, p_ssd_conv, p_ffn_conv = (jnp.stack(v, 0) for v in zip(*p_states))
    z_r, s_wkv, s5r, s5i, s_ssd, raw_xbc, gate = (jnp.stack(v, 0) for v in zip(*s_states))
    s_s5_re = s5r.reshape(depth, sb, S5_GROUPS, S5_STATE)
    s_s5_im = s5i.reshape(depth, sb, S5_GROUPS, S5_STATE)
    s_ssd_conv = jnp.concatenate([state_ssd_conv[:, :, 1:], raw_xbc[:, :, None]], axis=2)
    s_ffn_conv = jnp.concatenate([state_ffn_conv[:, :, 1:], gate[:, :, None]], axis=2)
    return (xp, xs.reshape(sb, s_t, d), p_shift, z_r, p_wkv, s_wkv, p_s5_re, s_s5_re, p_s5_im, s_s5_im,
            p_ssd, s_ssd, p_ssd_conv, s_ssd_conv, p_ffn_conv, s_ffn_conv)
```
